```python
import jax, jax.numpy as jnp
from jax import lax
import numpy as np

D_MODEL = 2048
BATCH = 4
SEQ = 4096
DEPTH = 1

CHUNK = 64
N_LEFT_CHUNKS = 8
BAND = (N_LEFT_CHUNKS + 1) * CHUNK

ATT_WIDTH = D_MODEL // 2
ATT_HEADS = 8
ATT_HEAD_DIM = ATT_WIDTH // ATT_HEADS
MAX_REL = 128
REL_SIZE = 2 * MAX_REL + 1

GRN_WIDTH = D_MODEL - ATT_WIDTH
GRN_HEADS = 8
GRN_KEY_DIM = GRN_WIDTH // GRN_HEADS
GRN_VAL_DIM = GRN_WIDTH // GRN_HEADS

MIX_WIDTH = ATT_WIDTH + GRN_WIDTH
IN_COLS = 3 * ATT_WIDTH + 4 * GRN_WIDTH
D_FF = -(-8 * D_MODEL // (3 * 256)) * 256
EPS = 1e-6

kernel_name = "hybrid_chunk_attn_hgrn2_block"


def rmsnorm(x, gain):
    xf = x.astype(jnp.float32)
    y = xf * lax.rsqrt(jnp.mean(xf * xf, axis=-1, keepdims=True) + EPS)
    return (y * gain.astype(jnp.float32)).astype(x.dtype)


def chunk_band_attention(q, k, v, rel_bias):
    B, S, H, Dh = q.shape
    n_chunks = S // CHUNK
    qc = q.reshape(B, n_chunks, CHUNK, H, Dh)
    pad = ((0, 0), (N_LEFT_CHUNKS * CHUNK, 0), (0, 0), (0, 0))
    kp = jnp.pad(k, pad).reshape(B, n_chunks + N_LEFT_CHUNKS, CHUNK, H, Dh)
    vp = jnp.pad(v, pad).reshape(B, n_chunks + N_LEFT_CHUNKS, CHUNK, H, Dh)
    band_idx = jnp.arange(n_chunks)[:, None] + jnp.arange(N_LEFT_CHUNKS + 1)[None, :]
    kb = kp[:, band_idx].reshape(B, n_chunks, BAND, H, Dh)
    vb = vp[:, band_idx].reshape(B, n_chunks, BAND, H, Dh)
    scores = jnp.einsum('bcqhd,bckhd->bhcqk', qc, kb,
                        preferred_element_type=jnp.float32) * (Dh ** -0.5)
    dist = N_LEFT_CHUNKS * CHUNK + jnp.arange(CHUNK)[:, None] - jnp.arange(BAND)[None, :]
    rel_idx = jnp.clip(dist, -MAX_REL, MAX_REL) + MAX_REL
    bias = rel_bias.astype(jnp.float32)[:, rel_idx]
    scores = scores + bias[None, :, None]
    key_chunk = (jnp.arange(n_chunks)[:, None] - N_LEFT_CHUNKS
                 + jnp.arange(BAND)[None, :] // CHUNK)
    valid = key_chunk >= 0
    scores = jnp.where(valid[None, None, :, None, :], scores, jnp.finfo(jnp.float32).min)
    p = jax.nn.softmax(scores, axis=-1).astype(v.dtype)
    out = jnp.einsum('bhcqk,bckhd->bcqhd', p, vb)
    return out.reshape(B, S, H, Dh)


def hgrn2_mixer(q_raw, f_raw, i_val, g_raw, lb, gnorm_gain):
    B, S, _ = q_raw.shape
    n_chunks = S // CHUNK
    dtype = q_raw.dtype
    q = jax.nn.silu(q_raw.astype(jnp.float32))
    lbf = lb.astype(jnp.float32)
    f = lbf + (1.0 - lbf) * jax.nn.sigmoid(f_raw.astype(jnp.float32))
    k = 1.0 - f
    shp_k = (B, n_chunks, CHUNK, GRN_HEADS, GRN_KEY_DIM)
    q = q.reshape(shp_k)
    k = k.reshape(shp_k)
    b = jnp.cumsum(jnp.log(f).reshape(shp_k), axis=2)
    b_last = b[:, :, -1:]
    v = i_val.astype(jnp.float32).reshape(B, n_chunks, CHUNK, GRN_HEADS, GRN_VAL_DIM)
    q_dec = q * jnp.exp(b)
    a = jnp.einsum('bcthk,bcshk->bchts', q_dec, k * jnp.exp(-b))
    causal = jnp.tril(jnp.ones((CHUNK, CHUNK), dtype=bool))
    a = jnp.where(causal, a, 0.0)
    o_intra = jnp.einsum('bchts,bcshv->bcthv', a, v)
    u = jnp.einsum('bcshk,bcshv->bchkv', k * jnp.exp(b_last - b), v)
    decay = jnp.exp(b_last[:, :, 0])

    def step(state, inp):
        u_c, d_c = inp
        return d_c[..., None] * state + u_c, state

    s0 = jnp.zeros((B, GRN_HEADS, GRN_KEY_DIM, GRN_VAL_DIM), jnp.float32)
    _, s_start = lax.scan(step, s0, (jnp.swapaxes(u, 0, 1), jnp.swapaxes(decay, 0, 1)))
    s_start = jnp.swapaxes(s_start, 0, 1)
    o_inter = jnp.einsum('bcthk,bchkv->bcthv', q_dec, s_start)
    o = (o_intra + o_inter).reshape(B, S, GRN_HEADS, GRN_VAL_DIM)
    o = o * lax.rsqrt(jnp.mean(o * o, axis=-1, keepdims=True) + EPS) * gnorm_gain.astype(jnp.float32)
    o = o.reshape(B, S, GRN_HEADS * GRN_VAL_DIM) * jax.nn.silu(g_raw.astype(jnp.float32))
    return o.astype(dtype)


def setup_inputs(seed: int = 0) -> dict:
    key = jax.random.key(seed)
    ks = jax.random.split(key, 16)
    f32 = jnp.float32
    x = jax.random.normal(ks[0], (BATCH, SEQ, D_MODEL), f32)
    norm1_gain = 1.0 + 0.02 * jax.random.normal(ks[1], (DEPTH, D_MODEL), f32)
    w_in = jax.random.normal(ks[2], (DEPTH, D_MODEL, IN_COLS), f32) * D_MODEL ** -0.5
    rel_bias = 0.1 * jax.random.normal(ks[3], (DEPTH, ATT_HEADS, REL_SIZE), f32)
    lower_bounds = 0.1 * jax.random.normal(ks[4], (DEPTH + 1, GRN_WIDTH), f32)
    grn_norm_gain = 1.0 + 0.02 * jax.random.normal(ks[5], (DEPTH, GRN_VAL_DIM), f32)
    attn_out_gain = 1.0 + 0.02 * jax.random.normal(ks[6], (DEPTH, ATT_WIDTH), f32)
    w_out = jax.random.normal(ks[7], (DEPTH, MIX_WIDTH, D_MODEL), f32) * MIX_WIDTH ** -0.5
    norm2_gain = 1.0 + 0.02 * jax.random.normal(ks[8], (DEPTH, D_MODEL), f32)
    w_gate = jax.random.normal(ks[9], (DEPTH, D_MODEL, D_FF), f32) * D_MODEL ** -0.5
    w_up = jax.random.normal(ks[10], (DEPTH, D_MODEL, D_FF), f32) * D_MODEL ** -0.5
    w_down = jax.random.normal(ks[11], (DEPTH, D_FF, D_MODEL), f32) * D_FF ** -0.5
    final_gain = 1.0 + 0.02 * jax.random.normal(ks[12], (D_MODEL,), f32)
    return {"x": x, "norm1_gain": norm1_gain, "w_in": w_in, "rel_bias": rel_bias,
            "lower_bounds": lower_bounds, "grn_norm_gain": grn_norm_gain,
            "attn_out_gain": attn_out_gain, "w_out": w_out, "norm2_gain": norm2_gain,
            "w_gate": w_gate, "w_up": w_up, "w_down": w_down, "final_gain": final_gain}


def reference(x, norm1_gain, w_in, rel_bias, lower_bounds, grn_norm_gain, attn_out_gain,
              w_out, norm2_gain, w_gate, w_up, w_down, final_gain):
    B, S, _ = x.shape
    lb_all = jnp.cumsum(jax.nn.softmax(lower_bounds.astype(jnp.float32), axis=0), axis=0)
    for l in range(DEPTH):
        h = rmsnorm(x, norm1_gain[l])
        proj = h @ w_in[l]
        o1 = ATT_WIDTH
        q_a = proj[..., 0:o1].reshape(B, S, ATT_HEADS, ATT_HEAD_DIM)
        k_a = proj[..., o1:2 * o1].reshape(B, S, ATT_HEADS, ATT_HEAD_DIM)
        v_a = proj[..., 2 * o1:3 * o1].reshape(B, S, ATT_HEADS, ATT_HEAD_DIM)
        base = 3 * ATT_WIDTH
        q_r = proj[..., base:base + GRN_WIDTH]
        f_r = proj[..., base + GRN_WIDTH:base + 2 * GRN_WIDTH]
        i_r = proj[..., base + 2 * GRN_WIDTH:base + 3 * GRN_WIDTH]
        g_r = proj[..., base + 3 * GRN_WIDTH:base + 4 * GRN_WIDTH]
        attn = chunk_band_attention(q_a, k_a, v_a, rel_bias[l]).reshape(B, S, ATT_WIDTH)
        attn = rmsnorm(attn, attn_out_gain[l])
        rec = hgrn2_mixer(q_r, f_r, i_r, g_r, lb_all[l], grn_norm_gain[l])
        mixed = jnp.concatenate([attn, rec.astype(attn.dtype)], axis=-1)
        x = x + (mixed @ w_out[l]).astype(x.dtype)
        h2 = rmsnorm(x, norm2_gain[l])
        ff = jax.nn.silu(h2 @ w_gate[l]) * (h2 @ w_up[l])
        x = x + (ff @ w_down[l]).astype(x.dtype)
    return rmsnorm(x, final_gain)
```

```python
import functools

import jax
import jax.numpy as jnp
from jax import lax
from jax.experimental import pallas as pl
from jax.experimental.pallas import tpu as pltpu

F32 = jnp.float32
BF16 = jnp.bfloat16

EPS = 1e-6
CHUNK = 64
N_LEFT = 8
MAX_REL = 128
HEADS = 8
HEAD_DIM = 128
GROUP = HEADS * HEAD_DIM

VMEM_LIMIT_BYTES = 56 * 1024 * 1024
NEG = -1e30


def _params(*sem):
    return pltpu.CompilerParams(dimension_semantics=sem, vmem_limit_bytes=VMEM_LIMIT_BYTES)


def _rms_rows(x, gain):
    ms = jnp.mean(x * x, axis=-1, keepdims=True)
    return x * lax.rsqrt(ms + EPS) * gain


ROW_STEP = 128


def _norm_rows_to(h_ref, x_ref, gain_ref):
    gain = gain_ref[...]

    def body(i, carry):
        rows = pl.ds(pl.multiple_of(i * ROW_STEP, ROW_STEP), ROW_STEP)
        h_ref[rows, :] = _rms_rows(x_ref[rows, :], gain).astype(h_ref.dtype)
        return carry

    lax.fori_loop(0, x_ref.shape[0] // ROW_STEP, body, 0)


def _inproj_kernel(x_ref, gain_ref, w_ref, o_ref, h_ref):
    @pl.when(pl.program_id(1) == 0)
    def _():
        _norm_rows_to(h_ref, x_ref, gain_ref)

    o_ref[...] = jnp.dot(h_ref[...], w_ref[...], preferred_element_type=F32).astype(o_ref.dtype)


def _inproj(x2d, gain, w_bf16, tm, tn):
    t, d = x2d.shape
    n = w_bf16.shape[1]
    return pl.pallas_call(
        _inproj_kernel,
        grid=(t // tm, n // tn),
        in_specs=[
            pl.BlockSpec((tm, d), lambda i, j: (i, 0)),
            pl.BlockSpec((1, d), lambda i, j: (0, 0)),
            pl.BlockSpec((d, tn), lambda i, j: (0, j)),
        ],
        out_specs=pl.BlockSpec((tm, tn), lambda i, j: (i, j)),
        out_shape=jax.ShapeDtypeStruct((t, n), BF16),
        scratch_shapes=[pltpu.VMEM((tm, d), BF16)],
        compiler_params=_params("arbitrary", "arbitrary"),
        name="inproj",
    )(x2d, gain, w_bf16)


QB = 256
N_SEG = 1 + (N_LEFT * CHUNK) // QB
WIN = N_SEG * QB
ROLL_W = 1024


def _bias_rows(rel_bias):
    h = rel_bias.shape[0]
    far_past = rel_bias[:, 2 * MAX_REL:2 * MAX_REL + 1]
    far_future = rel_bias[:, 0:1]
    n_head = (WIN - QB) - MAX_REL
    n_tail = WIN - n_head - (2 * MAX_REL + 1)
    return jnp.concatenate([
        jnp.broadcast_to(far_past, (h, n_head)),
        rel_bias[:, ::-1],
        jnp.broadcast_to(far_future, (h, n_tail)),
        jnp.broadcast_to(far_past, (h, ROLL_W - WIN)),
    ], axis=1).astype(F32)


def _attn_kernel(q_ref, k0_ref, k1_ref, k2_ref, v0_ref, v1_ref, v2_ref, brow_ref, gain_ref,
                 o_ref, bias_ref, acc_ref):
    qb = pl.program_id(1)

    @pl.when((pl.program_id(0) == 0) & (qb == 0))
    def _():
        qc = lax.broadcasted_iota(jnp.int32, (QB, WIN), 0) // CHUNK
        kc = lax.broadcasted_iota(jnp.int32, (QB, WIN), 1) // CHUNK
        band = (kc >= qc) & (kc <= qc + N_LEFT)
        for h in range(HEADS):
            row = jnp.broadcast_to(brow_ref[h:h + 1, :], (QB, ROLL_W))
            toeplitz = pltpu.roll(row, 0, 1, stride=1, stride_axis=0)
            bias_ref[h] = jnp.where(band, toeplitz[:, :WIN], NEG)

    first_valid = jnp.maximum(N_SEG - 1 - qb, 0) * QB
    col = lax.broadcasted_iota(jnp.int32, (QB, QB), 1)
    k_refs = (k0_ref, k1_ref, k2_ref)
    v_refs = (v0_ref, v1_ref, v2_ref)
    scale = HEAD_DIM ** -0.5
    ss = jnp.zeros((QB, 1), F32)
    for h in range(HEADS):
        sl = slice(h * HEAD_DIM, (h + 1) * HEAD_DIM)
        qh = q_ref[:, sl]
        s = []
        for j in range(N_SEG):
            sj = lax.dot_general(qh, k_refs[j][:, sl], (((1,), (1,)), ((), ())),
                                 preferred_element_type=F32)
            sj = sj * scale + bias_ref[h, :, j * QB:(j + 1) * QB]
            s.append(jnp.where(col + j * QB >= first_valid, sj, NEG))
        m = s[0].max(axis=-1, keepdims=True)
        for j in range(1, N_SEG):
            m = jnp.maximum(m, s[j].max(axis=-1, keepdims=True))
        l = jnp.zeros((QB, 1), F32)
        o = jnp.zeros((QB, HEAD_DIM), F32)
        for j in range(N_SEG):
            p = jnp.exp(s[j] - m)
            l = l + p.sum(axis=-1, keepdims=True)
            o = o + jnp.dot(p.astype(BF16), v_refs[j][:, sl], preferred_element_type=F32)
        o = o / l
        acc_ref[:, sl] = o
        ss = ss + (o * o).sum(axis=-1, keepdims=True)
    inv = lax.rsqrt(ss / GROUP + EPS)
    o_ref[...] = (acc_ref[...] * inv * gain_ref[...]).astype(o_ref.dtype)


def _attention(proj, brow, gain, batch, seq):
    t = proj.shape[0]
    nqb = seq // QB

    def kv_spec(col_block, seg):
        def index(b, i):
            return (b * nqb + jnp.maximum(i - (N_SEG - 1 - seg), 0), col_block)
        return pl.BlockSpec((QB, GROUP), index)

    return pl.pallas_call(
        _attn_kernel,
        grid=(batch, nqb),
        in_specs=[pl.BlockSpec((QB, GROUP), lambda b, i: (b * nqb + i, 0))]
        + [kv_spec(1, s) for s in range(N_SEG)]
        + [kv_spec(2, s) for s in range(N_SEG)]
        + [pl.BlockSpec((HEADS, ROLL_W), lambda b, i: (0, 0)),
           pl.BlockSpec((1, GROUP), lambda b, i: (0, 0))],
        out_specs=pl.BlockSpec((QB, GROUP), lambda b, i: (b * nqb + i, 0)),
        out_shape=jax.ShapeDtypeStruct((t, GROUP), BF16),
        scratch_shapes=[pltpu.VMEM((HEADS, QB, WIN), F32), pltpu.VMEM((QB, GROUP), F32)],
        compiler_params=_params("arbitrary", "arbitrary"),
        name="attention",
    )(proj, proj, proj, proj, proj, proj, proj, brow, gain)


TB = 512


def _split3(x):
    hi = x.astype(BF16)
    r1 = x - hi.astype(F32)
    mid = r1.astype(BF16)
    lo = (r1 - mid.astype(F32)).astype(BF16)
    return hi, mid, lo


def _hgrn_kernel(q_ref, f_ref, i_ref, g_ref, lb_ref, gn_ref, o_ref, st_ref):
    @pl.when(pl.program_id(1) == 0)
    def _():
        st_ref[...] = jnp.zeros_like(st_ref)

    lb = lb_ref[...]
    gn = gn_ref[...]
    r_i = lax.broadcasted_iota(jnp.int32, (CHUNK, CHUNK), 0)
    c_i = lax.broadcasted_iota(jnp.int32, (CHUNK, CHUNK), 1)
    causal = r_i >= c_i
    tril = causal.astype(BF16)

    def chunk(c, carry):
        rows = pl.ds(pl.multiple_of(c * CHUNK, CHUNK), CHUNK)
        qr = q_ref[rows, :].astype(F32)
        fr = f_ref[rows, :].astype(F32)
        q = qr * jax.nn.sigmoid(qr)
        f = lb + (1.0 - lb) * jax.nn.sigmoid(fr)
        k = 1.0 - f
        logf = jnp.log(f)
        b = sum(jnp.dot(tril, part, preferred_element_type=F32) for part in _split3(logf))
        b_last = b[CHUNK - 1:CHUNK, :]
        q_dec = (q * jnp.exp(b)).astype(BF16)
        k_intra = (k * jnp.exp(-b)).astype(BF16)
        k_state = (k * jnp.exp(b_last - b)).astype(BF16)
        decay = jnp.exp(b_last)
        for h in range(HEADS):
            sl = slice(h * HEAD_DIM, (h + 1) * HEAD_DIM)
            v = i_ref[rows, sl]
            qd = q_dec[:, sl]
            a = lax.dot_general(qd, k_intra[:, sl], (((1,), (1,)), ((), ())),
                                preferred_element_type=F32)
            a = jnp.where(causal, a, 0.0).astype(BF16)
            st = st_ref[h]
            o = jnp.dot(a, v, preferred_element_type=F32)
            o = o + lax.dot_general(qd, st.astype(BF16), (((1,), (1,)), ((), ())),
                                    preferred_element_type=F32)
            u_t = lax.dot_general(v, k_state[:, sl], (((0,), (0,)), ((), ())),
                                  preferred_element_type=F32)
            st_ref[h] = st * decay[:, sl] + u_t
            o = _rms_rows(o, gn)
            gr = g_ref[rows, sl].astype(F32)
            o_ref[rows, sl] = (o * (gr * jax.nn.sigmoid(gr))).astype(o_ref.dtype)
        return carry

    lax.fori_loop(0, TB // CHUNK, chunk, 0)


def _hgrn(proj, lb, gn, batch, seq):
    t = proj.shape[0]
    nb = seq // TB

    def col(cb):
        return pl.BlockSpec((TB, GROUP), lambda b, i: (b * nb + i, cb))

    return pl.pallas_call(
        _hgrn_kernel,
        grid=(batch, nb),
        in_specs=[col(3), col(4), col(5), col(6),
                  pl.BlockSpec((1, GROUP), lambda b, i: (0, 0)),
                  pl.BlockSpec((1, HEAD_DIM), lambda b, i: (0, 0))],
        out_specs=pl.BlockSpec((TB, GROUP), lambda b, i: (b * nb + i, 0)),
        out_shape=jax.ShapeDtypeStruct((t, GROUP), BF16),
        scratch_shapes=[pltpu.VMEM((HEADS, HEAD_DIM, HEAD_DIM), F32)],
        compiler_params=_params("arbitrary", "arbitrary"),
        name="hgrn2",
    )(proj, proj, proj, proj, lb, gn)


def _outproj_kernel(x_ref, a_ref, r_ref, wa_ref, wr_ref, o_ref):
    y = jnp.dot(a_ref[...], wa_ref[...], preferred_element_type=F32)
    y = y + jnp.dot(r_ref[...], wr_ref[...], preferred_element_type=F32)
    o_ref[...] = x_ref[...] + y


def _outproj(x2d, attn, rec, w_bf16, tm, tn):
    t, d = x2d.shape
    g = attn.shape[1]
    return pl.pallas_call(
        _outproj_kernel,
        grid=(t // tm, d // tn),
        in_specs=[
            pl.BlockSpec((tm, tn), lambda i, j: (i, j)),
            pl.BlockSpec((tm, g), lambda i, j: (i, 0)),
            pl.BlockSpec((tm, g), lambda i, j: (i, 0)),
            pl.BlockSpec((g, tn), lambda i, j: (0, j)),
            pl.BlockSpec((g, tn), lambda i, j: (1, j)),
        ],
        out_specs=pl.BlockSpec((tm, tn), lambda i, j: (i, j)),
        out_shape=jax.ShapeDtypeStruct((t, d), F32),
        compiler_params=_params("arbitrary", "arbitrary"),
        name="outproj",
    )(x2d, attn, rec, w_bf16, w_bf16)


def _ffn_kernel(x_ref, g2_ref, wg_ref, wu_ref, wd_ref, gf_ref, o_ref, h_ref):
    j = pl.program_id(1)

    @pl.when(j == 0)
    def _():
        _norm_rows_to(h_ref, x_ref, g2_ref)

    h = h_ref[...]
    gate = jnp.dot(h, wg_ref[...], preferred_element_type=F32)
    up = jnp.dot(h, wu_ref[...], preferred_element_type=F32)
    act = (gate * jax.nn.sigmoid(gate) * up).astype(BF16)
    y = jnp.dot(act, wd_ref[...], preferred_element_type=F32)

    @pl.when(j == 0)
    def _():
        o_ref[...] = y

    @pl.when(j > 0)
    def _():
        o_ref[...] += y

    @pl.when(j == pl.num_programs(1) - 1)
    def _():
        gf = gf_ref[...]

        def body(i, carry):
            rows = pl.ds(pl.multiple_of(i * ROW_STEP, ROW_STEP), ROW_STEP)
            o_ref[rows, :] = _rms_rows(x_ref[rows, :] + o_ref[rows, :], gf)
            return carry

        lax.fori_loop(0, x_ref.shape[0] // ROW_STEP, body, 0)


def _ffn(x1, g2, wg, wu, wd, gf, tm, tf):
    t, d = x1.shape
    ff = wg.shape[1]
    return pl.pallas_call(
        _ffn_kernel,
        grid=(t // tm, ff // tf),
        in_specs=[
            pl.BlockSpec((tm, d), lambda i, j: (i, 0)),
            pl.BlockSpec((1, d), lambda i, j: (0, 0)),
            pl.BlockSpec((d, tf), lambda i, j: (0, j)),
            pl.BlockSpec((d, tf), lambda i, j: (0, j)),
            pl.BlockSpec((tf, d), lambda i, j: (j, 0)),
            pl.BlockSpec((1, d), lambda i, j: (0, 0)),
        ],
        out_specs=pl.BlockSpec((tm, d), lambda i, j: (i, 0)),
        out_shape=jax.ShapeDtypeStruct((t, d), F32),
        scratch_shapes=[pltpu.VMEM((tm, d), BF16)],
        compiler_params=_params("arbitrary", "arbitrary"),
        name="ffn",
    )(x1, g2, wg, wu, wd, gf)


def kernel(x, norm1_gain, w_in, rel_bias, lower_bounds, grn_norm_gain, attn_out_gain, w_out,
           norm2_gain, w_gate, w_up, w_down, final_gain):
    batch, seq, d = x.shape
    depth = w_in.shape[0]
    assert seq % TB == 0 and seq % QB == 0 and w_in.shape[2] == 7 * GROUP and d == 2 * GROUP
    lb_all = jnp.cumsum(jax.nn.softmax(lower_bounds.astype(F32), axis=0), axis=0)
    xf = x.reshape(batch * seq, d)
    for l in range(depth):
        proj = _inproj(xf, norm1_gain[l][None, :], w_in[l].astype(BF16), tm=1024, tn=1024)
        attn = _attention(proj, _bias_rows(rel_bias[l]), attn_out_gain[l][None, :], batch, seq)
        rec = _hgrn(proj, lb_all[l][None, :], grn_norm_gain[l][None, :], batch, seq)
        x1 = _outproj(xf, attn, rec, w_out[l].astype(BF16), tm=1024, tn=1024)
        last = l == depth - 1
        assert last, "the final rmsnorm is fused into the last layer's FFN"
        xf = _ffn(x1, norm2_gain[l][None, :], w_gate[l].astype(BF16), w_up[l].astype(BF16),
                  w_down[l].astype(BF16), final_gain[None, :], tm=512, tf=512)
    return xf.reshape(batch, seq, d)
```

```python
import functools

import jax
import jax.numpy as jnp
from jax import lax
from jax.experimental import pallas as pl
from jax.experimental.pallas import tpu as pltpu

F32 = jnp.float32
BF16 = jnp.bfloat16

EPS = 1e-6
CHUNK = 64
N_LEFT = 8
MAX_REL = 128
HEADS = 8
HEAD_DIM = 128
GROUP = HEADS * HEAD_DIM

VMEM_LIMIT_BYTES = 56 * 1024 * 1024
NEG = -1e30


def _params(*sem):
    return pltpu.CompilerParams(dimension_semantics=sem, vmem_limit_bytes=VMEM_LIMIT_BYTES)


def _rms_rows(x, gain):
    ms = jnp.mean(x * x, axis=-1, keepdims=True)
    return x * lax.rsqrt(ms + EPS) * gain


ROW_STEP = 128


def _norm_rows_to(h_ref, x_ref, gain_ref):
    gain = gain_ref[...]

    def body(i, carry):
        rows = pl.ds(pl.multiple_of(i * ROW_STEP, ROW_STEP), ROW_STEP)
        h_ref[rows, :] = _rms_rows(x_ref[rows, :], gain).astype(h_ref.dtype)
        return carry

    lax.fori_loop(0, x_ref.shape[0] // ROW_STEP, body, 0)


def _inproj_kernel(x_ref, gain_ref, w_ref, o_ref, h_ref):
    @pl.when(pl.program_id(1) == 0)
    def _():
        _norm_rows_to(h_ref, x_ref, gain_ref)

    o_ref[...] = jnp.dot(h_ref[...], w_ref[...], preferred_element_type=F32).astype(o_ref.dtype)


def _inproj(x2d, gain, w_bf16, tm, tn):
    t, d = x2d.shape
    n = w_bf16.shape[1]
    return pl.pallas_call(
        _inproj_kernel,
        grid=(t // tm, n // tn),
        in_specs=[
            pl.BlockSpec((tm, d), lambda i, j: (i, 0)),
            pl.BlockSpec((1, d), lambda i, j: (0, 0)),
            pl.BlockSpec((d, tn), lambda i, j: (0, j)),
        ],
        out_specs=pl.BlockSpec((tm, tn), lambda i, j: (i, j)),
        out_shape=jax.ShapeDtypeStruct((t, n), BF16),
        scratch_shapes=[pltpu.VMEM((tm, d), BF16)],
        compiler_params=_params("arbitrary", "arbitrary"),
        name="inproj",
    )(x2d, gain, w_bf16)


QB = 256
N_SEG = 1 + (N_LEFT * CHUNK) // QB
WIN = N_SEG * QB
ROLL_W = 1024


def _bias_rows(rel_bias):
    h = rel_bias.shape[0]
    far_past = rel_bias[:, 2 * MAX_REL:2 * MAX_REL + 1]
    far_future = rel_bias[:, 0:1]
    n_head = (WIN - QB) - MAX_REL
    n_tail = WIN - n_head - (2 * MAX_REL + 1)
    return jnp.concatenate([
        jnp.broadcast_to(far_past, (h, n_head)),
        rel_bias[:, ::-1],
        jnp.broadcast_to(far_future, (h, n_tail)),
        jnp.broadcast_to(far_past, (h, ROLL_W - WIN)),
    ], axis=1).astype(F32)


def _attn_kernel(q_ref, k0_ref, k1_ref, k2_ref, v0_ref, v1_ref, v2_ref, brow_ref, gain_ref,
                 o_ref, bias_ref, acc_ref):
    qb = pl.program_id(1)

    @pl.when((pl.program_id(0) == 0) & (qb == 0))
    def _():
        qc = lax.broadcasted_iota(jnp.int32, (QB, WIN), 0) // CHUNK
        kc = lax.broadcasted_iota(jnp.int32, (QB, WIN), 1) // CHUNK
        band = (kc >= qc) & (kc <= qc + N_LEFT)
        for h in range(HEADS):
            row = jnp.broadcast_to(brow_ref[h:h + 1, :], (QB, ROLL_W))
            toeplitz = pltpu.roll(row, 0, 1, stride=1, stride_axis=0)
            bias_ref[h] = jnp.where(band, toeplitz[:, :WIN], NEG)

    first_valid = jnp.maximum(N_SEG - 1 - qb, 0) * QB
    col = lax.broadcasted_iota(jnp.int32, (QB, QB), 1)
    k_refs = (k0_ref, k1_ref, k2_ref)
    v_refs = (v0_ref, v1_ref, v2_ref)
    scale = HEAD_DIM ** -0.5
    ss = jnp.zeros((QB, 1), F32)
    for h in range(HEADS):
        sl = slice(h * HEAD_DIM, (h + 1) * HEAD_DIM)
        qh = q_ref[:, sl]
        s = []
        for j in range(N_SEG):
            sj = lax.dot_general(qh, k_refs[j][:, sl], (((1,), (1,)), ((), ())),
                                 preferred_element_type=F32)
            sj = sj * scale + bias_ref[h, :, j * QB:(j + 1) * QB]
            s.append(jnp.where(col + j * QB >= first_valid, sj, NEG))
        m = s[0].max(axis=-1, keepdims=True)
        for j in range(1, N_SEG):
            m = jnp.maximum(m, s[j].max(axis=-1, keepdims=True))
        l = jnp.zeros((QB, 1), F32)
        o = jnp.zeros((QB, HEAD_DIM), F32)
        for j in range(N_SEG):
            p = jnp.exp(s[j] - m)
            l = l + p.sum(axis=-1, keepdims=True)
            o = o + jnp.dot(p.astype(BF16), v_refs[j][:, sl], preferred_element_type=F32)
        o = o / l
        acc_ref[:, sl] = o
        ss = ss + (o * o).sum(axis=-1, keepdims=True)
    inv = lax.rsqrt(ss / GROUP + EPS)
    o_ref[...] = (acc_ref[...] * inv * gain_ref[...]).astype(o_ref.dtype)


def _attention(proj, brow, gain, batch, seq):
    t = proj.shape[0]
    nqb = seq // QB

    def kv_spec(col_block, seg):
        def index(b, i):
            return (b * nqb + jnp.maximum(i - (N_SEG - 1 - seg), 0), col_block)
        return pl.BlockSpec((QB, GROUP), index)

    return pl.pallas_call(
        _attn_kernel,
        grid=(batch, nqb),
        in_specs=[pl.BlockSpec((QB, GROUP), lambda b, i: (b * nqb + i, 0))]
        + [kv_spec(1, s) for s in range(N_SEG)]
        + [kv_spec(2, s) for s in range(N_SEG)]
        + [pl.BlockSpec((HEADS, ROLL_W), lambda b, i: (0, 0)),
           pl.BlockSpec((1, GROUP), lambda b, i: (0, 0))],
        out_specs=pl.BlockSpec((QB, GROUP), lambda b, i: (b * nqb + i, 0)),
        out_shape=jax.ShapeDtypeStruct((t, GROUP), BF16),
        scratch_shapes=[pltpu.VMEM((HEADS, QB, WIN), F32), pltpu.VMEM((QB, GROUP), F32)],
        compiler_params=_params("arbitrary", "arbitrary"),
        name="attention",
    )(proj, proj, proj, proj, proj, proj, proj, brow, gain)


TB = 512


def _split3(x):
    hi = x.astype(BF16)
    r1 = x - hi.astype(F32)
    mid = r1.astype(BF16)
    lo = (r1 - mid.astype(F32)).astype(BF16)
    return hi, mid, lo


def _hgrn_kernel(q_ref, f_ref, i_ref, g_ref, lb_ref, gn_ref, o_ref, st_ref):
    @pl.when(pl.program_id(1) == 0)
    def _():
        st_ref[...] = jnp.zeros_like(st_ref)

    lb = lb_ref[...]
    gn = gn_ref[...]
    r_i = lax.broadcasted_iota(jnp.int32, (CHUNK, CHUNK), 0)
    c_i = lax.broadcasted_iota(jnp.int32, (CHUNK, CHUNK), 1)
    causal = r_i >= c_i
    tril = causal.astype(BF16)

    def chunk(c, carry):
        rows = pl.ds(pl.multiple_of(c * CHUNK, CHUNK), CHUNK)
        qr = q_ref[rows, :].astype(F32)
        fr = f_ref[rows, :].astype(F32)
        q = qr * jax.nn.sigmoid(qr)
        f = lb + (1.0 - lb) * jax.nn.sigmoid(fr)
        k = 1.0 - f
        logf = jnp.log(f)
        b = sum(jnp.dot(tril, part, preferred_element_type=F32) for part in _split3(logf))
        b_last = b[CHUNK - 1:CHUNK, :]
        q_dec = (q * jnp.exp(b)).astype(BF16)
        k_intra = (k * jnp.exp(-b)).astype(BF16)
        k_state = (k * jnp.exp(b_last - b)).astype(BF16)
        decay = jnp.exp(b_last)
        for h in range(HEADS):
            sl = slice(h * HEAD_DIM, (h + 1) * HEAD_DIM)
            v = i_ref[rows, sl]
            qd = q_dec[:, sl]
            a = lax.dot_general(qd, k_intra[:, sl], (((1,), (1,)), ((), ())),
                                preferred_element_type=F32)
            a = jnp.where(causal, a, 0.0).astype(BF16)
            st = st_ref[h]
            o = jnp.dot(a, v, preferred_element_type=F32)
            o = o + lax.dot_general(qd, st.astype(BF16), (((1,), (1,)), ((), ())),
                                    preferred_element_type=F32)
            u_t = lax.dot_general(v, k_state[:, sl], (((0,), (0,)), ((), ())),
                                  preferred_element_type=F32)
            st_ref[h] = st * decay[:, sl] + u_t
            o = _rms_rows(o, gn)
            gr = g_ref[rows, sl].astype(F32)
            o_ref[rows, sl] = (o * (gr * jax.nn.sigmoid(gr))).astype(o_ref.dtype)
        return carry

    lax.fori_loop(0, TB // CHUNK, chunk, 0)


def _hgrn(proj, lb, gn, batch, seq):
    t = proj.shape[0]
    nb = seq // TB

    def col(cb):
        return pl.BlockSpec((TB, GROUP), lambda b, i: (b * nb + i, cb))

    return pl.pallas_call(
        _hgrn_kernel,
        grid=(batch, nb),
        in_specs=[col(3), col(4), col(5), col(6),
                  pl.BlockSpec((1, GROUP), lambda b, i: (0, 0)),
                  pl.BlockSpec((1, HEAD_DIM), lambda b, i: (0, 0))],
        out_specs=pl.BlockSpec((TB, GROUP), lambda b, i: (b * nb + i, 0)),
        out_shape=jax.ShapeDtypeStruct((t, GROUP), BF16),
        scratch_shapes=[pltpu.VMEM((HEADS, HEAD_DIM, HEAD_DIM), F32)],
        compiler_params=_params("arbitrary", "arbitrary"),
        name="hgrn2",
    )(proj, proj, proj, proj, lb, gn)


def _outproj_kernel(x_ref, a_ref, r_ref, wa_ref, wr_ref, o_ref):
    y = jnp.dot(a_ref[...], wa_ref[...], preferred_element_type=F32)
    y = y + jnp.dot(r_ref[...], wr_ref[...], preferred_element_type=F32)
    o_ref[...] = x_ref[...] + y


def _outproj(x2d, attn, rec, w_bf16, tm, tn):
    t, d = x2d.shape
    g = attn.shape[1]
    return pl.pallas_call(
        _outproj_kernel,
        grid=(t // tm, d // tn),
        in_specs=[
            pl.BlockSpec((tm, tn), lambda i, j: (i, j)),
            pl.BlockSpec((tm, g), lambda i, j: (i, 0)),
            pl.BlockSpec((tm, g), lambda i, j: (i, 0)),
            pl.BlockSpec((g, tn), lambda i, j: (0, j)),
            pl.BlockSpec((g, tn), lambda i, j: (1, j)),
        ],
        out_specs=pl.BlockSpec((tm, tn), lambda i, j: (i, j)),
        out_shape=jax.ShapeDtypeStruct((t, d), F32),
        compiler_params=_params("arbitrary", "arbitrary"),
        name="outproj",
    )(x2d, attn, rec, w_bf16, w_bf16)


FFN_DOWN_COLS = 512


def _ffn_kernel(x_ref, g2_ref, wg_ref, wu_ref, wd_ref, gf_ref, o_ref, h_ref):
    j = pl.program_id(1)
    n_rows, d = x_ref.shape

    @pl.when(j == 0)
    def _():
        g2 = g2_ref[...]

        def body(i, carry):
            rows = pl.ds(pl.multiple_of(i * ROW_STEP, ROW_STEP), ROW_STEP)
            x = x_ref[rows, :]
            h_ref[rows, :] = _rms_rows(x, g2).astype(h_ref.dtype)
            o_ref[rows, :] = x
            return carry

        lax.fori_loop(0, n_rows // ROW_STEP, body, 0)

    h = h_ref[...]
    gate = jnp.dot(h, wg_ref[...], preferred_element_type=F32)
    up = jnp.dot(h, wu_ref[...], preferred_element_type=F32)
    act = (gate * jax.nn.sigmoid(gate) * up).astype(BF16)
    for n in range(0, d, FFN_DOWN_COLS):
        cols = slice(n, n + FFN_DOWN_COLS)
        o_ref[:, cols] += jnp.dot(act, wd_ref[:, cols], preferred_element_type=F32)

    @pl.when(j == pl.num_programs(1) - 1)
    def _():
        gf = gf_ref[...]

        def body(i, carry):
            rows = pl.ds(pl.multiple_of(i * ROW_STEP, ROW_STEP), ROW_STEP)
            o_ref[rows, :] = _rms_rows(o_ref[rows, :], gf)
            return carry

        lax.fori_loop(0, n_rows // ROW_STEP, body, 0)


def _ffn(x1, g2, wg, wu, wd, gf, tm, tf):
    t, d = x1.shape
    ff = wg.shape[1]
    return pl.pallas_call(
        _ffn_kernel,
        grid=(t // tm, ff // tf),
        in_specs=[
            pl.BlockSpec((tm, d), lambda i, j: (i, 0)),
            pl.BlockSpec((1, d), lambda i, j: (0, 0)),
            pl.BlockSpec((d, tf), lambda i, j: (0, j)),
            pl.BlockSpec((d, tf), lambda i, j: (0, j)),
            pl.BlockSpec((tf, d), lambda i, j: (j, 0)),
            pl.BlockSpec((1, d), lambda i, j: (0, 0)),
        ],
        out_specs=pl.BlockSpec((tm, d), lambda i, j: (i, 0)),
        out_shape=jax.ShapeDtypeStruct((t, d), F32),
        scratch_shapes=[pltpu.VMEM((tm, d), BF16)],
        compiler_params=_params("arbitrary", "arbitrary"),
        name="ffn",
    )(x1, g2, wg, wu, wd, gf)


def kernel(x, norm1_gain, w_in, rel_bias, lower_bounds, grn_norm_gain, attn_out_gain, w_out,
           norm2_gain, w_gate, w_up, w_down, final_gain):
    batch, seq, d = x.shape
    depth = w_in.shape[0]
    assert seq % TB == 0 and seq % QB == 0 and w_in.shape[2] == 7 * GROUP and d == 2 * GROUP
    lb_all = jnp.cumsum(jax.nn.softmax(lower_bounds.astype(F32), axis=0), axis=0)
    xf = x.reshape(batch * seq, d)
    for l in range(depth):
        proj = _inproj(xf, norm1_gain[l][None, :], w_in[l].astype(BF16), tm=1024, tn=1024)
        attn = _attention(proj, _bias_rows(rel_bias[l]), attn_out_gain[l][None, :], batch, seq)
        rec = _hgrn(proj, lb_all[l][None, :], grn_norm_gain[l][None, :], batch, seq)
        x1 = _outproj(xf, attn, rec, w_out[l].astype(BF16), tm=1024, tn=1024)
        last = l == depth - 1
        assert last, "the final rmsnorm is fused into the last layer's FFN"
        xf = _ffn(x1, norm2_gain[l][None, :], w_gate[l].astype(BF16), w_up[l].astype(BF16),
                  w_down[l].astype(BF16), final_gain[None, :], tm=1024, tf=512)
    return xf.reshape(batch, seq, d)
```

```python
import functools

import jax
import jax.numpy as jnp
from jax import lax
from jax.experimental import pallas as pl
from jax.experimental.pallas import tpu as pltpu

F32 = jnp.float32
BF16 = jnp.bfloat16

EPS = 1e-6
CHUNK = 64
N_LEFT = 8
MAX_REL = 128
HEADS = 8
HEAD_DIM = 128
GROUP = HEADS * HEAD_DIM

VMEM_LIMIT_BYTES = 56 * 1024 * 1024
NEG = -1e30
LOG2E = 1.4426950408889634


def _params(*sem):
    return pltpu.CompilerParams(dimension_semantics=sem, vmem_limit_bytes=VMEM_LIMIT_BYTES)


def _rms_rows(x, gain):
    ms = jnp.mean(x * x, axis=-1, keepdims=True)
    return x * lax.rsqrt(ms + EPS) * gain


ROW_STEP = 128


def _norm_rows_to(h_ref, x_ref, gain_ref):
    gain = gain_ref[...]

    def body(i, carry):
        rows = pl.ds(pl.multiple_of(i * ROW_STEP, ROW_STEP), ROW_STEP)
        h_ref[rows, :] = _rms_rows(x_ref[rows, :], gain).astype(h_ref.dtype)
        return carry

    lax.fori_loop(0, x_ref.shape[0] // ROW_STEP, body, 0)


def _inproj_kernel(x_ref, gain_ref, w_ref, o_ref, h_ref):
    @pl.when(pl.program_id(1) == 0)
    def _():
        _norm_rows_to(h_ref, x_ref, gain_ref)

    o_ref[...] = jnp.dot(h_ref[...], w_ref[...], preferred_element_type=F32).astype(o_ref.dtype)


def _inproj(x2d, gain, w_bf16, tm, tn):
    t, d = x2d.shape
    n = w_bf16.shape[1]
    return pl.pallas_call(
        _inproj_kernel,
        grid=(t // tm, n // tn),
        in_specs=[
            pl.BlockSpec((tm, d), lambda i, j: (i, 0)),
            pl.BlockSpec((1, d), lambda i, j: (0, 0)),
            pl.BlockSpec((d, tn), lambda i, j: (0, j)),
        ],
        out_specs=pl.BlockSpec((tm, tn), lambda i, j: (i, j)),
        out_shape=jax.ShapeDtypeStruct((t, n), BF16),
        scratch_shapes=[pltpu.VMEM((tm, d), BF16)],
        compiler_params=_params("arbitrary", "arbitrary"),
        name="inproj",
    )(x2d, gain, w_bf16)


QB = 256
N_SEG = 1 + (N_LEFT * CHUNK) // QB
WIN = N_SEG * QB
ROLL_W = 1024


def _bias_rows(rel_bias):
    h = rel_bias.shape[0]
    far_past = rel_bias[:, 2 * MAX_REL:2 * MAX_REL + 1]
    far_future = rel_bias[:, 0:1]
    n_head = (WIN - QB) - MAX_REL
    n_tail = WIN - n_head - (2 * MAX_REL + 1)
    return jnp.concatenate([
        jnp.broadcast_to(far_past, (h, n_head)),
        rel_bias[:, ::-1],
        jnp.broadcast_to(far_future, (h, n_tail)),
        jnp.broadcast_to(far_past, (h, ROLL_W - WIN)),
    ], axis=1).astype(F32)


def _attn_kernel(q_ref, k0_ref, k1_ref, k2_ref, v0_ref, v1_ref, v2_ref, brow_ref, gain_ref,
                 o_ref, bias_ref, acc_ref):
    qb = pl.program_id(1)

    @pl.when((pl.program_id(0) == 0) & (qb == 0))
    def _():
        qc = lax.broadcasted_iota(jnp.int32, (QB, WIN), 0) // CHUNK
        kc = lax.broadcasted_iota(jnp.int32, (QB, WIN), 1) // CHUNK
        band = (kc >= qc) & (kc <= qc + N_LEFT)
        for h in range(HEADS):
            row = jnp.broadcast_to(brow_ref[h:h + 1, :] * LOG2E, (QB, ROLL_W))
            toeplitz = pltpu.roll(row, 0, 1, stride=1, stride_axis=0)
            bias_ref[h] = jnp.where(band, toeplitz[:, :WIN], NEG)

    k_refs = (k0_ref, k1_ref, k2_ref)
    v_refs = (v0_ref, v1_ref, v2_ref)
    scale = HEAD_DIM ** -0.5 * LOG2E

    def lane_halves(op, x):
        while x.shape[1] > HEAD_DIM:
            half = x.shape[1] // 2
            x = op(x[:, :half], x[:, half:])
        return x

    def attend(segs):
        sq = jnp.zeros((QB, HEAD_DIM), F32)
        for h in range(HEADS):
            sl = slice(h * HEAD_DIM, (h + 1) * HEAD_DIM)
            qh = q_ref[:, sl]
            s = [lax.dot_general(qh, k_refs[j][:, sl], (((1,), (1,)), ((), ())),
                                 preferred_element_type=F32) * scale
                 + bias_ref[h, :, j * QB:(j + 1) * QB] for j in segs]
            m = lane_halves(jnp.maximum, functools.reduce(jnp.maximum, s)).max(axis=-1, keepdims=True)
            p = [jnp.exp2(sj - m) for sj in s]
            l = lane_halves(jnp.add, functools.reduce(jnp.add, p)).sum(axis=-1, keepdims=True)
            o = functools.reduce(jnp.add, [
                jnp.dot(pj.astype(BF16), v_refs[j][:, sl], preferred_element_type=F32)
                for pj, j in zip(p, segs)])
            o = o * (1.0 / l)
            acc_ref[:, sl] = o
            sq = sq + o * o
        inv = lax.rsqrt(sq.sum(axis=-1, keepdims=True) / GROUP + EPS)
        o_ref[...] = (acc_ref[...] * inv * gain_ref[...]).astype(o_ref.dtype)

    for n_valid in range(1, N_SEG + 1):
        @pl.when(jnp.minimum(qb, N_SEG - 1) == n_valid - 1)
        def _():
            attend(tuple(range(N_SEG - n_valid, N_SEG)))


def _attention(proj, brow, gain, batch, seq):
    t = proj.shape[0]
    nqb = seq // QB

    def kv_spec(col_block, seg):
        def index(b, i):
            return (b * nqb + jnp.maximum(i - (N_SEG - 1 - seg), 0), col_block)
        return pl.BlockSpec((QB, GROUP), index)

    return pl.pallas_call(
        _attn_kernel,
        grid=(batch, nqb),
        in_specs=[pl.BlockSpec((QB, GROUP), lambda b, i: (b * nqb + i, 0))]
        + [kv_spec(1, s) for s in range(N_SEG)]
        + [kv_spec(2, s) for s in range(N_SEG)]
        + [pl.BlockSpec((HEADS, ROLL_W), lambda b, i: (0, 0)),
           pl.BlockSpec((1, GROUP), lambda b, i: (0, 0))],
        out_specs=pl.BlockSpec((QB, GROUP), lambda b, i: (b * nqb + i, 0)),
        out_shape=jax.ShapeDtypeStruct((t, GROUP), BF16),
        scratch_shapes=[pltpu.VMEM((HEADS, QB, WIN), F32), pltpu.VMEM((QB, GROUP), F32)],
        compiler_params=_params("arbitrary", "arbitrary"),
        name="attention",
    )(proj, proj, proj, proj, proj, proj, proj, brow, gain)


TB = 512
HGRN_UNROLL = 2


def _silu(x):
    hx = 0.5 * x
    return hx + hx * jnp.tanh(hx)


def _split3(x):
    hi = x.astype(BF16)
    r1 = x - hi.astype(F32)
    mid = r1.astype(BF16)
    lo = (r1 - mid.astype(F32)).astype(BF16)
    return hi, mid, lo


def _hgrn_kernel(q_ref, f_ref, i_ref, g_ref, lb_ref, gn_ref, o_ref, st_ref):
    @pl.when(pl.program_id(1) == 0)
    def _():
        st_ref[...] = jnp.zeros_like(st_ref)

    lb = lb_ref[...]
    f_mid = 0.5 * (1.0 + lb)
    f_amp = 0.5 * (1.0 - lb)
    gn = gn_ref[...]
    r_i = lax.broadcasted_iota(jnp.int32, (CHUNK, CHUNK), 0)
    c_i = lax.broadcasted_iota(jnp.int32, (CHUNK, CHUNK), 1)
    causal = r_i >= c_i
    tril = causal.astype(BF16)
    tril3 = jnp.concatenate([tril, tril, tril], axis=1)

    def chunk(c):
        rows = pl.ds(pl.multiple_of(c * CHUNK, CHUNK), CHUNK)
        q = _silu(q_ref[rows, :].astype(F32))
        f = f_mid + f_amp * jnp.tanh(0.5 * f_ref[rows, :].astype(F32))
        k = 1.0 - f
        b = jnp.dot(tril3, jnp.concatenate(_split3(jnp.log2(f)), axis=0),
                    preferred_element_type=F32)
        b_last = b[CHUNK - 1:CHUNK, :]
        q_dec = (q * jnp.exp2(b)).astype(BF16)
        k_intra = (k * jnp.exp2(-b)).astype(BF16)
        k_state = (k * jnp.exp2(b_last - b)).astype(BF16)
        decay = jnp.exp2(b_last)
        for h in range(HEADS):
            sl = slice(h * HEAD_DIM, (h + 1) * HEAD_DIM)
            v = i_ref[rows, sl]
            qd = q_dec[:, sl]
            a = lax.dot_general(qd, k_intra[:, sl], (((1,), (1,)), ((), ())),
                                preferred_element_type=F32)
            a = jnp.where(causal, a, 0.0).astype(BF16)
            st = st_ref[h]
            o = jnp.dot(a, v, preferred_element_type=F32)
            o = o + lax.dot_general(qd, st.astype(BF16), (((1,), (1,)), ((), ())),
                                    preferred_element_type=F32)
            u_t = lax.dot_general(v, k_state[:, sl], (((0,), (0,)), ((), ())),
                                  preferred_element_type=F32)
            st_ref[h] = st * decay[:, sl] + u_t
            o = _rms_rows(o, gn)
            o_ref[rows, sl] = (o * _silu(g_ref[rows, sl].astype(F32))).astype(o_ref.dtype)

    def chunks(i, carry):
        for u in range(HGRN_UNROLL):
            chunk(i * HGRN_UNROLL + u)
        return carry

    lax.fori_loop(0, TB // (CHUNK * HGRN_UNROLL), chunks, 0)


def _hgrn(proj, lb, gn, batch, seq):
    t = proj.shape[0]
    nb = seq // TB

    def col(cb):
        return pl.BlockSpec((TB, GROUP), lambda b, i: (b * nb + i, cb))

    return pl.pallas_call(
        _hgrn_kernel,
        grid=(batch, nb),
        in_specs=[col(3), col(4), col(5), col(6),
                  pl.BlockSpec((1, GROUP), lambda b, i: (0, 0)),
                  pl.BlockSpec((1, HEAD_DIM), lambda b, i: (0, 0))],
        out_specs=pl.BlockSpec((TB, GROUP), lambda b, i: (b * nb + i, 0)),
        out_shape=jax.ShapeDtypeStruct((t, GROUP), BF16),
        scratch_shapes=[pltpu.VMEM((HEADS, HEAD_DIM, HEAD_DIM), F32)],
        compiler_params=_params("arbitrary", "arbitrary"),
        name="hgrn2",
    )(proj, proj, proj, proj, lb, gn)


OUT_COLS = 512


def _outproj_kernel(x_ref, a_ref, r_ref, w_ref, o_ref):
    mixed = jnp.concatenate([a_ref[...], r_ref[...]], axis=1)
    for n in range(0, o_ref.shape[1], OUT_COLS):
        cols = slice(n, n + OUT_COLS)
        o_ref[:, cols] = x_ref[:, cols] + jnp.dot(mixed, w_ref[:, cols],
                                                  preferred_element_type=F32)


def _outproj(x2d, attn, rec, w_bf16, tm):
    t, d = x2d.shape
    g = attn.shape[1]
    return pl.pallas_call(
        _outproj_kernel,
        grid=(t // tm,),
        in_specs=[
            pl.BlockSpec((tm, d), lambda i: (i, 0)),
            pl.BlockSpec((tm, g), lambda i: (i, 0)),
            pl.BlockSpec((tm, g), lambda i: (i, 0)),
            pl.BlockSpec((2 * g, d), lambda i: (0, 0)),
        ],
        out_specs=pl.BlockSpec((tm, d), lambda i: (i, 0)),
        out_shape=jax.ShapeDtypeStruct((t, d), F32),
        compiler_params=_params("arbitrary"),
        name="outproj",
    )(x2d, attn, rec, w_bf16)


FFN_DOWN_COLS = 512


def _ffn_kernel(x_ref, g2_ref, wg_ref, wu_ref, wd_ref, gf_ref, o_ref, h_ref):
    j = pl.program_id(1)
    n_rows, d = x_ref.shape

    @pl.when(j == 0)
    def _():
        g2 = g2_ref[...]

        def body(i, carry):
            rows = pl.ds(pl.multiple_of(i * ROW_STEP, ROW_STEP), ROW_STEP)
            x = x_ref[rows, :]
            h_ref[rows, :] = _rms_rows(x, g2).astype(h_ref.dtype)
            o_ref[rows, :] = x
            return carry

        lax.fori_loop(0, n_rows // ROW_STEP, body, 0)

    h = h_ref[...]
    gate = jnp.dot(h, wg_ref[...], preferred_element_type=F32)
    up = jnp.dot(h, wu_ref[...], preferred_element_type=F32)
    act = (gate * jax.nn.sigmoid(gate) * up).astype(BF16)
    for n in range(0, d, FFN_DOWN_COLS):
        cols = slice(n, n + FFN_DOWN_COLS)
        o_ref[:, cols] += jnp.dot(act, wd_ref[:, cols], preferred_element_type=F32)

    @pl.when(j == pl.num_programs(1) - 1)
    def _():
        gf = gf_ref[...]

        def body(i, carry):
            rows = pl.ds(pl.multiple_of(i * ROW_STEP, ROW_STEP), ROW_STEP)
            o_ref[rows, :] = _rms_rows(o_ref[rows, :], gf)
            return carry

        lax.fori_loop(0, n_rows // ROW_STEP, body, 0)


def _ffn(x1, g2, wg, wu, wd, gf, tm, tf):
    t, d = x1.shape
    ff = wg.shape[1]
    return pl.pallas_call(
        _ffn_kernel,
        grid=(t // tm, ff // tf),
        in_specs=[
            pl.BlockSpec((tm, d), lambda i, j: (i, 0)),
            pl.BlockSpec((1, d), lambda i, j: (0, 0)),
            pl.BlockSpec((d, tf), lambda i, j: (0, j)),
            pl.BlockSpec((d, tf), lambda i, j: (0, j)),
            pl.BlockSpec((tf, d), lambda i, j: (j, 0)),
            pl.BlockSpec((1, d), lambda i, j: (0, 0)),
        ],
        out_specs=pl.BlockSpec((tm, d), lambda i, j: (i, 0)),
        out_shape=jax.ShapeDtypeStruct((t, d), F32),
        scratch_shapes=[pltpu.VMEM((tm, d), BF16)],
        compiler_params=_params("arbitrary", "arbitrary"),
        name="ffn",
    )(x1, g2, wg, wu, wd, gf)


def kernel(x, norm1_gain, w_in, rel_bias, lower_bounds, grn_norm_gain, attn_out_gain, w_out,
           norm2_gain, w_gate, w_up, w_down, final_gain):
    batch, seq, d = x.shape
    depth = w_in.shape[0]
    assert seq % TB == 0 and seq % QB == 0 and w_in.shape[2] == 7 * GROUP and d == 2 * GROUP
    lb_all = jnp.cumsum(jax.nn.softmax(lower_bounds.astype(F32), axis=0), axis=0)
    xf = x.reshape(batch * seq, d)
    for l in range(depth):
        proj = _inproj(xf, norm1_gain[l][None, :], w_in[l].astype(BF16), tm=1024, tn=1024)
        attn = _attention(proj, _bias_rows(rel_bias[l]), attn_out_gain[l][None, :], batch, seq)
        rec = _hgrn(proj, lb_all[l][None, :], grn_norm_gain[l][None, :], batch, seq)
        x1 = _outproj(xf, attn, rec, w_out[l].astype(BF16), tm=512)
        last = l == depth - 1
        assert last, "the final rmsnorm is fused into the last layer's FFN"
        xf = _ffn(x1, norm2_gain[l][None, :], w_gate[l].astype(BF16), w_up[l].astype(BF16),
                  w_down[l].astype(BF16), final_gain[None, :], tm=1024, tf=512)
    return xf.reshape(batch, seq, d)
```

```python
import functools

import jax
import jax.numpy as jnp
from jax import lax
from jax.experimental import pallas as pl
from jax.experimental.pallas import tpu as pltpu

F32 = jnp.float32
BF16 = jnp.bfloat16

EPS = 1e-6
CHUNK = 64
N_LEFT = 8
MAX_REL = 128
HEADS = 8
HEAD_DIM = 128
GROUP = HEADS * HEAD_DIM

VMEM_LIMIT_BYTES = 56 * 1024 * 1024
NEG = -1e30
LOG2E = 1.4426950408889634


def _params(*sem):
    return pltpu.CompilerParams(dimension_semantics=sem, vmem_limit_bytes=VMEM_LIMIT_BYTES)


def _rms_rows(x, gain):
    ms = jnp.mean(x * x, axis=-1, keepdims=True)
    return x * lax.rsqrt(ms + EPS) * gain


ROW_STEP = 128


def _norm_rows_to(h_ref, x_ref, gain_ref):
    gain = gain_ref[...]

    def body(i, carry):
        rows = pl.ds(pl.multiple_of(i * ROW_STEP, ROW_STEP), ROW_STEP)
        h_ref[rows, :] = _rms_rows(x_ref[rows, :], gain).astype(h_ref.dtype)
        return carry

    lax.fori_loop(0, x_ref.shape[0] // ROW_STEP, body, 0)


def _inproj_kernel(x_ref, gain_ref, w_ref, o_ref, h_ref):
    @pl.when(pl.program_id(1) == 0)
    def _():
        _norm_rows_to(h_ref, x_ref, gain_ref)

    o_ref[...] = jnp.dot(h_ref[...], w_ref[...], preferred_element_type=F32).astype(o_ref.dtype)


def _inproj(x2d, gain, w_bf16, tm, tn):
    t, d = x2d.shape
    n = w_bf16.shape[1]
    return pl.pallas_call(
        _inproj_kernel,
        grid=(t // tm, n // tn),
        in_specs=[
            pl.BlockSpec((tm, d), lambda i, j: (i, 0)),
            pl.BlockSpec((1, d), lambda i, j: (0, 0)),
            pl.BlockSpec((d, tn), lambda i, j: (0, j)),
        ],
        out_specs=pl.BlockSpec((tm, tn), lambda i, j: (i, j)),
        out_shape=jax.ShapeDtypeStruct((t, n), BF16),
        scratch_shapes=[pltpu.VMEM((tm, d), BF16)],
        compiler_params=_params("arbitrary", "arbitrary"),
        name="inproj",
    )(x2d, gain, w_bf16)


QB = 256
N_SEG = 1 + (N_LEFT * CHUNK) // QB
WIN = N_SEG * QB
ROLL_W = 1024


def _bias_rows(rel_bias):
    h = rel_bias.shape[0]
    far_past = rel_bias[:, 2 * MAX_REL:2 * MAX_REL + 1]
    far_future = rel_bias[:, 0:1]
    n_head = (WIN - QB) - MAX_REL
    n_tail = WIN - n_head - (2 * MAX_REL + 1)
    return jnp.concatenate([
        jnp.broadcast_to(far_past, (h, n_head)),
        rel_bias[:, ::-1],
        jnp.broadcast_to(far_future, (h, n_tail)),
        jnp.broadcast_to(far_past, (h, ROLL_W - WIN)),
    ], axis=1).astype(F32)


def _attn_kernel(q_ref, k0_ref, k1_ref, k2_ref, v0_ref, v1_ref, v2_ref, brow_ref, gain_ref,
                 o_ref, bias_ref, acc_ref, s_ref):
    qb = pl.program_id(1)

    @pl.when((pl.program_id(0) == 0) & (qb == 0))
    def _():
        qc = lax.broadcasted_iota(jnp.int32, (QB, WIN), 0) // CHUNK
        kc = lax.broadcasted_iota(jnp.int32, (QB, WIN), 1) // CHUNK
        band = (kc >= qc) & (kc <= qc + N_LEFT)
        for h in range(HEADS):
            row = jnp.broadcast_to(brow_ref[h:h + 1, :] * LOG2E, (QB, ROLL_W))
            toeplitz = pltpu.roll(row, 0, 1, stride=1, stride_axis=0)
            bias_ref[h] = jnp.where(band, toeplitz[:, :WIN], NEG)

    k_refs = (k0_ref, k1_ref, k2_ref)
    v_refs = (v0_ref, v1_ref, v2_ref)
    scale = HEAD_DIM ** -0.5 * LOG2E

    def lane_halves(op, x):
        while x.shape[1] > HEAD_DIM:
            half = x.shape[1] // 2
            x = op(x[:, :half], x[:, half:])
        return x

    def attend(segs):
        m = []
        for h in range(HEADS):
            sl = slice(h * HEAD_DIM, (h + 1) * HEAD_DIM)
            qh = q_ref[:, sl]
            mx = None
            for j in segs:
                cols = slice(j * QB, (j + 1) * QB)
                sj = lax.dot_general(qh, k_refs[j][:, sl], (((1,), (1,)), ((), ())),
                                     preferred_element_type=F32) * scale + bias_ref[h, :, cols]
                s_ref[h, :, cols] = sj
                mx = sj if mx is None else jnp.maximum(mx, sj)
            m.append(lane_halves(jnp.maximum, mx).max(axis=-1, keepdims=True))
        sq = jnp.zeros((QB, HEAD_DIM), F32)
        for h in range(HEADS):
            sl = slice(h * HEAD_DIM, (h + 1) * HEAD_DIM)
            psum = None
            o = None
            for j in segs:
                p = jnp.exp2(s_ref[h, :, j * QB:(j + 1) * QB] - m[h])
                pv = jnp.dot(p.astype(BF16), v_refs[j][:, sl], preferred_element_type=F32)
                psum = p if psum is None else psum + p
                o = pv if o is None else o + pv
            l = lane_halves(jnp.add, psum).sum(axis=-1, keepdims=True)
            o = o * (1.0 / l)
            acc_ref[:, sl] = o
            sq = sq + o * o
        inv = lax.rsqrt(sq.sum(axis=-1, keepdims=True) / GROUP + EPS)
        o_ref[...] = (acc_ref[...] * inv * gain_ref[...]).astype(o_ref.dtype)

    for n_valid in range(1, N_SEG + 1):
        @pl.when(jnp.minimum(qb, N_SEG - 1) == n_valid - 1)
        def _():
            attend(tuple(range(N_SEG - n_valid, N_SEG)))


def _attention(proj, brow, gain, batch, seq):
    t = proj.shape[0]
    nqb = seq // QB

    def kv_spec(col_block, seg):
        def index(b, i):
            return (b * nqb + jnp.maximum(i - (N_SEG - 1 - seg), 0), col_block)
        return pl.BlockSpec((QB, GROUP), index)

    return pl.pallas_call(
        _attn_kernel,
        grid=(batch, nqb),
        in_specs=[pl.BlockSpec((QB, GROUP), lambda b, i: (b * nqb + i, 0))]
        + [kv_spec(1, s) for s in range(N_SEG)]
        + [kv_spec(2, s) for s in range(N_SEG)]
        + [pl.BlockSpec((HEADS, ROLL_W), lambda b, i: (0, 0)),
           pl.BlockSpec((1, GROUP), lambda b, i: (0, 0))],
        out_specs=pl.BlockSpec((QB, GROUP), lambda b, i: (b * nqb + i, 0)),
        out_shape=jax.ShapeDtypeStruct((t, GROUP), BF16),
        scratch_shapes=[pltpu.VMEM((HEADS, QB, WIN), F32),
                        pltpu.VMEM((QB, GROUP), F32),
                        pltpu.VMEM((HEADS, QB, WIN), F32)],
        compiler_params=_params("arbitrary", "arbitrary"),
        name="attention",
    )(proj, proj, proj, proj, proj, proj, proj, brow, gain)


TB = 512
HGRN_UNROLL = 4


def _silu(x):
    hx = 0.5 * x
    return hx + hx * jnp.tanh(hx)


def _split3(x):
    hi = x.astype(BF16)
    r1 = x - hi.astype(F32)
    mid = r1.astype(BF16)
    lo = (r1 - mid.astype(F32)).astype(BF16)
    return hi, mid, lo


def _hgrn_kernel(q_ref, f_ref, i_ref, g_ref, lb_ref, gn_ref, o_ref,
                 st_ref, qd_ref, a_ref, u_ref, dec_ref):
    @pl.when(pl.program_id(1) == 0)
    def _():
        st_ref[...] = jnp.zeros_like(st_ref)

    lb = lb_ref[...]
    f_mid = 0.5 * (1.0 + lb)
    f_amp = 0.5 * (1.0 - lb)
    gn = gn_ref[...]
    r_i = lax.broadcasted_iota(jnp.int32, (CHUNK, CHUNK), 0)
    c_i = lax.broadcasted_iota(jnp.int32, (CHUNK, CHUNK), 1)
    causal = r_i >= c_i
    tril = causal.astype(BF16)
    tril3 = jnp.concatenate([tril, tril, tril], axis=1)

    def chunk_rows(c):
        return pl.ds(pl.multiple_of(c * CHUNK, CHUNK), CHUNK)

    def local_terms(c):
        rows = chunk_rows(c)
        q = _silu(q_ref[rows, :].astype(F32))
        f = f_mid + f_amp * jnp.tanh(0.5 * f_ref[rows, :].astype(F32))
        k = 1.0 - f
        b = jnp.dot(tril3, jnp.concatenate(_split3(jnp.log2(f)), axis=0),
                    preferred_element_type=F32)
        decay = jnp.exp2(b[CHUNK - 1:CHUNK, :])
        q_dec = (q * jnp.exp2(b)).astype(BF16)
        k_undecayed = k * jnp.exp2(-b)
        k_intra = k_undecayed.astype(BF16)
        k_state = (k_undecayed * decay).astype(BF16)
        qd_ref[rows, :] = q_dec
        dec_ref[c] = decay
        for h in range(HEADS):
            sl = slice(h * HEAD_DIM, (h + 1) * HEAD_DIM)
            a = lax.dot_general(q_dec[:, sl], k_intra[:, sl], (((1,), (1,)), ((), ())),
                                preferred_element_type=F32)
            a_ref[h, rows, :] = jnp.where(causal, a, 0.0).astype(BF16)
            u_ref[c, h] = lax.dot_general(i_ref[rows, sl], k_state[:, sl], (((0,), (0,)), ((), ())),
                                          preferred_element_type=F32)

    def outputs(c):
        rows = chunk_rows(c)
        decay = dec_ref[c]
        for h in range(HEADS):
            sl = slice(h * HEAD_DIM, (h + 1) * HEAD_DIM)
            st = st_ref[h]
            o = jnp.dot(a_ref[h, rows, :], i_ref[rows, sl], preferred_element_type=F32)
            o = o + lax.dot_general(qd_ref[rows, sl], st.astype(BF16), (((1,), (1,)), ((), ())),
                                    preferred_element_type=F32)
            st_ref[h] = st * decay[:, sl] + u_ref[c, h]
            o = _rms_rows(o, gn)
            o_ref[rows, sl] = (o * _silu(g_ref[rows, sl].astype(F32))).astype(o_ref.dtype)

    def unrolled(fn):
        def body(i, carry):
            for u in range(HGRN_UNROLL):
                fn(i * HGRN_UNROLL + u)
            return carry
        return body

    n_iter = TB // (CHUNK * HGRN_UNROLL)
    lax.fori_loop(0, n_iter, unrolled(local_terms), 0)
    lax.fori_loop(0, n_iter, unrolled(outputs), 0)


def _hgrn(proj, lb, gn, batch, seq):
    t = proj.shape[0]
    nb = seq // TB

    def col(cb):
        return pl.BlockSpec((TB, GROUP), lambda b, i: (b * nb + i, cb))

    return pl.pallas_call(
        _hgrn_kernel,
        grid=(batch, nb),
        in_specs=[col(3), col(4), col(5), col(6),
                  pl.BlockSpec((1, GROUP), lambda b, i: (0, 0)),
                  pl.BlockSpec((1, HEAD_DIM), lambda b, i: (0, 0))],
        out_specs=pl.BlockSpec((TB, GROUP), lambda b, i: (b * nb + i, 0)),
        out_shape=jax.ShapeDtypeStruct((t, GROUP), BF16),
        scratch_shapes=[
            pltpu.VMEM((HEADS, HEAD_DIM, HEAD_DIM), F32),
            pltpu.VMEM((TB, GROUP), BF16),
            pltpu.VMEM((HEADS, TB, CHUNK), BF16),
            pltpu.VMEM((TB // CHUNK, HEADS, HEAD_DIM, HEAD_DIM), F32),
            pltpu.VMEM((TB // CHUNK, 1, GROUP), F32),
        ],
        compiler_params=_params("arbitrary", "arbitrary"),
        name="hgrn2",
    )(proj, proj, proj, proj, lb, gn)


OUT_COLS = 512


def _outproj_kernel(x_ref, a_ref, r_ref, w_ref, o_ref):
    mixed = jnp.concatenate([a_ref[...], r_ref[...]], axis=1)
    for n in range(0, o_ref.shape[1], OUT_COLS):
        cols = slice(n, n + OUT_COLS)
        o_ref[:, cols] = x_ref[:, cols] + jnp.dot(mixed, w_ref[:, cols],
                                                  preferred_element_type=F32)


def _outproj(x2d, attn, rec, w_bf16, tm):
    t, d = x2d.shape
    g = attn.shape[1]
    return pl.pallas_call(
        _outproj_kernel,
        grid=(t // tm,),
        in_specs=[
            pl.BlockSpec((tm, d), lambda i: (i, 0)),
            pl.BlockSpec((tm, g), lambda i: (i, 0)),
            pl.BlockSpec((tm, g), lambda i: (i, 0)),
            pl.BlockSpec((2 * g, d), lambda i: (0, 0)),
        ],
        out_specs=pl.BlockSpec((tm, d), lambda i: (i, 0)),
        out_shape=jax.ShapeDtypeStruct((t, d), F32),
        compiler_params=_params("arbitrary"),
        name="outproj",
    )(x2d, attn, rec, w_bf16)


FFN_DOWN_COLS = 512


def _ffn_kernel(x_ref, g2_ref, wg_ref, wu_ref, wd_ref, gf_ref, o_ref, h_ref):
    j = pl.program_id(1)
    n_rows, d = x_ref.shape

    @pl.when(j == 0)
    def _():
        g2 = g2_ref[...]

        def body(i, carry):
            rows = pl.ds(pl.multiple_of(i * ROW_STEP, ROW_STEP), ROW_STEP)
            x = x_ref[rows, :]
            h_ref[rows, :] = _rms_rows(x, g2).astype(h_ref.dtype)
            o_ref[rows, :] = x
            return carry

        lax.fori_loop(0, n_rows // ROW_STEP, body, 0)

    h = h_ref[...]
    gate = jnp.dot(h, wg_ref[...], preferred_element_type=F32)
    up = jnp.dot(h, wu_ref[...], preferred_element_type=F32)
    act = (gate * jax.nn.sigmoid(gate) * up).astype(BF16)
    for n in range(0, d, FFN_DOWN_COLS):
        cols = slice(n, n + FFN_DOWN_COLS)
        o_ref[:, cols] += jnp.dot(act, wd_ref[:, cols], preferred_element_type=F32)

    @pl.when(j == pl.num_programs(1) - 1)
    def _():
        gf = gf_ref[...]

        def body(i, carry):
            rows = pl.ds(pl.multiple_of(i * ROW_STEP, ROW_STEP), ROW_STEP)
            o_ref[rows, :] = _rms_rows(o_ref[rows, :], gf)
            return carry

        lax.fori_loop(0, n_rows // ROW_STEP, body, 0)


def _ffn(x1, g2, wg, wu, wd, gf, tm, tf):
    t, d = x1.shape
    ff = wg.shape[1]
    return pl.pallas_call(
        _ffn_kernel,
        grid=(t // tm, ff // tf),
        in_specs=[
            pl.BlockSpec((tm, d), lambda i, j: (i, 0)),
            pl.BlockSpec((1, d), lambda i, j: (0, 0)),
            pl.BlockSpec((d, tf), lambda i, j: (0, j)),
            pl.BlockSpec((d, tf), lambda i, j: (0, j)),
            pl.BlockSpec((tf, d), lambda i, j: (j, 0)),
            pl.BlockSpec((1, d), lambda i, j: (0, 0)),
        ],
        out_specs=pl.BlockSpec((tm, d), lambda i, j: (i, 0)),
        out_shape=jax.ShapeDtypeStruct((t, d), F32),
        scratch_shapes=[pltpu.VMEM((tm, d), BF16)],
        compiler_params=_params("arbitrary", "arbitrary"),
        name="ffn",
    )(x1, g2, wg, wu, wd, gf)


def kernel(x, norm1_gain, w_in, rel_bias, lower_bounds, grn_norm_gain, attn_out_gain, w_out,
           norm2_gain, w_gate, w_up, w_down, final_gain):
    batch, seq, d = x.shape
    depth = w_in.shape[0]
    assert seq % TB == 0 and seq % QB == 0 and w_in.shape[2] == 7 * GROUP and d == 2 * GROUP
    lb_all = jnp.cumsum(jax.nn.softmax(lower_bounds.astype(F32), axis=0), axis=0)
    xf = x.reshape(batch * seq, d)
    for l in range(depth):
        proj = _inproj(xf, norm1_gain[l][None, :], w_in[l].astype(BF16), tm=1024, tn=1024)
        attn = _attention(proj, _bias_rows(rel_bias[l]), attn_out_gain[l][None, :], batch, seq)
        rec = _hgrn(proj, lb_all[l][None, :], grn_norm_gain[l][None, :], batch, seq)
        x1 = _outproj(xf, attn, rec, w_out[l].astype(BF16), tm=512)
        last = l == depth - 1
        assert last, "the final rmsnorm is fused into the last layer's FFN"
        xf = _ffn(x1, norm2_gain[l][None, :], w_gate[l].astype(BF16), w_up[l].astype(BF16),
                  w_down[l].astype(BF16), final_gain[None, :], tm=1024, tf=512)
    return xf.reshape(batch, seq, d)
```

```python
import functools

import jax
import jax.numpy as jnp
from jax import lax
from jax.experimental import pallas as pl
from jax.experimental.pallas import tpu as pltpu

F32 = jnp.float32
BF16 = jnp.bfloat16

EPS = 1e-6
CHUNK = 64
N_LEFT = 8
MAX_REL = 128
HEADS = 8
HEAD_DIM = 128
GROUP = HEADS * HEAD_DIM

VMEM_LIMIT_BYTES = 56 * 1024 * 1024
BF16_SUBLANES = 16
NEG = -1e30
LOG2E = 1.4426950408889634


def _params(*sem):
    return pltpu.CompilerParams(dimension_semantics=sem, vmem_limit_bytes=VMEM_LIMIT_BYTES)


def _rms_rows(x, gain):
    ms = jnp.mean(x * x, axis=-1, keepdims=True)
    return x * lax.rsqrt(ms + EPS) * gain


EDGE_ROWS = 256


def _row_slabs(n_rows):
    return [slice(r, r + EDGE_ROWS) for r in range(0, n_rows, EDGE_ROWS)]


def _inproj_kernel(x_ref, gain_ref, w_ref, o_ref, h_ref):
    j = pl.program_id(1)

    @pl.when(j == 0)
    def _():
        gain = gain_ref[...]
        for rows in _row_slabs(x_ref.shape[0]):
            h = _rms_rows(x_ref[rows, :], gain).astype(h_ref.dtype)
            h_ref[rows, :] = h
            o_ref[rows, :] = jnp.dot(h, w_ref[...], preferred_element_type=F32).astype(o_ref.dtype)

    @pl.when(j > 0)
    def _():
        o_ref[...] = jnp.dot(h_ref[...], w_ref[...],
                             preferred_element_type=F32).astype(o_ref.dtype)


def _inproj(x2d, gain, w_bf16, tm, tn):
    t, d = x2d.shape
    n = w_bf16.shape[1]
    return pl.pallas_call(
        _inproj_kernel,
        grid=(t // tm, n // tn),
        in_specs=[
            pl.BlockSpec((tm, d), lambda i, j: (i, 0)),
            pl.BlockSpec((1, d), lambda i, j: (0, 0)),
            pl.BlockSpec((d, tn), lambda i, j: (0, j)),
        ],
        out_specs=pl.BlockSpec((tm, tn), lambda i, j: (i, j)),
        out_shape=jax.ShapeDtypeStruct((t, n), BF16),
        scratch_shapes=[pltpu.VMEM((tm, d), BF16)],
        compiler_params=_params("arbitrary", "arbitrary"),
        name="inproj",
    )(x2d, gain, w_bf16)


QB = 256
N_SEG = 1 + (N_LEFT * CHUNK) // QB
WIN = N_SEG * QB
ROLL_W = 1024


def _bias_rows(rel_bias):
    h = rel_bias.shape[0]
    far_past = rel_bias[:, 2 * MAX_REL:2 * MAX_REL + 1]
    far_future = rel_bias[:, 0:1]
    n_head = (WIN - QB) - MAX_REL
    n_tail = WIN - n_head - (2 * MAX_REL + 1)
    return jnp.concatenate([
        jnp.broadcast_to(far_past, (h, n_head)),
        rel_bias[:, ::-1],
        jnp.broadcast_to(far_future, (h, n_tail)),
        jnp.broadcast_to(far_past, (h, ROLL_W - WIN)),
    ], axis=1).astype(F32)


def _attn_kernel(q_ref, k0_ref, k1_ref, k2_ref, v0_ref, v1_ref, v2_ref, brow_ref, gain_ref,
                 o_ref, bias_ref, acc_ref, s_ref):
    qb = pl.program_id(1)

    @pl.when((pl.program_id(0) == 0) & (qb == 0))
    def _():
        qc = lax.broadcasted_iota(jnp.int32, (QB, WIN), 0) // CHUNK
        kc = lax.broadcasted_iota(jnp.int32, (QB, WIN), 1) // CHUNK
        band = (kc >= qc) & (kc <= qc + N_LEFT)
        for h in range(HEADS):
            row = jnp.broadcast_to(brow_ref[h:h + 1, :] * LOG2E, (QB, ROLL_W))
            toeplitz = pltpu.roll(row, 0, 1, stride=1, stride_axis=0)
            bias_ref[h] = jnp.where(band, toeplitz[:, :WIN], NEG)

    k_refs = (k0_ref, k1_ref, k2_ref)
    v_refs = (v0_ref, v1_ref, v2_ref)
    scale = HEAD_DIM ** -0.5 * LOG2E

    def lane_halves(op, x):
        while x.shape[1] > HEAD_DIM:
            half = x.shape[1] // 2
            x = op(x[:, :half], x[:, half:])
        return x

    def attend(segs):
        m = []
        for h in range(HEADS):
            sl = slice(h * HEAD_DIM, (h + 1) * HEAD_DIM)
            qh = q_ref[:, sl]
            mx = None
            for j in segs:
                cols = slice(j * QB, (j + 1) * QB)
                sj = lax.dot_general(qh, k_refs[j][:, sl], (((1,), (1,)), ((), ())),
                                     preferred_element_type=F32) * scale + bias_ref[h, :, cols]
                s_ref[h, :, cols] = sj
                mx = sj if mx is None else jnp.maximum(mx, sj)
            m.append(lane_halves(jnp.maximum, mx).max(axis=-1, keepdims=True))
        sq = jnp.zeros((QB, HEAD_DIM), F32)
        for h in range(HEADS):
            sl = slice(h * HEAD_DIM, (h + 1) * HEAD_DIM)
            psum = None
            o = None
            for j in segs:
                p = jnp.exp2(s_ref[h, :, j * QB:(j + 1) * QB] - m[h])
                pv = jnp.dot(p.astype(BF16), v_refs[j][:, sl], preferred_element_type=F32)
                psum = p if psum is None else psum + p
                o = pv if o is None else o + pv
            l = lane_halves(jnp.add, psum).sum(axis=-1, keepdims=True)
            o = o * (1.0 / l)
            acc_ref[:, sl] = o
            sq = sq + o * o
        inv = lax.rsqrt(sq.sum(axis=-1, keepdims=True) / GROUP + EPS)
        o_ref[...] = (acc_ref[...] * inv * gain_ref[...]).astype(o_ref.dtype)

    for n_valid in range(1, N_SEG + 1):
        @pl.when(jnp.minimum(qb, N_SEG - 1) == n_valid - 1)
        def _():
            attend(tuple(range(N_SEG - n_valid, N_SEG)))


def _attention(proj, brow, gain, batch, seq):
    t = proj.shape[0]
    nqb = seq // QB

    def kv_spec(col_block, seg):
        def index(b, i):
            return (b * nqb + jnp.maximum(i - (N_SEG - 1 - seg), 0), col_block)
        return pl.BlockSpec((QB, GROUP), index)

    return pl.pallas_call(
        _attn_kernel,
        grid=(batch, nqb),
        in_specs=[pl.BlockSpec((QB, GROUP), lambda b, i: (b * nqb + i, 0))]
        + [kv_spec(1, s) for s in range(N_SEG)]
        + [kv_spec(2, s) for s in range(N_SEG)]
        + [pl.BlockSpec((HEADS, ROLL_W), lambda b, i: (0, 0)),
           pl.BlockSpec((1, GROUP), lambda b, i: (0, 0))],
        out_specs=pl.BlockSpec((QB, GROUP), lambda b, i: (b * nqb + i, 0)),
        out_shape=jax.ShapeDtypeStruct((t, GROUP), BF16),
        scratch_shapes=[pltpu.VMEM((HEADS, QB, WIN), F32),
                        pltpu.VMEM((QB, GROUP), F32),
                        pltpu.VMEM((HEADS, QB, WIN), F32)],
        compiler_params=_params("arbitrary", "arbitrary"),
        name="attention",
    )(proj, proj, proj, proj, proj, proj, proj, brow, gain)


TB = 512
HGRN_UNROLL = 4


def _silu(x):
    hx = 0.5 * x
    return hx + hx * jnp.tanh(hx)


def _split3(x):
    hi = x.astype(BF16)
    r1 = x - hi.astype(F32)
    mid = r1.astype(BF16)
    lo = (r1 - mid.astype(F32)).astype(BF16)
    return hi, mid, lo


def _hgrn_kernel(n_cast, q_ref, f_ref, i_ref, g_ref, lb_ref, gn_ref, *refs):
    cast_in, (o_ref, *cast_out) = refs[:n_cast], refs[n_cast:2 * n_cast + 1]
    st_ref, qd_ref, a_ref, u_ref, dec_ref = refs[2 * n_cast + 1:]

    for src, dst in zip(cast_in, cast_out):
        dst[...] = src[...].astype(dst.dtype)

    @pl.when(pl.program_id(1) == 0)
    def _():
        st_ref[...] = jnp.zeros_like(st_ref)

    lb = lb_ref[...]
    f_mid = 0.5 * (1.0 + lb)
    f_amp = 0.5 * (1.0 - lb)
    gn = gn_ref[...]
    r_i = lax.broadcasted_iota(jnp.int32, (CHUNK, CHUNK), 0)
    c_i = lax.broadcasted_iota(jnp.int32, (CHUNK, CHUNK), 1)
    causal = r_i >= c_i
    tril = causal.astype(BF16)
    tril3 = jnp.concatenate([tril, tril, tril], axis=1)

    def chunk_rows(c):
        return pl.ds(pl.multiple_of(c * CHUNK, CHUNK), CHUNK)

    def local_terms(c):
        rows = chunk_rows(c)
        q = _silu(q_ref[rows, :].astype(F32))
        f = f_mid + f_amp * jnp.tanh(0.5 * f_ref[rows, :].astype(F32))
        k = 1.0 - f
        b = jnp.dot(tril3, jnp.concatenate(_split3(jnp.log2(f)), axis=0),
                    preferred_element_type=F32)
        decay = jnp.exp2(b[CHUNK - 1:CHUNK, :])
        q_dec = (q * jnp.exp2(b)).astype(BF16)
        k_undecayed = k * jnp.exp2(-b)
        k_intra = k_undecayed.astype(BF16)
        k_state = (k_undecayed * decay).astype(BF16)
        qd_ref[rows, :] = q_dec
        dec_ref[c] = decay
        for h in range(HEADS):
            sl = slice(h * HEAD_DIM, (h + 1) * HEAD_DIM)
            a = lax.dot_general(q_dec[:, sl], k_intra[:, sl], (((1,), (1,)), ((), ())),
                                preferred_element_type=F32)
            a_ref[h, rows, :] = jnp.where(causal, a, 0.0).astype(BF16)
            u_ref[c, h] = lax.dot_general(i_ref[rows, sl], k_state[:, sl], (((0,), (0,)), ((), ())),
                                          preferred_element_type=F32)

    def outputs(c):
        rows = chunk_rows(c)
        decay = dec_ref[c]
        for h in range(HEADS):
            sl = slice(h * HEAD_DIM, (h + 1) * HEAD_DIM)
            st = st_ref[h]
            o = jnp.dot(a_ref[h, rows, :], i_ref[rows, sl], preferred_element_type=F32)
            o = o + lax.dot_general(qd_ref[rows, sl], st.astype(BF16), (((1,), (1,)), ((), ())),
                                    preferred_element_type=F32)
            st_ref[h] = st * decay[:, sl] + u_ref[c, h]
            o = _rms_rows(o, gn)
            o_ref[rows, sl] = (o * _silu(g_ref[rows, sl].astype(F32))).astype(o_ref.dtype)

    def unrolled(fn):
        def body(i, carry):
            for u in range(HGRN_UNROLL):
                fn(i * HGRN_UNROLL + u)
            return carry
        return body

    n_iter = TB // (CHUNK * HGRN_UNROLL)
    lax.fori_loop(0, n_iter, unrolled(local_terms), 0)
    lax.fori_loop(0, n_iter, unrolled(outputs), 0)


def _hgrn(proj, lb, gn, batch, seq, weights):
    t = proj.shape[0]
    nb = seq // TB
    steps = batch * nb

    def col(cb):
        return pl.BlockSpec((TB, GROUP), lambda b, i: (b * nb + i, cb))

    def slab(w):
        assert w.shape[0] % (steps * BF16_SUBLANES) == 0, w.shape
        return pl.BlockSpec((w.shape[0] // steps, w.shape[1]), lambda b, i: (b * nb + i, 0))

    return pl.pallas_call(
        functools.partial(_hgrn_kernel, len(weights)),
        grid=(batch, nb),
        in_specs=[col(3), col(4), col(5), col(6),
                  pl.BlockSpec((1, GROUP), lambda b, i: (0, 0)),
                  pl.BlockSpec((1, HEAD_DIM), lambda b, i: (0, 0))]
        + [slab(w) for w in weights],
        out_specs=[pl.BlockSpec((TB, GROUP), lambda b, i: (b * nb + i, 0))]
        + [slab(w) for w in weights],
        out_shape=[jax.ShapeDtypeStruct((t, GROUP), BF16)]
        + [jax.ShapeDtypeStruct(w.shape, BF16) for w in weights],
        scratch_shapes=[
            pltpu.VMEM((HEADS, HEAD_DIM, HEAD_DIM), F32),
            pltpu.VMEM((TB, GROUP), BF16),
            pltpu.VMEM((HEADS, TB, CHUNK), BF16),
            pltpu.VMEM((TB // CHUNK, HEADS, HEAD_DIM, HEAD_DIM), F32),
            pltpu.VMEM((TB // CHUNK, 1, GROUP), F32),
        ],
        compiler_params=_params("arbitrary", "arbitrary"),
        name="hgrn2",
    )(proj, proj, proj, proj, lb, gn, *weights)


OUT_COLS = 512


def _outproj_kernel(x_ref, a_ref, r_ref, w_ref, o_ref):
    mixed = jnp.concatenate([a_ref[...], r_ref[...]], axis=1)
    for n in range(0, o_ref.shape[1], OUT_COLS):
        cols = slice(n, n + OUT_COLS)
        o_ref[:, cols] = x_ref[:, cols] + jnp.dot(mixed, w_ref[:, cols],
                                                  preferred_element_type=F32)


def _outproj(x2d, attn, rec, w_bf16, tm):
    t, d = x2d.shape
    g = attn.shape[1]
    return pl.pallas_call(
        _outproj_kernel,
        grid=(t // tm,),
        in_specs=[
            pl.BlockSpec((tm, d), lambda i: (i, 0)),
            pl.BlockSpec((tm, g), lambda i: (i, 0)),
            pl.BlockSpec((tm, g), lambda i: (i, 0)),
            pl.BlockSpec((2 * g, d), lambda i: (0, 0)),
        ],
        out_specs=pl.BlockSpec((tm, d), lambda i: (i, 0)),
        out_shape=jax.ShapeDtypeStruct((t, d), F32),
        compiler_params=_params("arbitrary"),
        name="outproj",
    )(x2d, attn, rec, w_bf16)


FFN_DOWN_COLS = 512


def _ffn_kernel(x_ref, g2_ref, wg_ref, wu_ref, wd_ref, gf_ref, o_ref, h_ref):
    j = pl.program_id(1)
    last = pl.num_programs(1) - 1
    n_rows, d = x_ref.shape
    col_chunks = [slice(n, n + FFN_DOWN_COLS) for n in range(0, d, FFN_DOWN_COLS)]

    def swiglu_down(h):
        gate = jnp.dot(h, wg_ref[...], preferred_element_type=F32)
        up = jnp.dot(h, wu_ref[...], preferred_element_type=F32)
        act = (gate * jax.nn.sigmoid(gate) * up).astype(BF16)
        return [jnp.dot(act, wd_ref[:, cols], preferred_element_type=F32) for cols in col_chunks]

    @pl.when(j == 0)
    def _():
        g2 = g2_ref[...]
        for rows in _row_slabs(n_rows):
            h = _rms_rows(x_ref[rows, :], g2).astype(h_ref.dtype)
            h_ref[rows, :] = h
            for cols, y in zip(col_chunks, swiglu_down(h)):
                o_ref[rows, cols] = x_ref[rows, cols] + y

    @pl.when((j > 0) & (j < last))
    def _():
        for cols, y in zip(col_chunks, swiglu_down(h_ref[...])):
            o_ref[:, cols] += y

    @pl.when(j == last)
    def _():
        gf = gf_ref[...]
        for rows in _row_slabs(n_rows):
            ss = jnp.zeros((EDGE_ROWS, 1), F32)
            for cols, y in zip(col_chunks, swiglu_down(h_ref[rows, :])):
                o = o_ref[rows, cols] + y
                o_ref[rows, cols] = o
                ss = ss + (o * o).sum(axis=-1, keepdims=True)
            o_ref[rows, :] = o_ref[rows, :] * lax.rsqrt(ss / d + EPS) * gf


def _ffn(x1, g2, wg, wu, wd, gf, tm, tf):
    t, d = x1.shape
    ff = wg.shape[1]
    assert ff // tf >= 2, "the first and the last hidden-dimension step are separate code paths"
    return pl.pallas_call(
        _ffn_kernel,
        grid=(t // tm, ff // tf),
        in_specs=[
            pl.BlockSpec((tm, d), lambda i, j: (i, 0)),
            pl.BlockSpec((1, d), lambda i, j: (0, 0)),
            pl.BlockSpec((d, tf), lambda i, j: (0, j)),
            pl.BlockSpec((d, tf), lambda i, j: (0, j)),
            pl.BlockSpec((tf, d), lambda i, j: (j, 0)),
            pl.BlockSpec((1, d), lambda i, j: (0, 0)),
        ],
        out_specs=pl.BlockSpec((tm, d), lambda i, j: (i, 0)),
        out_shape=jax.ShapeDtypeStruct((t, d), F32),
        scratch_shapes=[pltpu.VMEM((tm, d), BF16)],
        compiler_params=_params("arbitrary", "arbitrary"),
        name="ffn",
    )(x1, g2, wg, wu, wd, gf)


def kernel(x, norm1_gain, w_in, rel_bias, lower_bounds, grn_norm_gain, attn_out_gain, w_out,
           norm2_gain, w_gate, w_up, w_down, final_gain):
    batch, seq, d = x.shape
    depth = w_in.shape[0]
    assert seq % TB == 0 and seq % QB == 0 and w_in.shape[2] == 7 * GROUP and d == 2 * GROUP
    lb_all = jnp.cumsum(jax.nn.softmax(lower_bounds.astype(F32), axis=0), axis=0)
    xf = x.reshape(batch * seq, d)
    for l in range(depth):
        proj = _inproj(xf, norm1_gain[l][None, :], w_in[l].astype(BF16), tm=1024, tn=1024)
        attn = _attention(proj, _bias_rows(rel_bias[l]), attn_out_gain[l][None, :], batch, seq)
        rec, wo, wg, wu, wd = _hgrn(proj, lb_all[l][None, :], grn_norm_gain[l][None, :], batch, seq,
                                    (w_out[l], w_gate[l], w_up[l], w_down[l]))
        x1 = _outproj(xf, attn, rec, wo, tm=512)
        last = l == depth - 1
        assert last, "the final rmsnorm is fused into the last layer's FFN"
        xf = _ffn(x1, norm2_gain[l][None, :], wg, wu, wd, final_gain[None, :], tm=1024, tf=512)
    return xf.reshape(batch, seq, d)
```

```python
import functools

import jax
import jax.numpy as jnp
from jax import lax
from jax.experimental import pallas as pl
from jax.experimental.pallas import tpu as pltpu

F32 = jnp.float32
BF16 = jnp.bfloat16

EPS = 1e-6
CHUNK = 64
N_LEFT = 8
MAX_REL = 128
HEADS = 8
HEAD_DIM = 128
GROUP = HEADS * HEAD_DIM

VMEM_LIMIT_BYTES = 56 * 1024 * 1024
BF16_SUBLANES = 16
NEG = -1e30
LOG2E = 1.4426950408889634
QUERY_SCALE = HEAD_DIM ** -0.5 * LOG2E


def _params(*sem):
    return pltpu.CompilerParams(dimension_semantics=sem, vmem_limit_bytes=VMEM_LIMIT_BYTES)


def _rms_rows(x, gain):
    ms = jnp.mean(x * x, axis=-1, keepdims=True)
    return x * lax.rsqrt(ms + EPS) * gain


EDGE_ROWS = 256


def _row_slabs(n_rows):
    return [slice(r, r + EDGE_ROWS) for r in range(0, n_rows, EDGE_ROWS)]


def _inproj_kernel(x_ref, gain_ref, w_ref, cs_ref, o_ref, h_ref):
    j = pl.program_id(1)
    col_scale = cs_ref[...]

    def project(h):
        return (jnp.dot(h, w_ref[...], preferred_element_type=F32) * col_scale).astype(o_ref.dtype)

    @pl.when(j == 0)
    def _():
        gain = gain_ref[...]
        for rows in _row_slabs(x_ref.shape[0]):
            h = _rms_rows(x_ref[rows, :], gain).astype(h_ref.dtype)
            h_ref[rows, :] = h
            o_ref[rows, :] = project(h)

    @pl.when(j > 0)
    def _():
        o_ref[...] = project(h_ref[...])


def _inproj(x2d, gain, w_bf16, col_scale, tm, tn):
    t, d = x2d.shape
    n = w_bf16.shape[1]
    return pl.pallas_call(
        _inproj_kernel,
        grid=(t // tm, n // tn),
        in_specs=[
            pl.BlockSpec((tm, d), lambda i, j: (i, 0)),
            pl.BlockSpec((1, d), lambda i, j: (0, 0)),
            pl.BlockSpec((d, tn), lambda i, j: (0, j)),
            pl.BlockSpec((1, tn), lambda i, j: (0, j)),
        ],
        out_specs=pl.BlockSpec((tm, tn), lambda i, j: (i, j)),
        out_shape=jax.ShapeDtypeStruct((t, n), BF16),
        scratch_shapes=[pltpu.VMEM((tm, d), BF16)],
        compiler_params=_params("arbitrary", "arbitrary"),
        name="inproj",
    )(x2d, gain, w_bf16, col_scale)


QB = 256
N_SEG = 1 + (N_LEFT * CHUNK) // QB
WIN = N_SEG * QB
ROLL_W = 1024


def _bias_rows(rel_bias):
    h = rel_bias.shape[0]
    far_past = rel_bias[:, 2 * MAX_REL:2 * MAX_REL + 1]
    far_future = rel_bias[:, 0:1]
    n_head = (WIN - QB) - MAX_REL
    n_tail = WIN - n_head - (2 * MAX_REL + 1)
    return jnp.concatenate([
        jnp.broadcast_to(far_past, (h, n_head)),
        rel_bias[:, ::-1],
        jnp.broadcast_to(far_future, (h, n_tail)),
        jnp.broadcast_to(far_past, (h, ROLL_W - WIN)),
    ], axis=1).astype(F32)


def _attn_kernel(q_ref, k0_ref, k1_ref, k2_ref, v0_ref, v1_ref, v2_ref, brow_ref, gain_ref,
                 o_ref, bias_ref, acc_ref, s_ref):
    qb = pl.program_id(1)

    @pl.when((pl.program_id(0) == 0) & (qb == 0))
    def _():
        qc = lax.broadcasted_iota(jnp.int32, (QB, WIN), 0) // CHUNK
        kc = lax.broadcasted_iota(jnp.int32, (QB, WIN), 1) // CHUNK
        band = (kc >= qc) & (kc <= qc + N_LEFT)
        for h in range(HEADS):
            row = jnp.broadcast_to(brow_ref[h:h + 1, :] * LOG2E, (QB, ROLL_W))
            toeplitz = pltpu.roll(row, 0, 1, stride=1, stride_axis=0)
            bias_ref[h] = jnp.where(band, toeplitz[:, :WIN], NEG)

    k_refs = (k0_ref, k1_ref, k2_ref)
    v_refs = (v0_ref, v1_ref, v2_ref)
    def lane_halves(op, x):
        while x.shape[1] > HEAD_DIM:
            half = x.shape[1] // 2
            x = op(x[:, :half], x[:, half:])
        return x

    def attend(segs):
        m = []
        for h in range(HEADS):
            sl = slice(h * HEAD_DIM, (h + 1) * HEAD_DIM)
            qh = q_ref[:, sl]
            mx = None
            for j in segs:
                cols = slice(j * QB, (j + 1) * QB)
                sj = lax.dot_general(qh, k_refs[j][:, sl], (((1,), (1,)), ((), ())),
                                     preferred_element_type=F32) + bias_ref[h, :, cols]
                s_ref[h, :, cols] = sj
                mx = sj if mx is None else jnp.maximum(mx, sj)
            m.append(lane_halves(jnp.maximum, mx).max(axis=-1, keepdims=True))
        sq = jnp.zeros((QB, HEAD_DIM), F32)
        for h in range(HEADS):
            sl = slice(h * HEAD_DIM, (h + 1) * HEAD_DIM)
            psum = None
            o = None
            for j in segs:
                p = jnp.exp2(s_ref[h, :, j * QB:(j + 1) * QB] - m[h])
                pv = jnp.dot(p.astype(BF16), v_refs[j][:, sl], preferred_element_type=F32)
                psum = p if psum is None else psum + p
                o = pv if o is None else o + pv
            l = lane_halves(jnp.add, psum).sum(axis=-1, keepdims=True)
            o = o * (1.0 / l)
            acc_ref[:, sl] = o
            sq = sq + o * o
        inv = lax.rsqrt(sq.sum(axis=-1, keepdims=True) / GROUP + EPS)
        o_ref[...] = (acc_ref[...] * inv * gain_ref[...]).astype(o_ref.dtype)

    for n_valid in range(1, N_SEG + 1):
        @pl.when(jnp.minimum(qb, N_SEG - 1) == n_valid - 1)
        def _():
            attend(tuple(range(N_SEG - n_valid, N_SEG)))


def _attention(proj, brow, gain, batch, seq):
    t = proj.shape[0]
    nqb = seq // QB

    def kv_spec(col_block, seg):
        def index(b, i):
            return (b * nqb + jnp.maximum(i - (N_SEG - 1 - seg), 0), col_block)
        return pl.BlockSpec((QB, GROUP), index)

    return pl.pallas_call(
        _attn_kernel,
        grid=(batch, nqb),
        in_specs=[pl.BlockSpec((QB, GROUP), lambda b, i: (b * nqb + i, 0))]
        + [kv_spec(1, s) for s in range(N_SEG)]
        + [kv_spec(2, s) for s in range(N_SEG)]
        + [pl.BlockSpec((HEADS, ROLL_W), lambda b, i: (0, 0)),
           pl.BlockSpec((1, GROUP), lambda b, i: (0, 0))],
        out_specs=pl.BlockSpec((QB, GROUP), lambda b, i: (b * nqb + i, 0)),
        out_shape=jax.ShapeDtypeStruct((t, GROUP), BF16),
        scratch_shapes=[pltpu.VMEM((HEADS, QB, WIN), F32),
                        pltpu.VMEM((QB, GROUP), F32),
                        pltpu.VMEM((HEADS, QB, WIN), F32)],
        compiler_params=_params("arbitrary", "arbitrary"),
        name="attention",
    )(proj, proj, proj, proj, proj, proj, proj, brow, gain)


TB = 512
HGRN_UNROLL = 4


def _silu(x):
    hx = 0.5 * x
    return hx + hx * jnp.tanh(hx)


def _split3(x):
    hi = x.astype(BF16)
    r1 = x - hi.astype(F32)
    mid = r1.astype(BF16)
    lo = (r1 - mid.astype(F32)).astype(BF16)
    return hi, mid, lo


def _hgrn_kernel(n_cast, q_ref, f_ref, i_ref, g_ref, lb_ref, gn_ref, *refs):
    cast_in, (o_ref, *cast_out) = refs[:n_cast], refs[n_cast:2 * n_cast + 1]
    st_ref, qd_ref, a_ref, u_ref, dec_ref = refs[2 * n_cast + 1:]

    for src, dst in zip(cast_in, cast_out):
        if len(dst.shape) == 2:
            dst[...] = src[...].astype(dst.dtype)
        else:
            width = dst.shape[2]
            for n in range(dst.shape[0]):
                dst[n] = src[:, n * width:(n + 1) * width].astype(dst.dtype)

    @pl.when(pl.program_id(1) == 0)
    def _():
        st_ref[...] = jnp.zeros_like(st_ref)

    lb = lb_ref[...]
    f_mid = 0.5 * (1.0 + lb)
    f_amp = 0.5 * (1.0 - lb)
    gn = gn_ref[...]
    r_i = lax.broadcasted_iota(jnp.int32, (CHUNK, CHUNK), 0)
    c_i = lax.broadcasted_iota(jnp.int32, (CHUNK, CHUNK), 1)
    causal = r_i >= c_i
    tril = causal.astype(BF16)
    tril3 = jnp.concatenate([tril, tril, tril], axis=1)

    def chunk_rows(c):
        return pl.ds(pl.multiple_of(c * CHUNK, CHUNK), CHUNK)

    def local_terms(c):
        rows = chunk_rows(c)
        q = _silu(q_ref[rows, :].astype(F32))
        f = f_mid + f_amp * jnp.tanh(0.5 * f_ref[rows, :].astype(F32))
        k = 1.0 - f
        b = jnp.dot(tril3, jnp.concatenate(_split3(jnp.log2(f)), axis=0),
                    preferred_element_type=F32)
        decay = jnp.exp2(b[CHUNK - 1:CHUNK, :])
        q_dec = (q * jnp.exp2(b)).astype(BF16)
        k_undecayed = k * jnp.exp2(-b)
        k_intra = k_undecayed.astype(BF16)
        k_state = (k_undecayed * decay).astype(BF16)
        qd_ref[rows, :] = q_dec
        dec_ref[c] = decay
        for h in range(HEADS):
            sl = slice(h * HEAD_DIM, (h + 1) * HEAD_DIM)
            a = lax.dot_general(q_dec[:, sl], k_intra[:, sl], (((1,), (1,)), ((), ())),
                                preferred_element_type=F32)
            a_ref[h, rows, :] = jnp.where(causal, a, 0.0).astype(BF16)
            u_ref[c, h] = lax.dot_general(i_ref[rows, sl], k_state[:, sl], (((0,), (0,)), ((), ())),
                                          preferred_element_type=F32)

    def outputs(c):
        rows = chunk_rows(c)
        decay = dec_ref[c]
        for h in range(HEADS):
            sl = slice(h * HEAD_DIM, (h + 1) * HEAD_DIM)
            st = st_ref[h]
            o = jnp.dot(a_ref[h, rows, :], i_ref[rows, sl], preferred_element_type=F32)
            o = o + lax.dot_general(qd_ref[rows, sl], st.astype(BF16), (((1,), (1,)), ((), ())),
                                    preferred_element_type=F32)
            st_ref[h] = st * decay[:, sl] + u_ref[c, h]
            o = _rms_rows(o, gn)
            o_ref[rows, sl] = (o * _silu(g_ref[rows, sl].astype(F32))).astype(o_ref.dtype)

    def unrolled(fn):
        def body(i, carry):
            for u in range(HGRN_UNROLL):
                fn(i * HGRN_UNROLL + u)
            return carry
        return body

    n_iter = TB // (CHUNK * HGRN_UNROLL)
    lax.fori_loop(0, n_iter, unrolled(local_terms), 0)
    lax.fori_loop(0, n_iter, unrolled(outputs), 0)


def _hgrn(proj, lb, gn, batch, seq, weights):
    t = proj.shape[0]
    nb = seq // TB
    steps = batch * nb
    for w, _ in weights:
        assert w.shape[0] % (steps * BF16_SUBLANES) == 0, w.shape

    def col(cb):
        return pl.BlockSpec((TB, GROUP), lambda b, i: (b * nb + i, cb))

    def slab(w):
        return pl.BlockSpec((w.shape[0] // steps, w.shape[1]), lambda b, i: (b * nb + i, 0))

    def cast_shape(w, tile):
        return w.shape if tile is None else (w.shape[1] // tile, w.shape[0], tile)

    def cast_slab(w, tile):
        if tile is None:
            return slab(w)
        return pl.BlockSpec((w.shape[1] // tile, w.shape[0] // steps, tile),
                            lambda b, i: (0, b * nb + i, 0))

    return pl.pallas_call(
        functools.partial(_hgrn_kernel, len(weights)),
        grid=(batch, nb),
        in_specs=[col(3), col(4), col(5), col(6),
                  pl.BlockSpec((1, GROUP), lambda b, i: (0, 0)),
                  pl.BlockSpec((1, HEAD_DIM), lambda b, i: (0, 0))]
        + [slab(w) for w, _ in weights],
        out_specs=[pl.BlockSpec((TB, GROUP), lambda b, i: (b * nb + i, 0))]
        + [cast_slab(w, tile) for w, tile in weights],
        out_shape=[jax.ShapeDtypeStruct((t, GROUP), BF16)]
        + [jax.ShapeDtypeStruct(cast_shape(w, tile), BF16) for w, tile in weights],
        scratch_shapes=[
            pltpu.VMEM((HEADS, HEAD_DIM, HEAD_DIM), F32),
            pltpu.VMEM((TB, GROUP), BF16),
            pltpu.VMEM((HEADS, TB, CHUNK), BF16),
            pltpu.VMEM((TB // CHUNK, HEADS, HEAD_DIM, HEAD_DIM), F32),
            pltpu.VMEM((TB // CHUNK, 1, GROUP), F32),
        ],
        compiler_params=_params("arbitrary", "arbitrary"),
        name="hgrn2",
    )(proj, proj, proj, proj, lb, gn, *[w for w, _ in weights])


OUT_COLS = 512


def _outproj_kernel(x_ref, a_ref, r_ref, w_ref, o_ref):
    mixed = jnp.concatenate([a_ref[...], r_ref[...]], axis=1)
    for n in range(0, o_ref.shape[1], OUT_COLS):
        cols = slice(n, n + OUT_COLS)
        o_ref[:, cols] = x_ref[:, cols] + jnp.dot(mixed, w_ref[:, cols],
                                                  preferred_element_type=F32)


def _outproj(x2d, attn, rec, w_bf16, tm):
    t, d = x2d.shape
    g = attn.shape[1]
    return pl.pallas_call(
        _outproj_kernel,
        grid=(t // tm,),
        in_specs=[
            pl.BlockSpec((tm, d), lambda i: (i, 0)),
            pl.BlockSpec((tm, g), lambda i: (i, 0)),
            pl.BlockSpec((tm, g), lambda i: (i, 0)),
            pl.BlockSpec((2 * g, d), lambda i: (0, 0)),
        ],
        out_specs=pl.BlockSpec((tm, d), lambda i: (i, 0)),
        out_shape=jax.ShapeDtypeStruct((t, d), F32),
        compiler_params=_params("arbitrary"),
        name="outproj",
    )(x2d, attn, rec, w_bf16)


FFN_DOWN_COLS = 512
FFN_HIDDEN_TILE = 512


def _ffn_kernel(x_ref, g2_ref, wg_ref, wu_ref, wd_ref, gf_ref, o_ref, h_ref):
    j = pl.program_id(1)
    last = pl.num_programs(1) - 1
    n_rows, d = x_ref.shape
    col_chunks = [slice(n, n + FFN_DOWN_COLS) for n in range(0, d, FFN_DOWN_COLS)]

    def swiglu_down(h):
        gate = jnp.dot(h, wg_ref[...], preferred_element_type=F32)
        up = jnp.dot(h, wu_ref[...], preferred_element_type=F32)
        act = (gate * jax.nn.sigmoid(gate) * up).astype(BF16)
        return [jnp.dot(act, wd_ref[:, cols], preferred_element_type=F32) for cols in col_chunks]

    @pl.when(j == 0)
    def _():
        g2 = g2_ref[...]
        for rows in _row_slabs(n_rows):
            h = _rms_rows(x_ref[rows, :], g2).astype(h_ref.dtype)
            h_ref[rows, :] = h
            for cols, y in zip(col_chunks, swiglu_down(h)):
                o_ref[rows, cols] = x_ref[rows, cols] + y

    @pl.when((j > 0) & (j < last))
    def _():
        for cols, y in zip(col_chunks, swiglu_down(h_ref[...])):
            o_ref[:, cols] += y

    @pl.when(j == last)
    def _():
        gf = gf_ref[...]
        for rows in _row_slabs(n_rows):
            ss = jnp.zeros((EDGE_ROWS, 1), F32)
            for cols, y in zip(col_chunks, swiglu_down(h_ref[rows, :])):
                o = o_ref[rows, cols] + y
                o_ref[rows, cols] = o
                ss = ss + (o * o).sum(axis=-1, keepdims=True)
            o_ref[rows, :] = o_ref[rows, :] * lax.rsqrt(ss / d + EPS) * gf


def _ffn(x1, g2, wg_tiles, wu_tiles, wd, gf, tm):
    t, d = x1.shape
    n_tiles, _, tf = wg_tiles.shape
    assert n_tiles >= 2, "the first and the last hidden-dimension step are separate code paths"
    return pl.pallas_call(
        _ffn_kernel,
        grid=(t // tm, n_tiles),
        in_specs=[
            pl.BlockSpec((tm, d), lambda i, j: (i, 0)),
            pl.BlockSpec((1, d), lambda i, j: (0, 0)),
            pl.BlockSpec((None, d, tf), lambda i, j: (j, 0, 0)),
            pl.BlockSpec((None, d, tf), lambda i, j: (j, 0, 0)),
            pl.BlockSpec((tf, d), lambda i, j: (j, 0)),
            pl.BlockSpec((1, d), lambda i, j: (0, 0)),
        ],
        out_specs=pl.BlockSpec((tm, d), lambda i, j: (i, 0)),
        out_shape=jax.ShapeDtypeStruct((t, d), F32),
        scratch_shapes=[pltpu.VMEM((tm, d), BF16)],
        compiler_params=_params("arbitrary", "arbitrary"),
        name="ffn",
    )(x1, g2, wg_tiles, wu_tiles, wd, gf)


def kernel(x, norm1_gain, w_in, rel_bias, lower_bounds, grn_norm_gain, attn_out_gain, w_out,
           norm2_gain, w_gate, w_up, w_down, final_gain):
    batch, seq, d = x.shape
    depth = w_in.shape[0]
    assert seq % TB == 0 and seq % QB == 0 and w_in.shape[2] == 7 * GROUP and d == 2 * GROUP
    lb_all = jnp.cumsum(jax.nn.softmax(lower_bounds.astype(F32), axis=0), axis=0)
    xf = x.reshape(batch * seq, d)
    col_scale = jnp.concatenate([jnp.full((1, GROUP), QUERY_SCALE, F32),
                                 jnp.ones((1, w_in.shape[2] - GROUP), F32)], axis=1)
    for l in range(depth):
        proj = _inproj(xf, norm1_gain[l][None, :], w_in[l].astype(BF16), col_scale,
                       tm=1024, tn=1792)
        attn = _attention(proj, _bias_rows(rel_bias[l]), attn_out_gain[l][None, :], batch, seq)
        rec, wo, wg, wu, wd = _hgrn(proj, lb_all[l][None, :], grn_norm_gain[l][None, :], batch, seq,
                                    [(w_out[l], None), (w_gate[l], FFN_HIDDEN_TILE),
                                     (w_up[l], FFN_HIDDEN_TILE), (w_down[l], None)])
        x1 = _outproj(xf, attn, rec, wo, tm=512)
        last = l == depth - 1
        assert last, "the final rmsnorm is fused into the last layer's FFN"
        xf = _ffn(x1, norm2_gain[l][None, :], wg, wu, wd, final_gain[None, :], tm=1024)
    return xf.reshape(batch, seq, d)
```

```python
import functools

import jax
import jax.numpy as jnp
from jax import lax
from jax.experimental import pallas as pl
from jax.experimental.pallas import tpu as pltpu

F32 = jnp.float32
BF16 = jnp.bfloat16

EPS = 1e-6
CHUNK = 64
N_LEFT = 8
MAX_REL = 128
HEADS = 8
HEAD_DIM = 128
GROUP = HEADS * HEAD_DIM

VMEM_LIMIT_BYTES = 56 * 1024 * 1024
BF16_SUBLANES = 16
NEG = -1e30
LOG2E = 1.4426950408889634
QUERY_SCALE = HEAD_DIM ** -0.5 * LOG2E


def _params(*sem):
    return pltpu.CompilerParams(dimension_semantics=sem, vmem_limit_bytes=VMEM_LIMIT_BYTES)


def _rms_rows(x, gain):
    ms = jnp.mean(x * x, axis=-1, keepdims=True)
    return x * lax.rsqrt(ms + EPS) * gain


EDGE_ROWS = 256


def _row_slabs(n_rows):
    return [slice(r, r + EDGE_ROWS) for r in range(0, n_rows, EDGE_ROWS)]


def _inproj_kernel(x_ref, gain_ref, w_ref, cs_ref, o_ref, h_ref):
    j = pl.program_id(1)
    col_scale = cs_ref[...]

    def project(h):
        return (jnp.dot(h, w_ref[...], preferred_element_type=F32) * col_scale).astype(o_ref.dtype)

    @pl.when(j == 0)
    def _():
        gain = gain_ref[...]
        for rows in _row_slabs(x_ref.shape[0]):
            h = _rms_rows(x_ref[rows, :], gain).astype(h_ref.dtype)
            h_ref[rows, :] = h
            o_ref[rows, :] = project(h)

    @pl.when(j > 0)
    def _():
        o_ref[...] = project(h_ref[...])


def _inproj(x2d, gain, w_bf16, col_scale, tm, tn):
    t, d = x2d.shape
    n = w_bf16.shape[1]
    return pl.pallas_call(
        _inproj_kernel,
        grid=(t // tm, n // tn),
        in_specs=[
            pl.BlockSpec((tm, d), lambda i, j: (i, 0)),
            pl.BlockSpec((1, d), lambda i, j: (0, 0)),
            pl.BlockSpec((d, tn), lambda i, j: (0, j)),
            pl.BlockSpec((1, tn), lambda i, j: (0, j)),
        ],
        out_specs=pl.BlockSpec((tm, tn), lambda i, j: (i, j)),
        out_shape=jax.ShapeDtypeStruct((t, n), BF16),
        scratch_shapes=[pltpu.VMEM((tm, d), BF16)],
        compiler_params=_params("arbitrary", "arbitrary"),
        name="inproj",
    )(x2d, gain, w_bf16, col_scale)


QB = 256
N_SEG = 1 + (N_LEFT * CHUNK) // QB
WIN = N_SEG * QB
ROLL_W = 1024


def _bias_rows(rel_bias):
    h = rel_bias.shape[0]
    far_past = rel_bias[:, 2 * MAX_REL:2 * MAX_REL + 1]
    far_future = rel_bias[:, 0:1]
    n_head = (WIN - QB) - MAX_REL
    n_tail = WIN - n_head - (2 * MAX_REL + 1)
    return jnp.concatenate([
        jnp.broadcast_to(far_past, (h, n_head)),
        rel_bias[:, ::-1],
        jnp.broadcast_to(far_future, (h, n_tail)),
        jnp.broadcast_to(far_past, (h, ROLL_W - WIN)),
    ], axis=1).astype(F32)


def _attn_kernel(q_ref, k0_ref, k1_ref, k2_ref, v0_ref, v1_ref, v2_ref, brow_ref, gain_ref,
                 o_ref, bias_ref, acc_ref, s_ref):
    qb = pl.program_id(1)

    @pl.when((pl.program_id(0) == 0) & (qb == 0))
    def _():
        qc = lax.broadcasted_iota(jnp.int32, (QB, WIN), 0) // CHUNK
        kc = lax.broadcasted_iota(jnp.int32, (QB, WIN), 1) // CHUNK
        band = (kc >= qc) & (kc <= qc + N_LEFT)
        for h in range(HEADS):
            row = jnp.broadcast_to(brow_ref[h:h + 1, :] * LOG2E, (QB, ROLL_W))
            toeplitz = pltpu.roll(row, 0, 1, stride=1, stride_axis=0)
            bias_ref[h] = jnp.where(band, toeplitz[:, :WIN], NEG)

    k_refs = (k0_ref, k1_ref, k2_ref)
    v_refs = (v0_ref, v1_ref, v2_ref)
    def lane_halves(op, x):
        while x.shape[1] > HEAD_DIM:
            half = x.shape[1] // 2
            x = op(x[:, :half], x[:, half:])
        return x

    def attend(segs):
        m = []
        for h in range(HEADS):
            sl = slice(h * HEAD_DIM, (h + 1) * HEAD_DIM)
            qh = q_ref[:, sl]
            mx = None
            for j in segs:
                cols = slice(j * QB, (j + 1) * QB)
                sj = lax.dot_general(qh, k_refs[j][:, sl], (((1,), (1,)), ((), ())),
                                     preferred_element_type=F32) + bias_ref[h, :, cols]
                s_ref[h, :, cols] = sj
                mx = sj if mx is None else jnp.maximum(mx, sj)
            m.append(lane_halves(jnp.maximum, mx).max(axis=-1, keepdims=True))
        sq = jnp.zeros((QB, HEAD_DIM), F32)
        ones = jnp.ones((QB, HEAD_DIM), BF16)
        for h in range(HEADS):
            sl = slice(h * HEAD_DIM, (h + 1) * HEAD_DIM)
            p = jnp.concatenate(
                [jnp.exp2(s_ref[h, :, j * QB:(j + 1) * QB] - m[h]).astype(BF16) for j in segs], axis=1)
            v1 = jnp.concatenate(
                [jnp.concatenate([v_refs[j][:, sl], ones], axis=1) for j in segs], axis=0)
            pv = jnp.dot(p, v1, preferred_element_type=F32)
            o = pv[:, :HEAD_DIM] * (1.0 / pv[:, HEAD_DIM:])
            acc_ref[:, sl] = o
            sq = sq + o * o
        inv = lax.rsqrt(sq.sum(axis=-1, keepdims=True) / GROUP + EPS)
        o_ref[...] = (acc_ref[...] * inv * gain_ref[...]).astype(o_ref.dtype)

    for n_valid in range(1, N_SEG + 1):
        @pl.when(jnp.minimum(qb, N_SEG - 1) == n_valid - 1)
        def _():
            attend(tuple(range(N_SEG - n_valid, N_SEG)))


def _attention(proj, brow, gain, batch, seq):
    t = proj.shape[0]
    nqb = seq // QB

    def kv_spec(col_block, seg):
        def index(b, i):
            return (b * nqb + jnp.maximum(i - (N_SEG - 1 - seg), 0), col_block)
        return pl.BlockSpec((QB, GROUP), index)

    return pl.pallas_call(
        _attn_kernel,
        grid=(batch, nqb),
        in_specs=[pl.BlockSpec((QB, GROUP), lambda b, i: (b * nqb + i, 0))]
        + [kv_spec(1, s) for s in range(N_SEG)]
        + [kv_spec(2, s) for s in range(N_SEG)]
        + [pl.BlockSpec((HEADS, ROLL_W), lambda b, i: (0, 0)),
           pl.BlockSpec((1, GROUP), lambda b, i: (0, 0))],
        out_specs=pl.BlockSpec((QB, GROUP), lambda b, i: (b * nqb + i, 0)),
        out_shape=jax.ShapeDtypeStruct((t, GROUP), BF16),
        scratch_shapes=[pltpu.VMEM((HEADS, QB, WIN), F32),
                        pltpu.VMEM((QB, GROUP), F32),
                        pltpu.VMEM((HEADS, QB, WIN), F32)],
        compiler_params=_params("arbitrary", "arbitrary"),
        name="attention",
    )(proj, proj, proj, proj, proj, proj, proj, brow, gain)


TB = 512
HGRN_UNROLL = 4


def _silu(x):
    hx = 0.5 * x
    return hx + hx * jnp.tanh(hx)


def _split3(x):
    hi = x.astype(BF16)
    r1 = x - hi.astype(F32)
    mid = r1.astype(BF16)
    lo = (r1 - mid.astype(F32)).astype(BF16)
    return hi, mid, lo


def _hgrn_kernel(n_cast, q_ref, f_ref, i_ref, g_ref, lb_ref, gn_ref, *refs):
    cast_in, (o_ref, *cast_out) = refs[:n_cast], refs[n_cast:2 * n_cast + 1]
    st_ref, qd_ref, a_ref, u_ref, dec_ref = refs[2 * n_cast + 1:]

    for src, dst in zip(cast_in, cast_out):
        if len(dst.shape) == 2:
            dst[...] = src[...].astype(dst.dtype)
        else:
            width = dst.shape[2]
            for n in range(dst.shape[0]):
                dst[n] = src[:, n * width:(n + 1) * width].astype(dst.dtype)

    @pl.when(pl.program_id(1) == 0)
    def _():
        st_ref[...] = jnp.zeros_like(st_ref)

    lb = lb_ref[...]
    f_mid = 0.5 * (1.0 + lb)
    f_amp = 0.5 * (1.0 - lb)
    gn = gn_ref[...]
    r_i = lax.broadcasted_iota(jnp.int32, (CHUNK, CHUNK), 0)
    c_i = lax.broadcasted_iota(jnp.int32, (CHUNK, CHUNK), 1)
    causal = r_i >= c_i
    tril = causal.astype(BF16)
    tril3 = jnp.concatenate([tril, tril, tril], axis=1)

    def chunk_rows(c):
        return pl.ds(pl.multiple_of(c * CHUNK, CHUNK), CHUNK)

    def local_terms(c):
        rows = chunk_rows(c)
        q = _silu(q_ref[rows, :].astype(F32))
        f = f_mid + f_amp * jnp.tanh(0.5 * f_ref[rows, :].astype(F32))
        k = 1.0 - f
        b = jnp.dot(tril3, jnp.concatenate(_split3(jnp.log2(f)), axis=0),
                    preferred_element_type=F32)
        decay = jnp.exp2(b[CHUNK - 1:CHUNK, :])
        q_dec = (q * jnp.exp2(b)).astype(BF16)
        k_undecayed = k * jnp.exp2(-b)
        k_intra = k_undecayed.astype(BF16)
        k_state = (k_undecayed * decay).astype(BF16)
        qd_ref[rows, :] = q_dec
        dec_ref[c] = decay
        for h in range(HEADS):
            sl = slice(h * HEAD_DIM, (h + 1) * HEAD_DIM)
            a = lax.dot_general(q_dec[:, sl], k_intra[:, sl], (((1,), (1,)), ((), ())),
                                preferred_element_type=F32)
            a_ref[h, rows, :] = jnp.where(causal, a, 0.0).astype(BF16)
            u_ref[c, h] = lax.dot_general(i_ref[rows, sl], k_state[:, sl], (((0,), (0,)), ((), ())),
                                          preferred_element_type=F32)

    def outputs(c):
        rows = chunk_rows(c)
        decay = dec_ref[c]
        for h in range(HEADS):
            sl = slice(h * HEAD_DIM, (h + 1) * HEAD_DIM)
            st = st_ref[h]
            o = jnp.dot(a_ref[h, rows, :], i_ref[rows, sl], preferred_element_type=F32)
            o = o + lax.dot_general(qd_ref[rows, sl], st.astype(BF16), (((1,), (1,)), ((), ())),
                                    preferred_element_type=F32)
            st_ref[h] = st * decay[:, sl] + u_ref[c, h]
            o = _rms_rows(o, gn)
            o_ref[rows, sl] = (o * _silu(g_ref[rows, sl].astype(F32))).astype(o_ref.dtype)

    def unrolled(fn):
        def body(i, carry):
            for u in range(HGRN_UNROLL):
                fn(i * HGRN_UNROLL + u)
            return carry
        return body

    n_iter = TB // (CHUNK * HGRN_UNROLL)
    lax.fori_loop(0, n_iter, unrolled(local_terms), 0)
    lax.fori_loop(0, n_iter, unrolled(outputs), 0)


def _hgrn(proj, lb, gn, batch, seq, weights):
    t = proj.shape[0]
    nb = seq // TB
    steps = batch * nb
    for w, _ in weights:
        assert w.shape[0] % (steps * BF16_SUBLANES) == 0, w.shape

    def col(cb):
        return pl.BlockSpec((TB, GROUP), lambda b, i: (b * nb + i, cb))

    def slab(w):
        return pl.BlockSpec((w.shape[0] // steps, w.shape[1]), lambda b, i: (b * nb + i, 0))

    def cast_shape(w, tile):
        return w.shape if tile is None else (w.shape[1] // tile, w.shape[0], tile)

    def cast_slab(w, tile):
        if tile is None:
            return slab(w)
        return pl.BlockSpec((w.shape[1] // tile, w.shape[0] // steps, tile),
                            lambda b, i: (0, b * nb + i, 0))

    return pl.pallas_call(
        functools.partial(_hgrn_kernel, len(weights)),
        grid=(batch, nb),
        in_specs=[col(3), col(4), col(5), col(6),
                  pl.BlockSpec((1, GROUP), lambda b, i: (0, 0)),
                  pl.BlockSpec((1, HEAD_DIM), lambda b, i: (0, 0))]
        + [slab(w) for w, _ in weights],
        out_specs=[pl.BlockSpec((TB, GROUP), lambda b, i: (b * nb + i, 0))]
        + [cast_slab(w, tile) for w, tile in weights],
        out_shape=[jax.ShapeDtypeStruct((t, GROUP), BF16)]
        + [jax.ShapeDtypeStruct(cast_shape(w, tile), BF16) for w, tile in weights],
        scratch_shapes=[
            pltpu.VMEM((HEADS, HEAD_DIM, HEAD_DIM), F32),
            pltpu.VMEM((TB, GROUP), BF16),
            pltpu.VMEM((HEADS, TB, CHUNK), BF16),
            pltpu.VMEM((TB // CHUNK, HEADS, HEAD_DIM, HEAD_DIM), F32),
            pltpu.VMEM((TB // CHUNK, 1, GROUP), F32),
        ],
        compiler_params=_params("arbitrary", "arbitrary"),
        name="hgrn2",
    )(proj, proj, proj, proj, lb, gn, *[w for w, _ in weights])


OUT_COLS = 512


def _outproj_kernel(x_ref, a_ref, r_ref, w_ref, o_ref):
    mixed = jnp.concatenate([a_ref[...], r_ref[...]], axis=1)
    for n in range(0, o_ref.shape[1], OUT_COLS):
        cols = slice(n, n + OUT_COLS)
        o_ref[:, cols] = x_ref[:, cols] + jnp.dot(mixed, w_ref[:, cols],
                                                  preferred_element_type=F32)


def _outproj(x2d, attn, rec, w_bf16, tm):
    t, d = x2d.shape
    g = attn.shape[1]
    return pl.pallas_call(
        _outproj_kernel,
        grid=(t // tm,),
        in_specs=[
            pl.BlockSpec((tm, d), lambda i: (i, 0)),
            pl.BlockSpec((tm, g), lambda i: (i, 0)),
            pl.BlockSpec((tm, g), lambda i: (i, 0)),
            pl.BlockSpec((2 * g, d), lambda i: (0, 0)),
        ],
        out_specs=pl.BlockSpec((tm, d), lambda i: (i, 0)),
        out_shape=jax.ShapeDtypeStruct((t, d), F32),
        compiler_params=_params("arbitrary"),
        name="outproj",
    )(x2d, attn, rec, w_bf16)


FFN_DOWN_COLS = 512
FFN_HIDDEN_TILE = 512


def _ffn_kernel(x_ref, g2_ref, wg_ref, wu_ref, wd_ref, gf_ref, o_ref, h_ref):
    j = pl.program_id(1)
    last = pl.num_programs(1) - 1
    n_rows, d = x_ref.shape
    col_chunks = [slice(n, n + FFN_DOWN_COLS) for n in range(0, d, FFN_DOWN_COLS)]

    def swiglu_down(h):
        gate = jnp.dot(h, wg_ref[...], preferred_element_type=F32)
        up = jnp.dot(h, wu_ref[...], preferred_element_type=F32)
        act = (gate * jax.nn.sigmoid(gate) * up).astype(BF16)
        return [jnp.dot(act, wd_ref[:, cols], preferred_element_type=F32) for cols in col_chunks]

    @pl.when(j == 0)
    def _():
        g2 = g2_ref[...]
        for rows in _row_slabs(n_rows):
            h = _rms_rows(x_ref[rows, :], g2).astype(h_ref.dtype)
            h_ref[rows, :] = h
            for cols, y in zip(col_chunks, swiglu_down(h)):
                o_ref[rows, cols] = x_ref[rows, cols] + y

    @pl.when((j > 0) & (j < last))
    def _():
        for cols, y in zip(col_chunks, swiglu_down(h_ref[...])):
            o_ref[:, cols] += y

    @pl.when(j == last)
    def _():
        gf = gf_ref[...]
        for rows in _row_slabs(n_rows):
            ss = jnp.zeros((EDGE_ROWS, 1), F32)
            for cols, y in zip(col_chunks, swiglu_down(h_ref[rows, :])):
                o = o_ref[rows, cols] + y
                o_ref[rows, cols] = o
                ss = ss + (o * o).sum(axis=-1, keepdims=True)
            o_ref[rows, :] = o_ref[rows, :] * lax.rsqrt(ss / d + EPS) * gf


def _ffn(x1, g2, wg_tiles, wu_tiles, wd, gf, tm):
    t, d = x1.shape
    n_tiles, _, tf = wg_tiles.shape
    assert n_tiles >= 2, "the first and the last hidden-dimension step are separate code paths"
    return pl.pallas_call(
        _ffn_kernel,
        grid=(t // tm, n_tiles),
        in_specs=[
            pl.BlockSpec((tm, d), lambda i, j: (i, 0)),
            pl.BlockSpec((1, d), lambda i, j: (0, 0)),
            pl.BlockSpec((None, d, tf), lambda i, j: (j, 0, 0)),
            pl.BlockSpec((None, d, tf), lambda i, j: (j, 0, 0)),
            pl.BlockSpec((tf, d), lambda i, j: (j, 0)),
            pl.BlockSpec((1, d), lambda i, j: (0, 0)),
        ],
        out_specs=pl.BlockSpec((tm, d), lambda i, j: (i, 0)),
        out_shape=jax.ShapeDtypeStruct((t, d), F32),
        scratch_shapes=[pltpu.VMEM((tm, d), BF16)],
        compiler_params=_params("arbitrary", "arbitrary"),
        name="ffn",
    )(x1, g2, wg_tiles, wu_tiles, wd, gf)


def kernel(x, norm1_gain, w_in, rel_bias, lower_bounds, grn_norm_gain, attn_out_gain, w_out,
           norm2_gain, w_gate, w_up, w_down, final_gain):
    batch, seq, d = x.shape
    depth = w_in.shape[0]
    assert seq % TB == 0 and seq % QB == 0 and w_in.shape[2] == 7 * GROUP and d == 2 * GROUP
    lb_all = jnp.cumsum(jax.nn.softmax(lower_bounds.astype(F32), axis=0), axis=0)
    xf = x.reshape(batch * seq, d)
    col_scale = jnp.concatenate([jnp.full((1, GROUP), QUERY_SCALE, F32),
                                 jnp.ones((1, w_in.shape[2] - GROUP), F32)], axis=1)
    for l in range(depth):
        proj = _inproj(xf, norm1_gain[l][None, :], w_in[l].astype(BF16), col_scale,
                       tm=1024, tn=1792)
        attn = _attention(proj, _bias_rows(rel_bias[l]), attn_out_gain[l][None, :], batch, seq)
        rec, wo, wg, wu, wd = _hgrn(proj, lb_all[l][None, :], grn_norm_gain[l][None, :], batch, seq,
                                    [(w_out[l], None), (w_gate[l], FFN_HIDDEN_TILE),
                                     (w_up[l], FFN_HIDDEN_TILE), (w_down[l], None)])
        x1 = _outproj(xf, attn, rec, wo, tm=512)
        last = l == depth - 1
        assert last, "the final rmsnorm is fused into the last layer's FFN"
        xf = _ffn(x1, norm2_gain[l][None, :], wg, wu, wd, final_gain[None, :], tm=1024)
    return xf.reshape(batch, seq, d)
```

```python
import functools

import jax
import jax.numpy as jnp
from jax import lax
from jax.experimental import pallas as pl
from jax.experimental.pallas import tpu as pltpu

F32 = jnp.float32
BF16 = jnp.bfloat16

EPS = 1e-6
CHUNK = 64
N_LEFT = 8
MAX_REL = 128
HEADS = 8
HEAD_DIM = 128
GROUP = HEADS * HEAD_DIM

VMEM_LIMIT_BYTES = 56 * 1024 * 1024
BF16_SUBLANES = 16
NEG = -1e30
LOG2E = 1.4426950408889634
QUERY_SCALE = HEAD_DIM ** -0.5 * LOG2E


def _params(*sem):
    return pltpu.CompilerParams(dimension_semantics=sem, vmem_limit_bytes=VMEM_LIMIT_BYTES)


def _rms_rows(x, gain):
    ms = jnp.mean(x * x, axis=-1, keepdims=True)
    return x * lax.rsqrt(ms + EPS) * gain


EDGE_ROWS = 256


def _row_slabs(n_rows):
    return [slice(r, r + EDGE_ROWS) for r in range(0, n_rows, EDGE_ROWS)]


def _inproj_kernel(x_ref, gain_ref, w_ref, cs_ref, o_ref, h_ref):
    j = pl.program_id(1)
    col_scale = cs_ref[...]

    def project(h):
        return (jnp.dot(h, w_ref[...], preferred_element_type=F32) * col_scale).astype(o_ref.dtype)

    @pl.when(j == 0)
    def _():
        gain = gain_ref[...]
        for rows in _row_slabs(x_ref.shape[0]):
            h = _rms_rows(x_ref[rows, :], gain).astype(h_ref.dtype)
            h_ref[rows, :] = h
            o_ref[rows, :] = project(h)

    @pl.when(j > 0)
    def _():
        o_ref[...] = project(h_ref[...])


def _inproj(x2d, gain, w_bf16, col_scale, tm, tn):
    t, d = x2d.shape
    n = w_bf16.shape[1]
    return pl.pallas_call(
        _inproj_kernel,
        grid=(t // tm, n // tn),
        in_specs=[
            pl.BlockSpec((tm, d), lambda i, j: (i, 0)),
            pl.BlockSpec((1, d), lambda i, j: (0, 0)),
            pl.BlockSpec((d, tn), lambda i, j: (0, j)),
            pl.BlockSpec((1, tn), lambda i, j: (0, j)),
        ],
        out_specs=pl.BlockSpec((tm, tn), lambda i, j: (i, j)),
        out_shape=jax.ShapeDtypeStruct((t, n), BF16),
        scratch_shapes=[pltpu.VMEM((tm, d), BF16)],
        compiler_params=_params("arbitrary", "arbitrary"),
        name="inproj",
    )(x2d, gain, w_bf16, col_scale)


QB = 256
N_SEG = 1 + (N_LEFT * CHUNK) // QB
WIN = N_SEG * QB
ROLL_W = 1024


def _bias_rows(rel_bias):
    h = rel_bias.shape[0]
    far_past = rel_bias[:, 2 * MAX_REL:2 * MAX_REL + 1]
    far_future = rel_bias[:, 0:1]
    n_head = (WIN - QB) - MAX_REL
    n_tail = WIN - n_head - (2 * MAX_REL + 1)
    return jnp.concatenate([
        jnp.broadcast_to(far_past, (h, n_head)),
        rel_bias[:, ::-1],
        jnp.broadcast_to(far_future, (h, n_tail)),
        jnp.broadcast_to(far_past, (h, ROLL_W - WIN)),
    ], axis=1).astype(F32)


def _attn_kernel(q_ref, k0_ref, k1_ref, k2_ref, v0_ref, v1_ref, v2_ref, brow_ref, gain_ref,
                 o_ref, bias_ref, acc_ref, s_ref):
    qb = pl.program_id(1)

    @pl.when((pl.program_id(0) == 0) & (qb == 0))
    def _():
        qc = lax.broadcasted_iota(jnp.int32, (QB, WIN), 0) // CHUNK
        kc = lax.broadcasted_iota(jnp.int32, (QB, WIN), 1) // CHUNK
        band = (kc >= qc) & (kc <= qc + N_LEFT)
        for h in range(HEADS):
            row = jnp.broadcast_to(brow_ref[h:h + 1, :] * LOG2E, (QB, ROLL_W))
            toeplitz = pltpu.roll(row, 0, 1, stride=1, stride_axis=0)
            bias_ref[h] = jnp.where(band, toeplitz[:, :WIN], NEG)

    k_refs = (k0_ref, k1_ref, k2_ref)
    v_refs = (v0_ref, v1_ref, v2_ref)
    def lane_halves(op, x):
        while x.shape[1] > HEAD_DIM:
            half = x.shape[1] // 2
            x = op(x[:, :half], x[:, half:])
        return x

    def attend(segs):
        m = []
        for h in range(HEADS):
            sl = slice(h * HEAD_DIM, (h + 1) * HEAD_DIM)
            qh = q_ref[:, sl]
            mx = None
            for j in segs:
                cols = slice(j * QB, (j + 1) * QB)
                sj = lax.dot_general(qh, k_refs[j][:, sl], (((1,), (1,)), ((), ())),
                                     preferred_element_type=F32) + bias_ref[h, :, cols]
                s_ref[h, :, cols] = sj
                mx = sj if mx is None else jnp.maximum(mx, sj)
            m.append(lane_halves(jnp.maximum, mx).max(axis=-1, keepdims=True))
        sq = jnp.zeros((QB, HEAD_DIM), F32)
        ones = jnp.ones((QB, HEAD_DIM), BF16)
        for h in range(HEADS):
            sl = slice(h * HEAD_DIM, (h + 1) * HEAD_DIM)
            p = jnp.concatenate(
                [jnp.exp2(s_ref[h, :, j * QB:(j + 1) * QB] - m[h]).astype(BF16) for j in segs], axis=1)
            v1 = jnp.concatenate(
                [jnp.concatenate([v_refs[j][:, sl], ones], axis=1) for j in segs], axis=0)
            pv = jnp.dot(p, v1, preferred_element_type=F32)
            o = pv[:, :HEAD_DIM] * (1.0 / pv[:, HEAD_DIM:])
            acc_ref[:, sl] = o
            sq = sq + o * o
        inv = lax.rsqrt(sq.sum(axis=-1, keepdims=True) / GROUP + EPS)
        o_ref[...] = (acc_ref[...] * inv * gain_ref[...]).astype(o_ref.dtype)

    for n_valid in range(1, N_SEG + 1):
        @pl.when(jnp.minimum(qb, N_SEG - 1) == n_valid - 1)
        def _():
            attend(tuple(range(N_SEG - n_valid, N_SEG)))


def _attention(proj, brow, gain, batch, seq):
    t = proj.shape[0]
    nqb = seq // QB

    def kv_spec(col_block, seg):
        def index(b, i):
            return (b * nqb + jnp.maximum(i - (N_SEG - 1 - seg), 0), col_block)
        return pl.BlockSpec((QB, GROUP), index)

    return pl.pallas_call(
        _attn_kernel,
        grid=(batch, nqb),
        in_specs=[pl.BlockSpec((QB, GROUP), lambda b, i: (b * nqb + i, 0))]
        + [kv_spec(1, s) for s in range(N_SEG)]
        + [kv_spec(2, s) for s in range(N_SEG)]
        + [pl.BlockSpec((HEADS, ROLL_W), lambda b, i: (0, 0)),
           pl.BlockSpec((1, GROUP), lambda b, i: (0, 0))],
        out_specs=pl.BlockSpec((QB, GROUP), lambda b, i: (b * nqb + i, 0)),
        out_shape=jax.ShapeDtypeStruct((t, GROUP), BF16),
        scratch_shapes=[pltpu.VMEM((HEADS, QB, WIN), F32),
                        pltpu.VMEM((QB, GROUP), F32),
                        pltpu.VMEM((HEADS, QB, WIN), F32)],
        compiler_params=_params("arbitrary", "arbitrary"),
        name="attention",
    )(proj, proj, proj, proj, proj, proj, proj, brow, gain)


TB = 512
HGRN_UNROLL = 4


def _silu(x):
    hx = 0.5 * x
    return hx + hx * jnp.tanh(hx)


def _split3(x):
    hi = x.astype(BF16)
    r1 = x - hi.astype(F32)
    mid = r1.astype(BF16)
    lo = (r1 - mid.astype(F32)).astype(BF16)
    return hi, mid, lo


def _hgrn_kernel(n_cast, q_ref, f_ref, i_ref, g_ref, lb_ref, gn_ref, *refs):
    cast_in, (o_ref, *cast_out) = refs[:n_cast], refs[n_cast:2 * n_cast + 1]
    st_ref, qd_ref, a_ref, u_ref, dec_ref = refs[2 * n_cast + 1:]

    for src, dst in zip(cast_in, cast_out):
        if len(dst.shape) == 2:
            dst[...] = src[...].astype(dst.dtype)
        else:
            width = dst.shape[2]
            for n in range(dst.shape[0]):
                dst[n] = src[:, n * width:(n + 1) * width].astype(dst.dtype)

    @pl.when(pl.program_id(1) == 0)
    def _():
        st_ref[...] = jnp.zeros_like(st_ref)

    lb = lb_ref[...]
    f_mid = 0.5 * (1.0 + lb)
    f_amp = 0.5 * (1.0 - lb)
    gn = gn_ref[...]
    r_i = lax.broadcasted_iota(jnp.int32, (CHUNK, CHUNK), 0)
    c_i = lax.broadcasted_iota(jnp.int32, (CHUNK, CHUNK), 1)
    causal = r_i >= c_i
    tril = causal.astype(BF16)
    tril3 = jnp.concatenate([tril, tril, tril], axis=1)

    def chunk_rows(c):
        return pl.ds(pl.multiple_of(c * CHUNK, CHUNK), CHUNK)

    def local_terms(c):
        rows = chunk_rows(c)
        q = _silu(q_ref[rows, :].astype(F32))
        f = f_mid + f_amp * jnp.tanh(0.5 * f_ref[rows, :].astype(F32))
        k = 1.0 - f
        b = jnp.dot(tril3, jnp.concatenate(_split3(jnp.log2(f)), axis=0),
                    preferred_element_type=F32)
        decay = jnp.exp2(b[CHUNK - 1:CHUNK, :])
        q_dec = (q * jnp.exp2(b)).astype(BF16)
        k_undecayed = k * jnp.exp2(-b)
        k_intra = k_undecayed.astype(BF16)
        k_state = (k_undecayed * decay).astype(BF16)
        qd_ref[rows, :] = q_dec
        dec_ref[c] = decay
        for h in range(HEADS):
            sl = slice(h * HEAD_DIM, (h + 1) * HEAD_DIM)
            a = lax.dot_general(q_dec[:, sl], k_intra[:, sl], (((1,), (1,)), ((), ())),
                                preferred_element_type=F32)
            a_ref[h, rows, :] = jnp.where(causal, a, 0.0).astype(BF16)
            u_ref[c, h] = lax.dot_general(i_ref[rows, sl], k_state[:, sl], (((0,), (0,)), ((), ())),
                                          preferred_element_type=F32)

    def outputs(c):
        rows = chunk_rows(c)
        decay = dec_ref[c]
        for h in range(HEADS):
            sl = slice(h * HEAD_DIM, (h + 1) * HEAD_DIM)
            st = st_ref[h]
            o = jnp.dot(a_ref[h, rows, :], i_ref[rows, sl], preferred_element_type=F32)
            o = o + lax.dot_general(qd_ref[rows, sl], st.astype(BF16), (((1,), (1,)), ((), ())),
                                    preferred_element_type=F32)
            st_ref[h] = st * decay[:, sl] + u_ref[c, h]
            o = _rms_rows(o, gn)
            o_ref[rows, sl] = (o * _silu(g_ref[rows, sl].astype(F32))).astype(o_ref.dtype)

    def unrolled(fn):
        def body(i, carry):
            for u in range(HGRN_UNROLL):
                fn(i * HGRN_UNROLL + u)
            return carry
        return body

    n_iter = TB // (CHUNK * HGRN_UNROLL)
    lax.fori_loop(0, n_iter, unrolled(local_terms), 0)
    lax.fori_loop(0, n_iter, unrolled(outputs), 0)


def _hgrn(proj, lb, gn, batch, seq, weights):
    t = proj.shape[0]
    nb = seq // TB
    steps = batch * nb
    for w, _ in weights:
        assert w.shape[0] % (steps * BF16_SUBLANES) == 0, w.shape

    def col(cb):
        return pl.BlockSpec((TB, GROUP), lambda b, i: (b * nb + i, cb))

    def slab(w):
        return pl.BlockSpec((w.shape[0] // steps, w.shape[1]), lambda b, i: (b * nb + i, 0))

    def cast_shape(w, tile):
        return w.shape if tile is None else (w.shape[1] // tile, w.shape[0], tile)

    def cast_slab(w, tile):
        if tile is None:
            return slab(w)
        return pl.BlockSpec((w.shape[1] // tile, w.shape[0] // steps, tile),
                            lambda b, i: (0, b * nb + i, 0))

    return pl.pallas_call(
        functools.partial(_hgrn_kernel, len(weights)),
        grid=(batch, nb),
        in_specs=[col(3), col(4), col(5), col(6),
                  pl.BlockSpec((1, GROUP), lambda b, i: (0, 0)),
                  pl.BlockSpec((1, HEAD_DIM), lambda b, i: (0, 0))]
        + [slab(w) for w, _ in weights],
        out_specs=[pl.BlockSpec((TB, GROUP), lambda b, i: (b * nb + i, 0))]
        + [cast_slab(w, tile) for w, tile in weights],
        out_shape=[jax.ShapeDtypeStruct((t, GROUP), BF16)]
        + [jax.ShapeDtypeStruct(cast_shape(w, tile), BF16) for w, tile in weights],
        scratch_shapes=[
            pltpu.VMEM((HEADS, HEAD_DIM, HEAD_DIM), F32),
            pltpu.VMEM((TB, GROUP), BF16),
            pltpu.VMEM((HEADS, TB, CHUNK), BF16),
            pltpu.VMEM((TB // CHUNK, HEADS, HEAD_DIM, HEAD_DIM), F32),
            pltpu.VMEM((TB // CHUNK, 1, GROUP), F32),
        ],
        compiler_params=_params("arbitrary", "arbitrary"),
        name="hgrn2",
    )(proj, proj, proj, proj, lb, gn, *[w for w, _ in weights])


OUT_COLS = 512


def _outproj_kernel(x_ref, a_ref, r_ref, w_ref, o_ref):
    mixed = jnp.concatenate([a_ref[...], r_ref[...]], axis=1)
    for n in range(0, o_ref.shape[1], OUT_COLS):
        cols = slice(n, n + OUT_COLS)
        o_ref[:, cols] = x_ref[:, cols] + jnp.dot(mixed, w_ref[:, cols],
                                                  preferred_element_type=F32)


def _outproj(x2d, attn, rec, w_bf16, tm):
    t, d = x2d.shape
    g = attn.shape[1]
    return pl.pallas_call(
        _outproj_kernel,
        grid=(t // tm,),
        in_specs=[
            pl.BlockSpec((tm, d), lambda i: (i, 0)),
            pl.BlockSpec((tm, g), lambda i: (i, 0)),
            pl.BlockSpec((tm, g), lambda i: (i, 0)),
            pl.BlockSpec((2 * g, d), lambda i: (0, 0)),
        ],
        out_specs=pl.BlockSpec((tm, d), lambda i: (i, 0)),
        out_shape=jax.ShapeDtypeStruct((t, d), F32),
        compiler_params=_params("arbitrary"),
        name="outproj",
    )(x2d, attn, rec, w_bf16)


FFN_DOWN_COLS = 512
FFN_HIDDEN_TILE = 512


def _ffn_kernel(x_hbm, g2_ref, wg_hbm, wu_hbm, wd_hbm, gf_ref, out_hbm,
                acc_ref, h_ref, wg_buf, wu_buf, wd_buf, x_sem, o_sem, w_sem):
    _, tm, d = acc_ref.shape
    n_tiles, _, tf = wg_hbm.shape
    n_row_tiles = x_hbm.shape[0] // tm
    n_total = n_row_tiles * n_tiles
    col_chunks = [slice(n, n + FFN_DOWN_COLS) for n in range(0, d, FFN_DOWN_COLS)]

    def x_copy(i, slot):
        return pltpu.make_async_copy(x_hbm.at[pl.ds(i * tm, tm), :], acc_ref.at[slot],
                                     x_sem.at[slot])

    def out_copy(i, slot):
        return pltpu.make_async_copy(acc_ref.at[slot], out_hbm.at[pl.ds(i * tm, tm), :],
                                     o_sem.at[slot])

    def w_copies(g):
        j = lax.rem(g, n_tiles)
        slot = lax.rem(g, 2)
        return (pltpu.make_async_copy(wg_hbm.at[j], wg_buf.at[slot], w_sem.at[0, slot]),
                pltpu.make_async_copy(wu_hbm.at[j], wu_buf.at[slot], w_sem.at[1, slot]),
                pltpu.make_async_copy(wd_hbm.at[pl.ds(j * tf, tf), :], wd_buf.at[slot],
                                      w_sem.at[2, slot]))

    def weights(g):
        for c in w_copies(g):
            c.wait()

        @pl.when(g + 1 < n_total)
        def _():
            for c in w_copies(g + 1):
                c.start()

        slot = lax.rem(g, 2)
        return wg_buf.at[slot], wu_buf.at[slot], wd_buf.at[slot]

    def swiglu_down(h, w):
        wg, wu, wd = w
        gate = jnp.dot(h, wg[...], preferred_element_type=F32)
        up = jnp.dot(h, wu[...], preferred_element_type=F32)
        act = (gate * jax.nn.sigmoid(gate) * up).astype(BF16)
        return [jnp.dot(act, wd[:, cols], preferred_element_type=F32) for cols in col_chunks]

    def row_tile(i, carry):
        slot = lax.rem(i, 2)
        acc = acc_ref.at[slot]
        g0 = i * n_tiles
        x_copy(i, slot).wait()

        w = weights(g0)
        g2 = g2_ref[...]
        for rows in _row_slabs(tm):
            h = _rms_rows(acc[rows, :], g2).astype(h_ref.dtype)
            h_ref[rows, :] = h
            for cols, y in zip(col_chunks, swiglu_down(h, w)):
                acc[rows, cols] += y

        @pl.when(i >= 1)
        def _():
            out_copy(i - 1, 1 - slot).wait()

        @pl.when(i + 1 < n_row_tiles)
        def _():
            x_copy(i + 1, 1 - slot).start()

        def hidden_tile(j, c):
            w = weights(g0 + j)
            for cols, y in zip(col_chunks, swiglu_down(h_ref[...], w)):
                acc[:, cols] += y
            return c

        lax.fori_loop(1, n_tiles - 1, hidden_tile, 0)

        w = weights(g0 + n_tiles - 1)
        gf = gf_ref[...]
        for rows in _row_slabs(tm):
            ss = jnp.zeros((EDGE_ROWS, 1), F32)
            for cols, y in zip(col_chunks, swiglu_down(h_ref[rows, :], w)):
                o = acc[rows, cols] + y
                acc[rows, cols] = o
                ss = ss + (o * o).sum(axis=-1, keepdims=True)
            acc[rows, :] = acc[rows, :] * lax.rsqrt(ss / d + EPS) * gf
        out_copy(i, slot).start()
        return carry

    x_copy(0, 0).start()
    for c in w_copies(0):
        c.start()
    lax.fori_loop(0, n_row_tiles, row_tile, 0)
    out_copy(n_row_tiles - 1, (n_row_tiles - 1) % 2).wait()


def _ffn(x1, g2, wg_tiles, wu_tiles, wd, gf, tm):
    t, d = x1.shape
    n_tiles, _, tf = wg_tiles.shape
    assert n_tiles >= 2, "the first and the last hidden tile are separate code paths"
    assert t % tm == 0 and tm % EDGE_ROWS == 0
    hbm = pl.BlockSpec(memory_space=pl.ANY)
    whole = pl.BlockSpec(memory_space=pltpu.VMEM)
    return pl.pallas_call(
        _ffn_kernel,
        in_specs=[hbm, whole, hbm, hbm, hbm, whole],
        out_specs=hbm,
        out_shape=jax.ShapeDtypeStruct((t, d), F32),
        scratch_shapes=[
            pltpu.VMEM((2, tm, d), F32),
            pltpu.VMEM((tm, d), BF16),
            pltpu.VMEM((2, d, tf), BF16),
            pltpu.VMEM((2, d, tf), BF16),
            pltpu.VMEM((2, tf, d), BF16),
            pltpu.SemaphoreType.DMA((2,)),
            pltpu.SemaphoreType.DMA((2,)),
            pltpu.SemaphoreType.DMA((3, 2)),
        ],
        compiler_params=pltpu.CompilerParams(vmem_limit_bytes=VMEM_LIMIT_BYTES),
        name="ffn",
    )(x1, g2, wg_tiles, wu_tiles, wd, gf)


def kernel(x, norm1_gain, w_in, rel_bias, lower_bounds, grn_norm_gain, attn_out_gain, w_out,
           norm2_gain, w_gate, w_up, w_down, final_gain):
    batch, seq, d = x.shape
    depth = w_in.shape[0]
    assert seq % TB == 0 and seq % QB == 0 and w_in.shape[2] == 7 * GROUP and d == 2 * GROUP
    lb_all = jnp.cumsum(jax.nn.softmax(lower_bounds.astype(F32), axis=0), axis=0)
    xf = x.reshape(batch * seq, d)
    col_scale = jnp.concatenate([jnp.full((1, GROUP), QUERY_SCALE, F32),
                                 jnp.ones((1, w_in.shape[2] - GROUP), F32)], axis=1)
    for l in range(depth):
        proj = _inproj(xf, norm1_gain[l][None, :], w_in[l].astype(BF16), col_scale,
                       tm=1024, tn=1792)
        attn = _attention(proj, _bias_rows(rel_bias[l]), attn_out_gain[l][None, :], batch, seq)
        rec, wo, wg, wu, wd = _hgrn(proj, lb_all[l][None, :], grn_norm_gain[l][None, :], batch, seq,
                                    [(w_out[l], None), (w_gate[l], FFN_HIDDEN_TILE),
                                     (w_up[l], FFN_HIDDEN_TILE), (w_down[l], None)])
        x1 = _outproj(xf, attn, rec, wo, tm=512)
        last = l == depth - 1
        assert last, "the final rmsnorm is fused into the last layer's FFN"
        xf = _ffn(x1, norm2_gain[l][None, :], wg, wu, wd, final_gain[None, :], tm=1024)
    return xf.reshape(batch, seq, d)
```

```python
import functools

import jax
import jax.numpy as jnp
from jax import lax
from jax.experimental import pallas as pl
from jax.experimental.pallas import tpu as pltpu

F32 = jnp.float32
BF16 = jnp.bfloat16

EPS = 1e-6
CHUNK = 64
N_LEFT = 8
MAX_REL = 128
HEADS = 8
HEAD_DIM = 128
GROUP = HEADS * HEAD_DIM

VMEM_LIMIT_BYTES = 56 * 1024 * 1024
BF16_SUBLANES = 16
NEG = -1e30
LOG2E = 1.4426950408889634
QUERY_SCALE = HEAD_DIM ** -0.5 * LOG2E


def _params(*sem):
    return pltpu.CompilerParams(dimension_semantics=sem, vmem_limit_bytes=VMEM_LIMIT_BYTES)


def _rms_rows(x, gain):
    ms = jnp.mean(x * x, axis=-1, keepdims=True)
    return x * lax.rsqrt(ms + EPS) * gain


EDGE_ROWS = 256


def _row_slabs(n_rows):
    return [slice(r, r + EDGE_ROWS) for r in range(0, n_rows, EDGE_ROWS)]


def _inproj_kernel(x_hbm, gain_ref, w_hbm, cs_ref, out_hbm,
                   x_buf, h_ref, w_buf, o_buf, x_sem, w_sem, o_sem):
    _, tm, d = x_buf.shape
    tn = w_buf.shape[2]
    n_col_tiles = w_hbm.shape[1] // tn
    n_row_tiles = x_hbm.shape[0] // tm
    n_total = n_row_tiles * n_col_tiles

    def x_copy(i, slot):
        return pltpu.make_async_copy(x_hbm.at[pl.ds(i * tm, tm), :], x_buf.at[slot], x_sem.at[slot])

    def w_copy(j, slot):
        return pltpu.make_async_copy(w_hbm.at[:, pl.ds(j * tn, tn)], w_buf.at[slot], w_sem.at[slot])

    def out_copy(i, j, slot):
        return pltpu.make_async_copy(o_buf.at[slot],
                                     out_hbm.at[pl.ds(i * tm, tm), pl.ds(j * tn, tn)],
                                     o_sem.at[slot])

    def row_tile(i, carry):
        x_slot = lax.rem(i, 2)
        x_copy(i, x_slot).wait()

        @pl.when(i + 1 < n_row_tiles)
        def _():
            x_copy(i + 1, 1 - x_slot).start()

        for j in range(n_col_tiles):
            g = i * n_col_tiles + j
            slot = lax.rem(g, 2)
            w_copy(j, slot).wait()

            @pl.when(g + 1 < n_total)
            def _():
                w_copy((j + 1) % n_col_tiles, 1 - slot).start()

            @pl.when(g >= 2)
            def _():
                out_copy(i - (1 if j < 2 else 0), (j - 2) % n_col_tiles, slot).wait()

            w = w_buf.at[slot]
            o = o_buf.at[slot]
            col_scale = cs_ref[:, j * tn:(j + 1) * tn]

            def project(h):
                y = jnp.dot(h, w[...], preferred_element_type=F32)
                return (y * col_scale).astype(o.dtype)

            if j == 0:
                gain = gain_ref[...]
                x = x_buf.at[x_slot]
                for rows in _row_slabs(tm):
                    h = _rms_rows(x[rows, :], gain).astype(h_ref.dtype)
                    h_ref[rows, :] = h
                    o[rows, :] = project(h)
            else:
                o[...] = project(h_ref[...])
            out_copy(i, j, slot).start()
        return carry

    x_copy(0, 0).start()
    w_copy(0, 0).start()
    lax.fori_loop(0, n_row_tiles, row_tile, 0)
    for g in (n_total - 2, n_total - 1):
        out_copy(g // n_col_tiles, g % n_col_tiles, g % 2).wait()


def _inproj(x2d, gain, w_bf16, col_scale, tm, tn):
    t, d = x2d.shape
    n = w_bf16.shape[1]
    assert t % tm == 0 and tm % EDGE_ROWS == 0 and n % tn == 0 and n // tn >= 2
    hbm = pl.BlockSpec(memory_space=pl.ANY)
    whole = pl.BlockSpec(memory_space=pltpu.VMEM)
    return pl.pallas_call(
        _inproj_kernel,
        in_specs=[hbm, whole, hbm, whole],
        out_specs=hbm,
        out_shape=jax.ShapeDtypeStruct((t, n), BF16),
        scratch_shapes=[
            pltpu.VMEM((2, tm, d), F32),
            pltpu.VMEM((tm, d), BF16),
            pltpu.VMEM((2, d, tn), BF16),
            pltpu.VMEM((2, tm, tn), BF16),
            pltpu.SemaphoreType.DMA((2,)),
            pltpu.SemaphoreType.DMA((2,)),
            pltpu.SemaphoreType.DMA((2,)),
        ],
        compiler_params=pltpu.CompilerParams(vmem_limit_bytes=VMEM_LIMIT_BYTES),
        name="inproj",
    )(x2d, gain, w_bf16, col_scale)


QB = 256
N_SEG = 1 + (N_LEFT * CHUNK) // QB
WIN = N_SEG * QB
ROLL_W = 1024


def _bias_rows(rel_bias):
    h = rel_bias.shape[0]
    far_past = rel_bias[:, 2 * MAX_REL:2 * MAX_REL + 1]
    far_future = rel_bias[:, 0:1]
    n_head = (WIN - QB) - MAX_REL
    n_tail = WIN - n_head - (2 * MAX_REL + 1)
    return jnp.concatenate([
        jnp.broadcast_to(far_past, (h, n_head)),
        rel_bias[:, ::-1],
        jnp.broadcast_to(far_future, (h, n_tail)),
        jnp.broadcast_to(far_past, (h, ROLL_W - WIN)),
    ], axis=1).astype(F32)


def _attn_kernel(q_ref, k0_ref, k1_ref, k2_ref, v0_ref, v1_ref, v2_ref, brow_ref, gain_ref,
                 o_ref, bias_ref, acc_ref, s_ref):
    qb = pl.program_id(1)

    @pl.when((pl.program_id(0) == 0) & (qb == 0))
    def _():
        qc = lax.broadcasted_iota(jnp.int32, (QB, WIN), 0) // CHUNK
        kc = lax.broadcasted_iota(jnp.int32, (QB, WIN), 1) // CHUNK
        band = (kc >= qc) & (kc <= qc + N_LEFT)
        for h in range(HEADS):
            row = jnp.broadcast_to(brow_ref[h:h + 1, :] * LOG2E, (QB, ROLL_W))
            toeplitz = pltpu.roll(row, 0, 1, stride=1, stride_axis=0)
            bias_ref[h] = jnp.where(band, toeplitz[:, :WIN], NEG)

    k_refs = (k0_ref, k1_ref, k2_ref)
    v_refs = (v0_ref, v1_ref, v2_ref)
    def lane_halves(op, x):
        while x.shape[1] > HEAD_DIM:
            half = x.shape[1] // 2
            x = op(x[:, :half], x[:, half:])
        return x

    def attend(segs):
        m = []
        for h in range(HEADS):
            sl = slice(h * HEAD_DIM, (h + 1) * HEAD_DIM)
            qh = q_ref[:, sl]
            mx = None
            for j in segs:
                cols = slice(j * QB, (j + 1) * QB)
                sj = lax.dot_general(qh, k_refs[j][:, sl], (((1,), (1,)), ((), ())),
                                     preferred_element_type=F32) + bias_ref[h, :, cols]
                s_ref[h, :, cols] = sj
                mx = sj if mx is None else jnp.maximum(mx, sj)
            m.append(lane_halves(jnp.maximum, mx).max(axis=-1, keepdims=True))
        sq = jnp.zeros((QB, HEAD_DIM), F32)
        ones = jnp.ones((QB, HEAD_DIM), BF16)
        for h in range(HEADS):
            sl = slice(h * HEAD_DIM, (h + 1) * HEAD_DIM)
            p = jnp.concatenate(
                [jnp.exp2(s_ref[h, :, j * QB:(j + 1) * QB] - m[h]).astype(BF16) for j in segs], axis=1)
            v1 = jnp.concatenate(
                [jnp.concatenate([v_refs[j][:, sl], ones], axis=1) for j in segs], axis=0)
            pv = jnp.dot(p, v1, preferred_element_type=F32)
            o = pv[:, :HEAD_DIM] * (1.0 / pv[:, HEAD_DIM:])
            acc_ref[:, sl] = o
            sq = sq + o * o
        inv = lax.rsqrt(sq.sum(axis=-1, keepdims=True) / GROUP + EPS)
        o_ref[...] = (acc_ref[...] * inv * gain_ref[...]).astype(o_ref.dtype)

    for n_valid in range(1, N_SEG + 1):
        @pl.when(jnp.minimum(qb, N_SEG - 1) == n_valid - 1)
        def _():
            attend(tuple(range(N_SEG - n_valid, N_SEG)))


def _attention(proj, brow, gain, batch, seq):
    t = proj.shape[0]
    nqb = seq // QB

    def kv_spec(col_block, seg):
        def index(b, i):
            return (b * nqb + jnp.maximum(i - (N_SEG - 1 - seg), 0), col_block)
        return pl.BlockSpec((QB, GROUP), index)

    return pl.pallas_call(
        _attn_kernel,
        grid=(batch, nqb),
        in_specs=[pl.BlockSpec((QB, GROUP), lambda b, i: (b * nqb + i, 0))]
        + [kv_spec(1, s) for s in range(N_SEG)]
        + [kv_spec(2, s) for s in range(N_SEG)]
        + [pl.BlockSpec((HEADS, ROLL_W), lambda b, i: (0, 0)),
           pl.BlockSpec((1, GROUP), lambda b, i: (0, 0))],
        out_specs=pl.BlockSpec((QB, GROUP), lambda b, i: (b * nqb + i, 0)),
        out_shape=jax.ShapeDtypeStruct((t, GROUP), BF16),
        scratch_shapes=[pltpu.VMEM((HEADS, QB, WIN), F32),
                        pltpu.VMEM((QB, GROUP), F32),
                        pltpu.VMEM((HEADS, QB, WIN), F32)],
        compiler_params=_params("arbitrary", "arbitrary"),
        name="attention",
    )(proj, proj, proj, proj, proj, proj, proj, brow, gain)


TB = 512
HGRN_UNROLL = 4


def _silu(x):
    hx = 0.5 * x
    return hx + hx * jnp.tanh(hx)


def _split3(x):
    hi = x.astype(BF16)
    r1 = x - hi.astype(F32)
    mid = r1.astype(BF16)
    lo = (r1 - mid.astype(F32)).astype(BF16)
    return hi, mid, lo


def _hgrn_kernel(n_cast, q_ref, f_ref, i_ref, g_ref, lb_ref, gn_ref, *refs):
    cast_in, (o_ref, *cast_out) = refs[:n_cast], refs[n_cast:2 * n_cast + 1]
    st_ref, qd_ref, a_ref, u_ref, dec_ref = refs[2 * n_cast + 1:]

    for src, dst in zip(cast_in, cast_out):
        if len(dst.shape) == 2:
            dst[...] = src[...].astype(dst.dtype)
        else:
            width = dst.shape[2]
            for n in range(dst.shape[0]):
                dst[n] = src[:, n * width:(n + 1) * width].astype(dst.dtype)

    @pl.when(pl.program_id(1) == 0)
    def _():
        st_ref[...] = jnp.zeros_like(st_ref)

    lb = lb_ref[...]
    f_mid = 0.5 * (1.0 + lb)
    f_amp = 0.5 * (1.0 - lb)
    gn = gn_ref[...]
    r_i = lax.broadcasted_iota(jnp.int32, (CHUNK, CHUNK), 0)
    c_i = lax.broadcasted_iota(jnp.int32, (CHUNK, CHUNK), 1)
    causal = r_i >= c_i
    tril = causal.astype(BF16)
    tril3 = jnp.concatenate([tril, tril, tril], axis=1)

    def chunk_rows(c):
        return pl.ds(pl.multiple_of(c * CHUNK, CHUNK), CHUNK)

    def local_terms(c):
        rows = chunk_rows(c)
        q = _silu(q_ref[rows, :].astype(F32))
        f = f_mid + f_amp * jnp.tanh(0.5 * f_ref[rows, :].astype(F32))
        k = 1.0 - f
        b = jnp.dot(tril3, jnp.concatenate(_split3(jnp.log2(f)), axis=0),
                    preferred_element_type=F32)
        decay = jnp.exp2(b[CHUNK - 1:CHUNK, :])
        q_dec = (q * jnp.exp2(b)).astype(BF16)
        k_undecayed = k * jnp.exp2(-b)
        k_intra = k_undecayed.astype(BF16)
        k_state = (k_undecayed * decay).astype(BF16)
        qd_ref[rows, :] = q_dec
        dec_ref[c] = decay
        for h in range(HEADS):
            sl = slice(h * HEAD_DIM, (h + 1) * HEAD_DIM)
            a = lax.dot_general(q_dec[:, sl], k_intra[:, sl], (((1,), (1,)), ((), ())),
                                preferred_element_type=F32)
            a_ref[h, rows, :] = jnp.where(causal, a, 0.0).astype(BF16)
            u_ref[c, h] = lax.dot_general(i_ref[rows, sl], k_state[:, sl], (((0,), (0,)), ((), ())),
                                          preferred_element_type=F32)

    def outputs(c):
        rows = chunk_rows(c)
        decay = dec_ref[c]
        for h in range(HEADS):
            sl = slice(h * HEAD_DIM, (h + 1) * HEAD_DIM)
            st = st_ref[h]
            o = jnp.dot(a_ref[h, rows, :], i_ref[rows, sl], preferred_element_type=F32)
            o = o + lax.dot_general(qd_ref[rows, sl], st.astype(BF16), (((1,), (1,)), ((), ())),
                                    preferred_element_type=F32)
            st_ref[h] = st * decay[:, sl] + u_ref[c, h]
            o = _rms_rows(o, gn)
            o_ref[rows, sl] = (o * _silu(g_ref[rows, sl].astype(F32))).astype(o_ref.dtype)

    def unrolled(fn):
        def body(i, carry):
            for u in range(HGRN_UNROLL):
                fn(i * HGRN_UNROLL + u)
            return carry
        return body

    n_iter = TB // (CHUNK * HGRN_UNROLL)
    lax.fori_loop(0, n_iter, unrolled(local_terms), 0)
    lax.fori_loop(0, n_iter, unrolled(outputs), 0)


def _hgrn(proj, lb, gn, batch, seq, weights):
    t = proj.shape[0]
    nb = seq // TB
    steps = batch * nb
    for w, _ in weights:
        assert w.shape[0] % (steps * BF16_SUBLANES) == 0, w.shape

    def col(cb):
        return pl.BlockSpec((TB, GROUP), lambda b, i: (b * nb + i, cb))

    def slab(w):
        return pl.BlockSpec((w.shape[0] // steps, w.shape[1]), lambda b, i: (b * nb + i, 0))

    def cast_shape(w, tile):
        return w.shape if tile is None else (w.shape[1] // tile, w.shape[0], tile)

    def cast_slab(w, tile):
        if tile is None:
            return slab(w)
        return pl.BlockSpec((w.shape[1] // tile, w.shape[0] // steps, tile),
                            lambda b, i: (0, b * nb + i, 0))

    return pl.pallas_call(
        functools.partial(_hgrn_kernel, len(weights)),
        grid=(batch, nb),
        in_specs=[col(3), col(4), col(5), col(6),
                  pl.BlockSpec((1, GROUP), lambda b, i: (0, 0)),
                  pl.BlockSpec((1, HEAD_DIM), lambda b, i: (0, 0))]
        + [slab(w) for w, _ in weights],
        out_specs=[pl.BlockSpec((TB, GROUP), lambda b, i: (b * nb + i, 0))]
        + [cast_slab(w, tile) for w, tile in weights],
        out_shape=[jax.ShapeDtypeStruct((t, GROUP), BF16)]
        + [jax.ShapeDtypeStruct(cast_shape(w, tile), BF16) for w, tile in weights],
        scratch_shapes=[
            pltpu.VMEM((HEADS, HEAD_DIM, HEAD_DIM), F32),
            pltpu.VMEM((TB, GROUP), BF16),
            pltpu.VMEM((HEADS, TB, CHUNK), BF16),
            pltpu.VMEM((TB // CHUNK, HEADS, HEAD_DIM, HEAD_DIM), F32),
            pltpu.VMEM((TB // CHUNK, 1, GROUP), F32),
        ],
        compiler_params=_params("arbitrary", "arbitrary"),
        name="hgrn2",
    )(proj, proj, proj, proj, lb, gn, *[w for w, _ in weights])


OUT_COLS = 512


def _outproj_kernel(x_ref, a_ref, r_ref, w_ref, o_ref):
    mixed = jnp.concatenate([a_ref[...], r_ref[...]], axis=1)
    for n in range(0, o_ref.shape[1], OUT_COLS):
        cols = slice(n, n + OUT_COLS)
        o_ref[:, cols] = x_ref[:, cols] + jnp.dot(mixed, w_ref[:, cols],
                                                  preferred_element_type=F32)


def _outproj(x2d, attn, rec, w_bf16, tm):
    t, d = x2d.shape
    g = attn.shape[1]
    return pl.pallas_call(
        _outproj_kernel,
        grid=(t // tm,),
        in_specs=[
            pl.BlockSpec((tm, d), lambda i: (i, 0)),
            pl.BlockSpec((tm, g), lambda i: (i, 0)),
            pl.BlockSpec((tm, g), lambda i: (i, 0)),
            pl.BlockSpec((2 * g, d), lambda i: (0, 0)),
        ],
        out_specs=pl.BlockSpec((tm, d), lambda i: (i, 0)),
        out_shape=jax.ShapeDtypeStruct((t, d), F32),
        compiler_params=_params("arbitrary"),
        name="outproj",
    )(x2d, attn, rec, w_bf16)


FFN_DOWN_COLS = 512
FFN_HIDDEN_TILE = 512


def _ffn_kernel(x_hbm, g2_ref, wg_hbm, wu_hbm, wd_hbm, gf_ref, out_hbm,
                acc_ref, h_ref, wg_buf, wu_buf, wd_buf, x_sem, o_sem, w_sem):
    _, tm, d = acc_ref.shape
    n_tiles, _, tf = wg_hbm.shape
    n_row_tiles = x_hbm.shape[0] // tm
    n_total = n_row_tiles * n_tiles
    col_chunks = [slice(n, n + FFN_DOWN_COLS) for n in range(0, d, FFN_DOWN_COLS)]

    def x_copy(i, slot):
        return pltpu.make_async_copy(x_hbm.at[pl.ds(i * tm, tm), :], acc_ref.at[slot],
                                     x_sem.at[slot])

    def out_copy(i, slot):
        return pltpu.make_async_copy(acc_ref.at[slot], out_hbm.at[pl.ds(i * tm, tm), :],
                                     o_sem.at[slot])

    def w_copies(g):
        j = lax.rem(g, n_tiles)
        slot = lax.rem(g, 2)
        return (pltpu.make_async_copy(wg_hbm.at[j], wg_buf.at[slot], w_sem.at[0, slot]),
                pltpu.make_async_copy(wu_hbm.at[j], wu_buf.at[slot], w_sem.at[1, slot]),
                pltpu.make_async_copy(wd_hbm.at[pl.ds(j * tf, tf), :], wd_buf.at[slot],
                                      w_sem.at[2, slot]))

    def weights(g):
        for c in w_copies(g):
            c.wait()

        @pl.when(g + 1 < n_total)
        def _():
            for c in w_copies(g + 1):
                c.start()

        slot = lax.rem(g, 2)
        return wg_buf.at[slot], wu_buf.at[slot], wd_buf.at[slot]

    def swiglu_down(h, w):
        wg, wu, wd = w
        gate = jnp.dot(h, wg[...], preferred_element_type=F32)
        up = jnp.dot(h, wu[...], preferred_element_type=F32)
        act = (gate * jax.nn.sigmoid(gate) * up).astype(BF16)
        return [jnp.dot(act, wd[:, cols], preferred_element_type=F32) for cols in col_chunks]

    def row_tile(i, carry):
        slot = lax.rem(i, 2)
        acc = acc_ref.at[slot]
        g0 = i * n_tiles
        x_copy(i, slot).wait()

        w = weights(g0)
        g2 = g2_ref[...]
        for rows in _row_slabs(tm):
            h = _rms_rows(acc[rows, :], g2).astype(h_ref.dtype)
            h_ref[rows, :] = h
            for cols, y in zip(col_chunks, swiglu_down(h, w)):
                acc[rows, cols] += y

        @pl.when(i >= 1)
        def _():
            out_copy(i - 1, 1 - slot).wait()

        @pl.when(i + 1 < n_row_tiles)
        def _():
            x_copy(i + 1, 1 - slot).start()

        def hidden_tile(j, c):
            w = weights(g0 + j)
            for cols, y in zip(col_chunks, swiglu_down(h_ref[...], w)):
                acc[:, cols] += y
            return c

        lax.fori_loop(1, n_tiles - 1, hidden_tile, 0)

        w = weights(g0 + n_tiles - 1)
        gf = gf_ref[...]
        for rows in _row_slabs(tm):
            ss = jnp.zeros((EDGE_ROWS, 1), F32)
            for cols, y in zip(col_chunks, swiglu_down(h_ref[rows, :], w)):
                o = acc[rows, cols] + y
                acc[rows, cols] = o
                ss = ss + (o * o).sum(axis=-1, keepdims=True)
            acc[rows, :] = acc[rows, :] * lax.rsqrt(ss / d + EPS) * gf
        out_copy(i, slot).start()
        return carry

    x_copy(0, 0).start()
    for c in w_copies(0):
        c.start()
    lax.fori_loop(0, n_row_tiles, row_tile, 0)
    out_copy(n_row_tiles - 1, (n_row_tiles - 1) % 2).wait()


def _ffn(x1, g2, wg_tiles, wu_tiles, wd, gf, tm):
    t, d = x1.shape
    n_tiles, _, tf = wg_tiles.shape
    assert n_tiles >= 2, "the first and the last hidden tile are separate code paths"
    assert t % tm == 0 and tm % EDGE_ROWS == 0
    hbm = pl.BlockSpec(memory_space=pl.ANY)
    whole = pl.BlockSpec(memory_space=pltpu.VMEM)
    return pl.pallas_call(
        _ffn_kernel,
        in_specs=[hbm, whole, hbm, hbm, hbm, whole],
        out_specs=hbm,
        out_shape=jax.ShapeDtypeStruct((t, d), F32),
        scratch_shapes=[
            pltpu.VMEM((2, tm, d), F32),
            pltpu.VMEM((tm, d), BF16),
            pltpu.VMEM((2, d, tf), BF16),
            pltpu.VMEM((2, d, tf), BF16),
            pltpu.VMEM((2, tf, d), BF16),
            pltpu.SemaphoreType.DMA((2,)),
            pltpu.SemaphoreType.DMA((2,)),
            pltpu.SemaphoreType.DMA((3, 2)),
        ],
        compiler_params=pltpu.CompilerParams(vmem_limit_bytes=VMEM_LIMIT_BYTES),
        name="ffn",
    )(x1, g2, wg_tiles, wu_tiles, wd, gf)


def kernel(x, norm1_gain, w_in, rel_bias, lower_bounds, grn_norm_gain, attn_out_gain, w_out,
           norm2_gain, w_gate, w_up, w_down, final_gain):
    batch, seq, d = x.shape
    depth = w_in.shape[0]
    assert seq % TB == 0 and seq % QB == 0 and w_in.shape[2] == 7 * GROUP and d == 2 * GROUP
    lb_all = jnp.cumsum(jax.nn.softmax(lower_bounds.astype(F32), axis=0), axis=0)
    xf = x.reshape(batch * seq, d)
    col_scale = jnp.concatenate([jnp.full((1, GROUP), QUERY_SCALE, F32),
                                 jnp.ones((1, w_in.shape[2] - GROUP), F32)], axis=1)
    for l in range(depth):
        proj = _inproj(xf, norm1_gain[l][None, :], w_in[l].astype(BF16), col_scale,
                       tm=1024, tn=1024)
        attn = _attention(proj, _bias_rows(rel_bias[l]), attn_out_gain[l][None, :], batch, seq)
        rec, wo, wg, wu, wd = _hgrn(proj, lb_all[l][None, :], grn_norm_gain[l][None, :], batch, seq,
                                    [(w_out[l], None), (w_gate[l], FFN_HIDDEN_TILE),
                                     (w_up[l], FFN_HIDDEN_TILE), (w_down[l], None)])
        x1 = _outproj(xf, attn, rec, wo, tm=512)
        last = l == depth - 1
        assert last, "the final rmsnorm is fused into the last layer's FFN"
        xf = _ffn(x1, norm2_gain[l][None, :], wg, wu, wd, final_gain[None, :], tm=1024)
    return xf.reshape(batch, seq, d)
```

```python
import functools

import jax
import jax.numpy as jnp
from jax import lax
from jax.experimental import pallas as pl
from jax.experimental.pallas import tpu as pltpu

F32 = jnp.float32
BF16 = jnp.bfloat16

EPS = 1e-6
CHUNK = 64
N_LEFT = 8
MAX_REL = 128
HEADS = 8
HEAD_DIM = 128
GROUP = HEADS * HEAD_DIM

VMEM_LIMIT_BYTES = 56 * 1024 * 1024
BF16_SUBLANES = 16
NEG = -1e30
LOG2E = 1.4426950408889634
QUERY_SCALE = HEAD_DIM ** -0.5 * LOG2E


def _params(*sem):
    return pltpu.CompilerParams(dimension_semantics=sem, vmem_limit_bytes=VMEM_LIMIT_BYTES)


def _rms_rows(x, gain):
    ms = jnp.mean(x * x, axis=-1, keepdims=True)
    return x * lax.rsqrt(ms + EPS) * gain


EDGE_ROWS = 256


def _row_slabs(n_rows):
    return [slice(r, r + EDGE_ROWS) for r in range(0, n_rows, EDGE_ROWS)]


def _inproj_kernel(x_hbm, gain_ref, w_hbm, cs_ref, out_hbm,
                   x_buf, h_ref, w_buf, o_buf, x_sem, w_sem, o_sem):
    _, tm, d = x_buf.shape
    tn = w_buf.shape[2]
    n_col_tiles = w_hbm.shape[0]
    n_row_tiles = x_hbm.shape[0] // tm
    n_total = n_row_tiles * n_col_tiles

    def x_copy(i, slot):
        return pltpu.make_async_copy(x_hbm.at[pl.ds(i * tm, tm), :], x_buf.at[slot], x_sem.at[slot])

    def w_copy(j, slot):
        return pltpu.make_async_copy(w_hbm.at[j], w_buf.at[slot], w_sem.at[slot])

    def out_copy(i, j, slot):
        return pltpu.make_async_copy(o_buf.at[slot],
                                     out_hbm.at[j, pl.ds(i * tm, tm), :],
                                     o_sem.at[slot])

    def row_tile(i, carry):
        x_slot = lax.rem(i, 2)
        x_copy(i, x_slot).wait()

        @pl.when(i + 1 < n_row_tiles)
        def _():
            x_copy(i + 1, 1 - x_slot).start()

        for j in range(n_col_tiles):
            g = i * n_col_tiles + j
            slot = lax.rem(g, 2)
            w_copy(j, slot).wait()

            @pl.when(g + 1 < n_total)
            def _():
                w_copy((j + 1) % n_col_tiles, 1 - slot).start()

            @pl.when(g >= 2)
            def _():
                out_copy(i - (1 if j < 2 else 0), (j - 2) % n_col_tiles, slot).wait()

            w = w_buf.at[slot]
            o = o_buf.at[slot]
            col_scale = cs_ref[:, j * tn:(j + 1) * tn]

            def project(h):
                y = jnp.dot(h, w[...], preferred_element_type=F32)
                return (y * col_scale).astype(o.dtype)

            if j == 0:
                gain = gain_ref[...]
                x = x_buf.at[x_slot]
                for rows in _row_slabs(tm):
                    h = _rms_rows(x[rows, :], gain).astype(h_ref.dtype)
                    h_ref[rows, :] = h
                    o[rows, :] = project(h)
            else:
                o[...] = project(h_ref[...])
            out_copy(i, j, slot).start()
        return carry

    x_copy(0, 0).start()
    w_copy(0, 0).start()
    lax.fori_loop(0, n_row_tiles, row_tile, 0)
    for g in (n_total - 2, n_total - 1):
        out_copy(g // n_col_tiles, g % n_col_tiles, g % 2).wait()


def _inproj(x2d, gain, w_tiles, col_scale, tm):
    t, d = x2d.shape
    n_col_tiles, _, tn = w_tiles.shape
    assert t % tm == 0 and tm % EDGE_ROWS == 0 and n_col_tiles >= 2
    hbm = pl.BlockSpec(memory_space=pl.ANY)
    whole = pl.BlockSpec(memory_space=pltpu.VMEM)
    return pl.pallas_call(
        _inproj_kernel,
        in_specs=[hbm, whole, hbm, whole],
        out_specs=hbm,
        out_shape=jax.ShapeDtypeStruct((n_col_tiles, t, tn), BF16),
        scratch_shapes=[
            pltpu.VMEM((2, tm, d), F32),
            pltpu.VMEM((tm, d), BF16),
            pltpu.VMEM((2, d, tn), BF16),
            pltpu.VMEM((2, tm, tn), BF16),
            pltpu.SemaphoreType.DMA((2,)),
            pltpu.SemaphoreType.DMA((2,)),
            pltpu.SemaphoreType.DMA((2,)),
        ],
        compiler_params=pltpu.CompilerParams(vmem_limit_bytes=VMEM_LIMIT_BYTES),
        name="inproj",
    )(x2d, gain, w_tiles, col_scale)


QB = 256
N_SEG = 1 + (N_LEFT * CHUNK) // QB
WIN = N_SEG * QB
ROLL_W = 1024


def _bias_rows(rel_bias):
    h = rel_bias.shape[0]
    far_past = rel_bias[:, 2 * MAX_REL:2 * MAX_REL + 1]
    far_future = rel_bias[:, 0:1]
    n_head = (WIN - QB) - MAX_REL
    n_tail = WIN - n_head - (2 * MAX_REL + 1)
    return jnp.concatenate([
        jnp.broadcast_to(far_past, (h, n_head)),
        rel_bias[:, ::-1],
        jnp.broadcast_to(far_future, (h, n_tail)),
        jnp.broadcast_to(far_past, (h, ROLL_W - WIN)),
    ], axis=1).astype(F32)


def _attn_kernel(q_ref, k0_ref, k1_ref, k2_ref, v0_ref, v1_ref, v2_ref, brow_ref, gain_ref,
                 o_ref, bias_ref, acc_ref, s_ref):
    qb = pl.program_id(1)

    @pl.when((pl.program_id(0) == 0) & (qb == 0))
    def _():
        qc = lax.broadcasted_iota(jnp.int32, (QB, WIN), 0) // CHUNK
        kc = lax.broadcasted_iota(jnp.int32, (QB, WIN), 1) // CHUNK
        band = (kc >= qc) & (kc <= qc + N_LEFT)
        for h in range(HEADS):
            row = jnp.broadcast_to(brow_ref[h:h + 1, :] * LOG2E, (QB, ROLL_W))
            toeplitz = pltpu.roll(row, 0, 1, stride=1, stride_axis=0)
            bias_ref[h] = jnp.where(band, toeplitz[:, :WIN], NEG)

    k_refs = (k0_ref, k1_ref, k2_ref)
    v_refs = (v0_ref, v1_ref, v2_ref)
    def lane_halves(op, x):
        while x.shape[1] > HEAD_DIM:
            half = x.shape[1] // 2
            x = op(x[:, :half], x[:, half:])
        return x

    def attend(segs):
        m = []
        for h in range(HEADS):
            sl = slice(h * HEAD_DIM, (h + 1) * HEAD_DIM)
            qh = q_ref[:, sl]
            mx = None
            for j in segs:
                cols = slice(j * QB, (j + 1) * QB)
                sj = lax.dot_general(qh, k_refs[j][:, sl], (((1,), (1,)), ((), ())),
                                     preferred_element_type=F32) + bias_ref[h, :, cols]
                s_ref[h, :, cols] = sj
                mx = sj if mx is None else jnp.maximum(mx, sj)
            m.append(lane_halves(jnp.maximum, mx).max(axis=-1, keepdims=True))
        sq = jnp.zeros((QB, HEAD_DIM), F32)
        ones = jnp.ones((QB, HEAD_DIM), BF16)
        for h in range(HEADS):
            sl = slice(h * HEAD_DIM, (h + 1) * HEAD_DIM)
            p = jnp.concatenate(
                [jnp.exp2(s_ref[h, :, j * QB:(j + 1) * QB] - m[h]).astype(BF16) for j in segs], axis=1)
            v1 = jnp.concatenate(
                [jnp.concatenate([v_refs[j][:, sl], ones], axis=1) for j in segs], axis=0)
            pv = jnp.dot(p, v1, preferred_element_type=F32)
            o = pv[:, :HEAD_DIM] * (1.0 / pv[:, HEAD_DIM:])
            acc_ref[:, sl] = o
            sq = sq + o * o
        inv = lax.rsqrt(sq.sum(axis=-1, keepdims=True) / GROUP + EPS)
        o_ref[...] = (acc_ref[...] * inv * gain_ref[...]).astype(o_ref.dtype)

    for n_valid in range(1, N_SEG + 1):
        @pl.when(jnp.minimum(qb, N_SEG - 1) == n_valid - 1)
        def _():
            attend(tuple(range(N_SEG - n_valid, N_SEG)))


def _attention(proj, brow, gain, batch, seq):
    t = proj.shape[1]
    nqb = seq // QB

    def kv_spec(col_block, seg):
        def index(b, i):
            return (col_block, b * nqb + jnp.maximum(i - (N_SEG - 1 - seg), 0), 0)
        return pl.BlockSpec((None, QB, GROUP), index)

    return pl.pallas_call(
        _attn_kernel,
        grid=(batch, nqb),
        in_specs=[pl.BlockSpec((None, QB, GROUP), lambda b, i: (0, b * nqb + i, 0))]
        + [kv_spec(1, s) for s in range(N_SEG)]
        + [kv_spec(2, s) for s in range(N_SEG)]
        + [pl.BlockSpec((HEADS, ROLL_W), lambda b, i: (0, 0)),
           pl.BlockSpec((1, GROUP), lambda b, i: (0, 0))],
        out_specs=pl.BlockSpec((QB, GROUP), lambda b, i: (b * nqb + i, 0)),
        out_shape=jax.ShapeDtypeStruct((t, GROUP), BF16),
        scratch_shapes=[pltpu.VMEM((HEADS, QB, WIN), F32),
                        pltpu.VMEM((QB, GROUP), F32),
                        pltpu.VMEM((HEADS, QB, WIN), F32)],
        compiler_params=_params("arbitrary", "arbitrary"),
        name="attention",
    )(proj, proj, proj, proj, proj, proj, proj, brow, gain)


TB = 512
HGRN_UNROLL = 4


def _silu(x):
    hx = 0.5 * x
    return hx + hx * jnp.tanh(hx)


def _split3(x):
    hi = x.astype(BF16)
    r1 = x - hi.astype(F32)
    mid = r1.astype(BF16)
    lo = (r1 - mid.astype(F32)).astype(BF16)
    return hi, mid, lo


def _hgrn_kernel(n_cast, q_ref, f_ref, i_ref, g_ref, lb_ref, gn_ref, *refs):
    cast_in, (o_ref, *cast_out) = refs[:n_cast], refs[n_cast:2 * n_cast + 1]
    st_ref, qd_ref, a_ref, u_ref, dec_ref = refs[2 * n_cast + 1:]

    for src, dst in zip(cast_in, cast_out):
        if len(dst.shape) == 2:
            dst[...] = src[...].astype(dst.dtype)
        else:
            width = dst.shape[2]
            for n in range(dst.shape[0]):
                dst[n] = src[:, n * width:(n + 1) * width].astype(dst.dtype)

    @pl.when(pl.program_id(1) == 0)
    def _():
        st_ref[...] = jnp.zeros_like(st_ref)

    lb = lb_ref[...]
    f_mid = 0.5 * (1.0 + lb)
    f_amp = 0.5 * (1.0 - lb)
    gn = gn_ref[...]
    r_i = lax.broadcasted_iota(jnp.int32, (CHUNK, CHUNK), 0)
    c_i = lax.broadcasted_iota(jnp.int32, (CHUNK, CHUNK), 1)
    causal = r_i >= c_i
    tril = causal.astype(BF16)
    tril3 = jnp.concatenate([tril, tril, tril], axis=1)

    def chunk_rows(c):
        return pl.ds(pl.multiple_of(c * CHUNK, CHUNK), CHUNK)

    def local_terms(c):
        rows = chunk_rows(c)
        q = _silu(q_ref[rows, :].astype(F32))
        f = f_mid + f_amp * jnp.tanh(0.5 * f_ref[rows, :].astype(F32))
        k = 1.0 - f
        b = jnp.dot(tril3, jnp.concatenate(_split3(jnp.log2(f)), axis=0),
                    preferred_element_type=F32)
        decay = jnp.exp2(b[CHUNK - 1:CHUNK, :])
        q_dec = (q * jnp.exp2(b)).astype(BF16)
        k_undecayed = k * jnp.exp2(-b)
        k_intra = k_undecayed.astype(BF16)
        k_state = (k_undecayed * decay).astype(BF16)
        qd_ref[rows, :] = q_dec
        dec_ref[c] = decay
        for h in range(HEADS):
            sl = slice(h * HEAD_DIM, (h + 1) * HEAD_DIM)
            a = lax.dot_general(q_dec[:, sl], k_intra[:, sl], (((1,), (1,)), ((), ())),
                                preferred_element_type=F32)
            a_ref[h, rows, :] = jnp.where(causal, a, 0.0).astype(BF16)
            u_ref[c, h] = lax.dot_general(i_ref[rows, sl], k_state[:, sl], (((0,), (0,)), ((), ())),
                                          preferred_element_type=F32)

    def outputs(c):
        rows = chunk_rows(c)
        decay = dec_ref[c]
        for h in range(HEADS):
            sl = slice(h * HEAD_DIM, (h + 1) * HEAD_DIM)
            st = st_ref[h]
            o = jnp.dot(a_ref[h, rows, :], i_ref[rows, sl], preferred_element_type=F32)
            o = o + lax.dot_general(qd_ref[rows, sl], st.astype(BF16), (((1,), (1,)), ((), ())),
                                    preferred_element_type=F32)
            st_ref[h] = st * decay[:, sl] + u_ref[c, h]
            o = _rms_rows(o, gn)
            o_ref[rows, sl] = (o * _silu(g_ref[rows, sl].astype(F32))).astype(o_ref.dtype)

    def unrolled(fn):
        def body(i, carry):
            for u in range(HGRN_UNROLL):
                fn(i * HGRN_UNROLL + u)
            return carry
        return body

    n_iter = TB // (CHUNK * HGRN_UNROLL)
    lax.fori_loop(0, n_iter, unrolled(local_terms), 0)
    lax.fori_loop(0, n_iter, unrolled(outputs), 0)


def _hgrn(proj, lb, gn, batch, seq, weights):
    t = proj.shape[1]
    nb = seq // TB
    steps = batch * nb
    for w, _ in weights:
        assert w.shape[0] % (steps * BF16_SUBLANES) == 0, w.shape

    def col(cb):
        return pl.BlockSpec((None, TB, GROUP), lambda b, i: (cb, b * nb + i, 0))

    def slab(w):
        return pl.BlockSpec((w.shape[0] // steps, w.shape[1]), lambda b, i: (b * nb + i, 0))

    def cast_shape(w, tile):
        return w.shape if tile is None else (w.shape[1] // tile, w.shape[0], tile)

    def cast_slab(w, tile):
        if tile is None:
            return slab(w)
        return pl.BlockSpec((w.shape[1] // tile, w.shape[0] // steps, tile),
                            lambda b, i: (0, b * nb + i, 0))

    return pl.pallas_call(
        functools.partial(_hgrn_kernel, len(weights)),
        grid=(batch, nb),
        in_specs=[col(3), col(4), col(5), col(6),
                  pl.BlockSpec((1, GROUP), lambda b, i: (0, 0)),
                  pl.BlockSpec((1, HEAD_DIM), lambda b, i: (0, 0))]
        + [slab(w) for w, _ in weights],
        out_specs=[pl.BlockSpec((TB, GROUP), lambda b, i: (b * nb + i, 0))]
        + [cast_slab(w, tile) for w, tile in weights],
        out_shape=[jax.ShapeDtypeStruct((t, GROUP), BF16)]
        + [jax.ShapeDtypeStruct(cast_shape(w, tile), BF16) for w, tile in weights],
        scratch_shapes=[
            pltpu.VMEM((HEADS, HEAD_DIM, HEAD_DIM), F32),
            pltpu.VMEM((TB, GROUP), BF16),
            pltpu.VMEM((HEADS, TB, CHUNK), BF16),
            pltpu.VMEM((TB // CHUNK, HEADS, HEAD_DIM, HEAD_DIM), F32),
            pltpu.VMEM((TB // CHUNK, 1, GROUP), F32),
        ],
        compiler_params=_params("arbitrary", "arbitrary"),
        name="hgrn2",
    )(proj, proj, proj, proj, lb, gn, *[w for w, _ in weights])


OUT_COLS = 512


def _outproj_kernel(x_ref, a_ref, r_ref, w_ref, o_ref):
    mixed = jnp.concatenate([a_ref[...], r_ref[...]], axis=1)
    for n in range(0, o_ref.shape[1], OUT_COLS):
        cols = slice(n, n + OUT_COLS)
        o_ref[:, cols] = x_ref[:, cols] + jnp.dot(mixed, w_ref[:, cols],
                                                  preferred_element_type=F32)


def _outproj(x2d, attn, rec, w_bf16, tm):
    t, d = x2d.shape
    g = attn.shape[1]
    return pl.pallas_call(
        _outproj_kernel,
        grid=(t // tm,),
        in_specs=[
            pl.BlockSpec((tm, d), lambda i: (i, 0)),
            pl.BlockSpec((tm, g), lambda i: (i, 0)),
            pl.BlockSpec((tm, g), lambda i: (i, 0)),
            pl.BlockSpec((2 * g, d), lambda i: (0, 0)),
        ],
        out_specs=pl.BlockSpec((tm, d), lambda i: (i, 0)),
        out_shape=jax.ShapeDtypeStruct((t, d), F32),
        compiler_params=_params("arbitrary"),
        name="outproj",
    )(x2d, attn, rec, w_bf16)


FFN_DOWN_COLS = 512
FFN_HIDDEN_TILE = 512


def _ffn_kernel(x_hbm, g2_ref, wg_hbm, wu_hbm, wd_hbm, gf_ref, out_hbm,
                acc_ref, h_ref, wg_buf, wu_buf, wd_buf, x_sem, o_sem, w_sem):
    _, tm, d = acc_ref.shape
    n_tiles, _, tf = wg_hbm.shape
    n_row_tiles = x_hbm.shape[0] // tm
    n_total = n_row_tiles * n_tiles
    col_chunks = [slice(n, n + FFN_DOWN_COLS) for n in range(0, d, FFN_DOWN_COLS)]

    def x_copy(i, slot):
        return pltpu.make_async_copy(x_hbm.at[pl.ds(i * tm, tm), :], acc_ref.at[slot],
                                     x_sem.at[slot])

    def out_copy(i, slot):
        return pltpu.make_async_copy(acc_ref.at[slot], out_hbm.at[pl.ds(i * tm, tm), :],
                                     o_sem.at[slot])

    def w_copies(g):
        j = lax.rem(g, n_tiles)
        slot = lax.rem(g, 2)
        return (pltpu.make_async_copy(wg_hbm.at[j], wg_buf.at[slot], w_sem.at[0, slot]),
                pltpu.make_async_copy(wu_hbm.at[j], wu_buf.at[slot], w_sem.at[1, slot]),
                pltpu.make_async_copy(wd_hbm.at[pl.ds(j * tf, tf), :], wd_buf.at[slot],
                                      w_sem.at[2, slot]))

    def weights(g):
        for c in w_copies(g):
            c.wait()

        @pl.when(g + 1 < n_total)
        def _():
            for c in w_copies(g + 1):
                c.start()

        slot = lax.rem(g, 2)
        return wg_buf.at[slot], wu_buf.at[slot], wd_buf.at[slot]

    def swiglu_down(h, w):
        wg, wu, wd = w
        gate = jnp.dot(h, wg[...], preferred_element_type=F32)
        up = jnp.dot(h, wu[...], preferred_element_type=F32)
        act = (gate * jax.nn.sigmoid(gate) * up).astype(BF16)
        return [jnp.dot(act, wd[:, cols], preferred_element_type=F32) for cols in col_chunks]

    def row_tile(i, carry):
        slot = lax.rem(i, 2)
        acc = acc_ref.at[slot]
        g0 = i * n_tiles
        x_copy(i, slot).wait()

        w = weights(g0)
        g2 = g2_ref[...]
        for rows in _row_slabs(tm):
            h = _rms_rows(acc[rows, :], g2).astype(h_ref.dtype)
            h_ref[rows, :] = h
            for cols, y in zip(col_chunks, swiglu_down(h, w)):
                acc[rows, cols] += y

        @pl.when(i >= 1)
        def _():
            out_copy(i - 1, 1 - slot).wait()

        @pl.when(i + 1 < n_row_tiles)
        def _():
            x_copy(i + 1, 1 - slot).start()

        def hidden_tile(j, c):
            w = weights(g0 + j)
            for cols, y in zip(col_chunks, swiglu_down(h_ref[...], w)):
                acc[:, cols] += y
            return c

        lax.fori_loop(1, n_tiles - 1, hidden_tile, 0)

        w = weights(g0 + n_tiles - 1)
        gf = gf_ref[...]
        for rows in _row_slabs(tm):
            ss = jnp.zeros((EDGE_ROWS, 1), F32)
            for cols, y in zip(col_chunks, swiglu_down(h_ref[rows, :], w)):
                o = acc[rows, cols] + y
                acc[rows, cols] = o
                ss = ss + (o * o).sum(axis=-1, keepdims=True)
            acc[rows, :] = acc[rows, :] * lax.rsqrt(ss / d + EPS) * gf
        out_copy(i, slot).start()
        return carry

    x_copy(0, 0).start()
    for c in w_copies(0):
        c.start()
    lax.fori_loop(0, n_row_tiles, row_tile, 0)
    out_copy(n_row_tiles - 1, (n_row_tiles - 1) % 2).wait()


def _ffn(x1, g2, wg_tiles, wu_tiles, wd, gf, tm):
    t, d = x1.shape
    n_tiles, _, tf = wg_tiles.shape
    assert n_tiles >= 2, "the first and the last hidden tile are separate code paths"
    assert t % tm == 0 and tm % EDGE_ROWS == 0
    hbm = pl.BlockSpec(memory_space=pl.ANY)
    whole = pl.BlockSpec(memory_space=pltpu.VMEM)
    return pl.pallas_call(
        _ffn_kernel,
        in_specs=[hbm, whole, hbm, hbm, hbm, whole],
        out_specs=hbm,
        out_shape=jax.ShapeDtypeStruct((t, d), F32),
        scratch_shapes=[
            pltpu.VMEM((2, tm, d), F32),
            pltpu.VMEM((tm, d), BF16),
            pltpu.VMEM((2, d, tf), BF16),
            pltpu.VMEM((2, d, tf), BF16),
            pltpu.VMEM((2, tf, d), BF16),
            pltpu.SemaphoreType.DMA((2,)),
            pltpu.SemaphoreType.DMA((2,)),
            pltpu.SemaphoreType.DMA((3, 2)),
        ],
        compiler_params=pltpu.CompilerParams(vmem_limit_bytes=VMEM_LIMIT_BYTES),
        name="ffn",
    )(x1, g2, wg_tiles, wu_tiles, wd, gf)


def kernel(x, norm1_gain, w_in, rel_bias, lower_bounds, grn_norm_gain, attn_out_gain, w_out,
           norm2_gain, w_gate, w_up, w_down, final_gain):
    batch, seq, d = x.shape
    depth = w_in.shape[0]
    assert seq % TB == 0 and seq % QB == 0 and w_in.shape[2] == 7 * GROUP and d == 2 * GROUP
    lb_all = jnp.cumsum(jax.nn.softmax(lower_bounds.astype(F32), axis=0), axis=0)
    xf = x.reshape(batch * seq, d)
    col_scale = jnp.concatenate([jnp.full((1, GROUP), QUERY_SCALE, F32),
                                 jnp.ones((1, w_in.shape[2] - GROUP), F32)], axis=1)
    for l in range(depth):
        w_in_tiles = w_in[l].reshape(d, -1, GROUP).transpose(1, 0, 2).astype(BF16)
        proj = _inproj(xf, norm1_gain[l][None, :], w_in_tiles, col_scale, tm=1024)
        attn = _attention(proj, _bias_rows(rel_bias[l]), attn_out_gain[l][None, :], batch, seq)
        rec, wo, wg, wu, wd = _hgrn(proj, lb_all[l][None, :], grn_norm_gain[l][None, :], batch, seq,
                                    [(w_out[l], None), (w_gate[l], FFN_HIDDEN_TILE),
                                     (w_up[l], FFN_HIDDEN_TILE), (w_down[l], None)])
        x1 = _outproj(xf, attn, rec, wo, tm=512)
        last = l == depth - 1
        assert last, "the final rmsnorm is fused into the last layer's FFN"
        xf = _ffn(x1, norm2_gain[l][None, :], wg, wu, wd, final_gain[None, :], tm=1024)
    return xf.reshape(batch, seq, d)
```

```python
import functools

import jax
import jax.numpy as jnp
from jax import lax
from jax.experimental import pallas as pl
from jax.experimental.pallas import tpu as pltpu

F32 = jnp.float32
BF16 = jnp.bfloat16

EPS = 1e-6
CHUNK = 64
N_LEFT = 8
MAX_REL = 128
HEADS = 8
HEAD_DIM = 128
GROUP = HEADS * HEAD_DIM

VMEM_LIMIT_BYTES = 56 * 1024 * 1024
BF16_SUBLANES = 16
NEG = -1e30
LOG2E = 1.4426950408889634
QUERY_SCALE = HEAD_DIM ** -0.5 * LOG2E


def _params(*sem):
    return pltpu.CompilerParams(dimension_semantics=sem, vmem_limit_bytes=VMEM_LIMIT_BYTES)


def _rms_rows(x, gain):
    ms = jnp.mean(x * x, axis=-1, keepdims=True)
    return x * lax.rsqrt(ms + EPS) * gain


EDGE_ROWS = 256


def _row_slabs(n_rows):
    return [slice(r, r + EDGE_ROWS) for r in range(0, n_rows, EDGE_ROWS)]


def _inproj_kernel(x_ref, gain_ref, w_ref, cs_ref, o_ref, h_ref):
    j = pl.program_id(1)
    col_scale = cs_ref[...]

    def project(h):
        return (jnp.dot(h, w_ref[...], preferred_element_type=F32) * col_scale).astype(o_ref.dtype)

    @pl.when(j == 0)
    def _():
        gain = gain_ref[...]
        for rows in _row_slabs(x_ref.shape[0]):
            h = _rms_rows(x_ref[rows, :], gain).astype(h_ref.dtype)
            h_ref[rows, :] = h
            o_ref[rows, :] = project(h)

    @pl.when(j > 0)
    def _():
        o_ref[...] = project(h_ref[...])


def _inproj(x2d, gain, w_bf16, col_scale, tm, tn):
    t, d = x2d.shape
    n = w_bf16.shape[1]
    return pl.pallas_call(
        _inproj_kernel,
        grid=(t // tm, n // tn),
        in_specs=[
            pl.BlockSpec((tm, d), lambda i, j: (i, 0)),
            pl.BlockSpec((1, d), lambda i, j: (0, 0)),
            pl.BlockSpec((d, tn), lambda i, j: (0, j)),
            pl.BlockSpec((1, tn), lambda i, j: (0, j)),
        ],
        out_specs=pl.BlockSpec((tm, tn), lambda i, j: (i, j)),
        out_shape=jax.ShapeDtypeStruct((t, n), BF16),
        scratch_shapes=[pltpu.VMEM((tm, d), BF16)],
        compiler_params=_params("arbitrary", "arbitrary"),
        name="inproj",
    )(x2d, gain, w_bf16, col_scale)


QB = 256
N_SEG = 1 + (N_LEFT * CHUNK) // QB
WIN = N_SEG * QB
ROLL_W = 1024


def _bias_rows(rel_bias):
    h = rel_bias.shape[0]
    far_past = rel_bias[:, 2 * MAX_REL:2 * MAX_REL + 1]
    far_future = rel_bias[:, 0:1]
    n_head = (WIN - QB) - MAX_REL
    n_tail = WIN - n_head - (2 * MAX_REL + 1)
    return jnp.concatenate([
        jnp.broadcast_to(far_past, (h, n_head)),
        rel_bias[:, ::-1],
        jnp.broadcast_to(far_future, (h, n_tail)),
        jnp.broadcast_to(far_past, (h, ROLL_W - WIN)),
    ], axis=1).astype(F32)


def _attn_kernel(q_ref, k0_ref, k1_ref, k2_ref, v0_ref, v1_ref, v2_ref, brow_ref, gain_ref,
                 o_ref, bias_ref, acc_ref, s_ref):
    qb = pl.program_id(1)

    @pl.when((pl.program_id(0) == 0) & (qb == 0))
    def _():
        qc = lax.broadcasted_iota(jnp.int32, (QB, WIN), 0) // CHUNK
        kc = lax.broadcasted_iota(jnp.int32, (QB, WIN), 1) // CHUNK
        band = (kc >= qc) & (kc <= qc + N_LEFT)
        for h in range(HEADS):
            row = jnp.broadcast_to(brow_ref[h:h + 1, :] * LOG2E, (QB, ROLL_W))
            toeplitz = pltpu.roll(row, 0, 1, stride=1, stride_axis=0)
            bias_ref[h] = jnp.where(band, toeplitz[:, :WIN], NEG)

    k_refs = (k0_ref, k1_ref, k2_ref)
    v_refs = (v0_ref, v1_ref, v2_ref)
    def lane_halves(op, x):
        while x.shape[1] > HEAD_DIM:
            half = x.shape[1] // 2
            x = op(x[:, :half], x[:, half:])
        return x

    def attend(segs):
        m = []
        for h in range(HEADS):
            sl = slice(h * HEAD_DIM, (h + 1) * HEAD_DIM)
            qh = q_ref[:, sl]
            mx = None
            for j in segs:
                cols = slice(j * QB, (j + 1) * QB)
                sj = lax.dot_general(qh, k_refs[j][:, sl], (((1,), (1,)), ((), ())),
                                     preferred_element_type=F32) + bias_ref[h, :, cols]
                s_ref[h, :, cols] = sj
                mx = sj if mx is None else jnp.maximum(mx, sj)
            m.append(lane_halves(jnp.maximum, mx).max(axis=-1, keepdims=True))
        sq = jnp.zeros((QB, HEAD_DIM), F32)
        ones = jnp.ones((QB, HEAD_DIM), BF16)
        for h in range(HEADS):
            sl = slice(h * HEAD_DIM, (h + 1) * HEAD_DIM)
            p = jnp.concatenate(
                [jnp.exp2(s_ref[h, :, j * QB:(j + 1) * QB] - m[h]).astype(BF16) for j in segs], axis=1)
            v1 = jnp.concatenate(
                [jnp.concatenate([v_refs[j][:, sl], ones], axis=1) for j in segs], axis=0)
            pv = jnp.dot(p, v1, preferred_element_type=F32)
            o = pv[:, :HEAD_DIM] * (1.0 / pv[:, HEAD_DIM:])
            acc_ref[:, sl] = o
            sq = sq + o * o
        inv = lax.rsqrt(sq.sum(axis=-1, keepdims=True) / GROUP + EPS)
        o_ref[...] = (acc_ref[...] * inv * gain_ref[...]).astype(o_ref.dtype)

    for n_valid in range(1, N_SEG + 1):
        @pl.when(jnp.minimum(qb, N_SEG - 1) == n_valid - 1)
        def _():
            attend(tuple(range(N_SEG - n_valid, N_SEG)))


def _attention(proj, brow, gain, batch, seq):
    t = proj.shape[0]
    nqb = seq // QB

    def kv_spec(col_block, seg):
        def index(b, i):
            return (b * nqb + jnp.maximum(i - (N_SEG - 1 - seg), 0), col_block)
        return pl.BlockSpec((QB, GROUP), index)

    return pl.pallas_call(
        _attn_kernel,
        grid=(batch, nqb),
        in_specs=[pl.BlockSpec((QB, GROUP), lambda b, i: (b * nqb + i, 0))]
        + [kv_spec(1, s) for s in range(N_SEG)]
        + [kv_spec(2, s) for s in range(N_SEG)]
        + [pl.BlockSpec((HEADS, ROLL_W), lambda b, i: (0, 0)),
           pl.BlockSpec((1, GROUP), lambda b, i: (0, 0))],
        out_specs=pl.BlockSpec((QB, GROUP), lambda b, i: (b * nqb + i, 0)),
        out_shape=jax.ShapeDtypeStruct((t, GROUP), BF16),
        scratch_shapes=[pltpu.VMEM((HEADS, QB, WIN), F32),
                        pltpu.VMEM((QB, GROUP), F32),
                        pltpu.VMEM((HEADS, QB, WIN), F32)],
        compiler_params=_params("arbitrary", "arbitrary"),
        name="attention",
    )(proj, proj, proj, proj, proj, proj, proj, brow, gain)


TB = 512
HGRN_UNROLL = 8


def _silu(x):
    hx = 0.5 * x
    return hx + hx * jnp.tanh(hx)


def _split3(x):
    hi = x.astype(BF16)
    r1 = x - hi.astype(F32)
    mid = r1.astype(BF16)
    lo = (r1 - mid.astype(F32)).astype(BF16)
    return hi, mid, lo


def _hgrn_kernel(n_cast, q_ref, f_ref, i_ref, g_ref, lb_ref, gn_ref, *refs):
    cast_in, (o_ref, *cast_out) = refs[:n_cast], refs[n_cast:2 * n_cast + 1]
    st_ref, qd_ref, a_ref, u_ref, dec_ref = refs[2 * n_cast + 1:]

    for src, dst in zip(cast_in, cast_out):
        if len(dst.shape) == 2:
            dst[...] = src[...].astype(dst.dtype)
        else:
            width = dst.shape[2]
            for n in range(dst.shape[0]):
                dst[n] = src[:, n * width:(n + 1) * width].astype(dst.dtype)

    @pl.when(pl.program_id(1) == 0)
    def _():
        st_ref[...] = jnp.zeros_like(st_ref)

    lb = lb_ref[...]
    f_mid = 0.5 * (1.0 + lb)
    f_amp = 0.5 * (1.0 - lb)
    gn = gn_ref[...]
    r_i = lax.broadcasted_iota(jnp.int32, (CHUNK, CHUNK), 0)
    c_i = lax.broadcasted_iota(jnp.int32, (CHUNK, CHUNK), 1)
    causal = r_i >= c_i
    tril = causal.astype(BF16)
    tril3 = jnp.concatenate([tril, tril, tril], axis=1)

    def chunk_rows(c):
        return pl.ds(pl.multiple_of(c * CHUNK, CHUNK), CHUNK)

    def local_terms(c):
        rows = chunk_rows(c)
        q = _silu(q_ref[rows, :].astype(F32))
        f = f_mid + f_amp * jnp.tanh(0.5 * f_ref[rows, :].astype(F32))
        k = 1.0 - f
        b = jnp.dot(tril3, jnp.concatenate(_split3(jnp.log2(f)), axis=0),
                    preferred_element_type=F32)
        decay = jnp.exp2(b[CHUNK - 1:CHUNK, :])
        q_dec = (q * jnp.exp2(b)).astype(BF16)
        k_undecayed = k * jnp.exp2(-b)
        k_intra = k_undecayed.astype(BF16)
        k_state = (k_undecayed * decay).astype(BF16)
        qd_ref[rows, :] = q_dec
        dec_ref[c] = decay
        for h in range(HEADS):
            sl = slice(h * HEAD_DIM, (h + 1) * HEAD_DIM)
            a = lax.dot_general(q_dec[:, sl], k_intra[:, sl], (((1,), (1,)), ((), ())),
                                preferred_element_type=F32)
            a_ref[h, rows, :] = jnp.where(causal, a, 0.0).astype(BF16)
            u_ref[c, h] = lax.dot_general(i_ref[rows, sl], k_state[:, sl], (((0,), (0,)), ((), ())),
                                          preferred_element_type=F32)

    def outputs(c):
        rows = chunk_rows(c)
        decay = dec_ref[c]
        for h in range(HEADS):
            sl = slice(h * HEAD_DIM, (h + 1) * HEAD_DIM)
            st = st_ref[h]
            o = jnp.dot(a_ref[h, rows, :], i_ref[rows, sl], preferred_element_type=F32)
            o = o + lax.dot_general(qd_ref[rows, sl], st.astype(BF16), (((1,), (1,)), ((), ())),
                                    preferred_element_type=F32)
            st_ref[h] = st * decay[:, sl] + u_ref[c, h]
            o = _rms_rows(o, gn)
            o_ref[rows, sl] = (o * _silu(g_ref[rows, sl].astype(F32))).astype(o_ref.dtype)

    def unrolled(fn):
        def body(i, carry):
            for u in range(HGRN_UNROLL):
                fn(i * HGRN_UNROLL + u)
            return carry
        return body

    n_iter = TB // (CHUNK * HGRN_UNROLL)
    lax.fori_loop(0, n_iter, unrolled(local_terms), 0)
    lax.fori_loop(0, n_iter, unrolled(outputs), 0)


def _hgrn(proj, lb, gn, batch, seq, weights):
    t = proj.shape[0]
    nb = seq // TB
    steps = batch * nb
    for w, _ in weights:
        assert w.shape[0] % (steps * BF16_SUBLANES) == 0, w.shape

    def col(cb):
        return pl.BlockSpec((TB, GROUP), lambda b, i: (b * nb + i, cb))

    def slab(w):
        return pl.BlockSpec((w.shape[0] // steps, w.shape[1]), lambda b, i: (b * nb + i, 0))

    def cast_shape(w, tile):
        return w.shape if tile is None else (w.shape[1] // tile, w.shape[0], tile)

    def cast_slab(w, tile):
        if tile is None:
            return slab(w)
        return pl.BlockSpec((w.shape[1] // tile, w.shape[0] // steps, tile),
                            lambda b, i: (0, b * nb + i, 0))

    return pl.pallas_call(
        functools.partial(_hgrn_kernel, len(weights)),
        grid=(batch, nb),
        in_specs=[col(3), col(4), col(5), col(6),
                  pl.BlockSpec((1, GROUP), lambda b, i: (0, 0)),
                  pl.BlockSpec((1, HEAD_DIM), lambda b, i: (0, 0))]
        + [slab(w) for w, _ in weights],
        out_specs=[pl.BlockSpec((TB, GROUP), lambda b, i: (b * nb + i, 0))]
        + [cast_slab(w, tile) for w, tile in weights],
        out_shape=[jax.ShapeDtypeStruct((t, GROUP), BF16)]
        + [jax.ShapeDtypeStruct(cast_shape(w, tile), BF16) for w, tile in weights],
        scratch_shapes=[
            pltpu.VMEM((HEADS, HEAD_DIM, HEAD_DIM), F32),
            pltpu.VMEM((TB, GROUP), BF16),
            pltpu.VMEM((HEADS, TB, CHUNK), BF16),
            pltpu.VMEM((TB // CHUNK, HEADS, HEAD_DIM, HEAD_DIM), F32),
            pltpu.VMEM((TB // CHUNK, 1, GROUP), F32),
        ],
        compiler_params=_params("arbitrary", "arbitrary"),
        name="hgrn2",
    )(proj, proj, proj, proj, lb, gn, *[w for w, _ in weights])


OUT_COLS = 512


def _outproj_kernel(x_ref, a_ref, r_ref, w_ref, o_ref):
    mixed = jnp.concatenate([a_ref[...], r_ref[...]], axis=1)
    for n in range(0, o_ref.shape[1], OUT_COLS):
        cols = slice(n, n + OUT_COLS)
        o_ref[:, cols] = x_ref[:, cols] + jnp.dot(mixed, w_ref[:, cols],
                                                  preferred_element_type=F32)


def _outproj(x2d, attn, rec, w_bf16, tm):
    t, d = x2d.shape
    g = attn.shape[1]
    return pl.pallas_call(
        _outproj_kernel,
        grid=(t // tm,),
        in_specs=[
            pl.BlockSpec((tm, d), lambda i: (i, 0)),
            pl.BlockSpec((tm, g), lambda i: (i, 0)),
            pl.BlockSpec((tm, g), lambda i: (i, 0)),
            pl.BlockSpec((2 * g, d), lambda i: (0, 0), pipeline_mode=pl.Buffered(1)),
        ],
        out_specs=pl.BlockSpec((tm, d), lambda i: (i, 0)),
        out_shape=jax.ShapeDtypeStruct((t, d), F32),
        compiler_params=_params("arbitrary"),
        name="outproj",
    )(x2d, attn, rec, w_bf16)


FFN_DOWN_COLS = 512
FFN_HIDDEN_TILE = 512


def _ffn_kernel(x_hbm, g2_ref, wg_hbm, wu_hbm, wd_hbm, gf_ref, out_hbm,
                acc_ref, h_ref, wg_buf, wu_buf, wd_buf, x_sem, o_sem, w_sem):
    _, tm, d = acc_ref.shape
    n_tiles, _, tf = wg_hbm.shape
    n_row_tiles = x_hbm.shape[0] // tm
    n_total = n_row_tiles * n_tiles
    col_chunks = [slice(n, n + FFN_DOWN_COLS) for n in range(0, d, FFN_DOWN_COLS)]

    def x_copy(i, slot):
        return pltpu.make_async_copy(x_hbm.at[pl.ds(i * tm, tm), :], acc_ref.at[slot],
                                     x_sem.at[slot])

    def out_copy(i, slot):
        return pltpu.make_async_copy(acc_ref.at[slot], out_hbm.at[pl.ds(i * tm, tm), :],
                                     o_sem.at[slot])

    def w_copies(g):
        j = lax.rem(g, n_tiles)
        slot = lax.rem(g, 2)
        return (pltpu.make_async_copy(wg_hbm.at[j], wg_buf.at[slot], w_sem.at[0, slot]),
                pltpu.make_async_copy(wu_hbm.at[j], wu_buf.at[slot], w_sem.at[1, slot]),
                pltpu.make_async_copy(wd_hbm.at[pl.ds(j * tf, tf), :], wd_buf.at[slot],
                                      w_sem.at[2, slot]))

    def weights(g):
        for c in w_copies(g):
            c.wait()

        @pl.when(g + 1 < n_total)
        def _():
            for c in w_copies(g + 1):
                c.start()

        slot = lax.rem(g, 2)
        return wg_buf.at[slot], wu_buf.at[slot], wd_buf.at[slot]

    def swiglu_down(h, w):
        wg, wu, wd = w
        gate = jnp.dot(h, wg[...], preferred_element_type=F32)
        up = jnp.dot(h, wu[...], preferred_element_type=F32)
        act = (gate * jax.nn.sigmoid(gate) * up).astype(BF16)
        return [jnp.dot(act, wd[:, cols], preferred_element_type=F32) for cols in col_chunks]

    def row_tile(i, carry):
        slot = lax.rem(i, 2)
        acc = acc_ref.at[slot]
        g0 = i * n_tiles
        x_copy(i, slot).wait()

        w = weights(g0)
        g2 = g2_ref[...]
        for rows in _row_slabs(tm):
            h = _rms_rows(acc[rows, :], g2).astype(h_ref.dtype)
            h_ref[rows, :] = h
            for cols, y in zip(col_chunks, swiglu_down(h, w)):
                acc[rows, cols] += y

        @pl.when(i >= 1)
        def _():
            out_copy(i - 1, 1 - slot).wait()

        @pl.when(i + 1 < n_row_tiles)
        def _():
            x_copy(i + 1, 1 - slot).start()

        def hidden_tile(j, c):
            w = weights(g0 + j)
            for cols, y in zip(col_chunks, swiglu_down(h_ref[...], w)):
                acc[:, cols] += y
            return c

        lax.fori_loop(1, n_tiles - 1, hidden_tile, 0)

        w = weights(g0 + n_tiles - 1)
        gf = gf_ref[...]
        for rows in _row_slabs(tm):
            ss = jnp.zeros((EDGE_ROWS, 1), F32)
            for cols, y in zip(col_chunks, swiglu_down(h_ref[rows, :], w)):
                o = acc[rows, cols] + y
                acc[rows, cols] = o
                ss = ss + (o * o).sum(axis=-1, keepdims=True)
            acc[rows, :] = acc[rows, :] * lax.rsqrt(ss / d + EPS) * gf
        out_copy(i, slot).start()
        return carry

    x_copy(0, 0).start()
    for c in w_copies(0):
        c.start()
    lax.fori_loop(0, n_row_tiles, row_tile, 0)
    out_copy(n_row_tiles - 1, (n_row_tiles - 1) % 2).wait()


def _ffn(x1, g2, wg_tiles, wu_tiles, wd, gf, tm):
    t, d = x1.shape
    n_tiles, _, tf = wg_tiles.shape
    assert n_tiles >= 2, "the first and the last hidden tile are separate code paths"
    assert t % tm == 0 and tm % EDGE_ROWS == 0
    hbm = pl.BlockSpec(memory_space=pl.ANY)
    whole = pl.BlockSpec(memory_space=pltpu.VMEM)
    return pl.pallas_call(
        _ffn_kernel,
        in_specs=[hbm, whole, hbm, hbm, hbm, whole],
        out_specs=hbm,
        out_shape=jax.ShapeDtypeStruct((t, d), F32),
        scratch_shapes=[
            pltpu.VMEM((2, tm, d), F32),
            pltpu.VMEM((tm, d), BF16),
            pltpu.VMEM((2, d, tf), BF16),
            pltpu.VMEM((2, d, tf), BF16),
            pltpu.VMEM((2, tf, d), BF16),
            pltpu.SemaphoreType.DMA((2,)),
            pltpu.SemaphoreType.DMA((2,)),
            pltpu.SemaphoreType.DMA((3, 2)),
        ],
        compiler_params=pltpu.CompilerParams(vmem_limit_bytes=VMEM_LIMIT_BYTES),
        name="ffn",
    )(x1, g2, wg_tiles, wu_tiles, wd, gf)


def kernel(x, norm1_gain, w_in, rel_bias, lower_bounds, grn_norm_gain, attn_out_gain, w_out,
           norm2_gain, w_gate, w_up, w_down, final_gain):
    batch, seq, d = x.shape
    depth = w_in.shape[0]
    assert seq % TB == 0 and seq % QB == 0 and w_in.shape[2] == 7 * GROUP and d == 2 * GROUP
    lb_all = jnp.cumsum(jax.nn.softmax(lower_bounds.astype(F32), axis=0), axis=0)
    xf = x.reshape(batch * seq, d)
    col_scale = jnp.concatenate([jnp.full((1, GROUP), QUERY_SCALE, F32),
                                 jnp.ones((1, w_in.shape[2] - GROUP), F32)], axis=1)
    for l in range(depth):
        proj = _inproj(xf, norm1_gain[l][None, :], w_in[l].astype(BF16), col_scale,
                       tm=1024, tn=1792)
        attn = _attention(proj, _bias_rows(rel_bias[l]), attn_out_gain[l][None, :], batch, seq)
        rec, wo, wg, wu, wd = _hgrn(proj, lb_all[l][None, :], grn_norm_gain[l][None, :], batch, seq,
                                    [(w_out[l], None), (w_gate[l], FFN_HIDDEN_TILE),
                                     (w_up[l], FFN_HIDDEN_TILE), (w_down[l], None)])
        x1 = _outproj(xf, attn, rec, wo, tm=1024)
        last = l == depth - 1
        assert last, "the final rmsnorm is fused into the last layer's FFN"
        xf = _ffn(x1, norm2_gain[l][None, :], wg, wu, wd, final_gain[None, :], tm=1024)
    return xf.reshape(batch, seq, d)
```

```python
import functools

import jax
import jax.numpy as jnp
from jax import lax
from jax.experimental import pallas as pl
from jax.experimental.pallas import tpu as pltpu

F32 = jnp.float32
BF16 = jnp.bfloat16

EPS = 1e-6
CHUNK = 64
N_LEFT = 8
MAX_REL = 128
HEADS = 8
HEAD_DIM = 128
GROUP = HEADS * HEAD_DIM

VMEM_LIMIT_BYTES = 56 * 1024 * 1024
BF16_SUBLANES = 16
NEG = -1e30
LOG2E = 1.4426950408889634
QUERY_SCALE = HEAD_DIM ** -0.5 * LOG2E


def _params(*sem):
    return pltpu.CompilerParams(dimension_semantics=sem, vmem_limit_bytes=VMEM_LIMIT_BYTES)


def _rms_rows(x, gain):
    ms = jnp.mean(x * x, axis=-1, keepdims=True)
    return x * lax.rsqrt(ms + EPS) * gain


EDGE_ROWS = 256


def _row_slabs(n_rows):
    return [slice(r, r + EDGE_ROWS) for r in range(0, n_rows, EDGE_ROWS)]


def _inproj_kernel(n_cast, x_ref, gain_ref, w_ref, cs_ref, *refs):
    cast_in = refs[:n_cast]
    o_ref, *cast_out = refs[n_cast:2 * n_cast + 1]
    h_ref = refs[2 * n_cast + 1]
    j = pl.program_id(1)
    col_scale = cs_ref[...]

    def project(h):
        return (jnp.dot(h, w_ref[...], preferred_element_type=F32) * col_scale).astype(o_ref.dtype)

    @pl.when(j == 0)
    def _():
        _cast_slabs(cast_in, cast_out)
        gain = gain_ref[...]
        for rows in _row_slabs(x_ref.shape[0]):
            h = _rms_rows(x_ref[rows, :], gain).astype(h_ref.dtype)
            h_ref[rows, :] = h
            o_ref[rows, :] = project(h)

    @pl.when(j > 0)
    def _():
        _cast_slabs(cast_in, cast_out)
        o_ref[...] = project(h_ref[...])


def _inproj(x2d, gain, w_bf16, col_scale, tm, tn, weights):
    t, d = x2d.shape
    n = w_bf16.shape[1]
    n_j = n // tn
    cast_in, cast_out, cast_shapes = _cast_specs(weights, (t // tm) * n_j, lambda i, j: i * n_j + j)
    return pl.pallas_call(
        functools.partial(_inproj_kernel, len(weights)),
        grid=(t // tm, n_j),
        in_specs=[
            pl.BlockSpec((tm, d), lambda i, j: (i, 0)),
            pl.BlockSpec((1, d), lambda i, j: (0, 0)),
            pl.BlockSpec((d, tn), lambda i, j: (0, j)),
            pl.BlockSpec((1, tn), lambda i, j: (0, j)),
        ] + cast_in,
        out_specs=[pl.BlockSpec((tm, tn), lambda i, j: (i, j))] + cast_out,
        out_shape=[jax.ShapeDtypeStruct((t, n), BF16)] + cast_shapes,
        scratch_shapes=[pltpu.VMEM((tm, d), BF16)],
        compiler_params=_params("arbitrary", "arbitrary"),
        name="inproj",
    )(x2d, gain, w_bf16, col_scale, *[w for w, _ in weights])


QB = 256
N_SEG = 1 + (N_LEFT * CHUNK) // QB
WIN = N_SEG * QB
ROLL_W = 1024


def _bias_rows(rel_bias):
    h = rel_bias.shape[0]
    far_past = rel_bias[:, 2 * MAX_REL:2 * MAX_REL + 1]
    far_future = rel_bias[:, 0:1]
    n_head = (WIN - QB) - MAX_REL
    n_tail = WIN - n_head - (2 * MAX_REL + 1)
    return jnp.concatenate([
        jnp.broadcast_to(far_past, (h, n_head)),
        rel_bias[:, ::-1],
        jnp.broadcast_to(far_future, (h, n_tail)),
        jnp.broadcast_to(far_past, (h, ROLL_W - WIN)),
    ], axis=1).astype(F32)


def _lane_halves(op, x):
    while x.shape[1] > HEAD_DIM:
        half = x.shape[1] // 2
        x = op(x[:, :half], x[:, half:])
    return x


TB = 512


def _silu(x):
    hx = 0.5 * x
    return hx + hx * jnp.tanh(hx)


def _split3(x):
    hi = x.astype(BF16)
    r1 = x - hi.astype(F32)
    mid = r1.astype(BF16)
    lo = (r1 - mid.astype(F32)).astype(BF16)
    return hi, mid, lo


def _cast_slabs(cast_in, cast_out):
    for src, dst in zip(cast_in, cast_out):
        if len(dst.shape) == 2:
            dst[...] = src[...].astype(dst.dtype)
        else:
            width = dst.shape[2]
            for n in range(dst.shape[0]):
                dst[n] = src[:, n * width:(n + 1) * width].astype(dst.dtype)


def _cast_specs(weights, steps, step_index):
    in_specs, out_specs, out_shapes = [], [], []
    for w, tile in weights:
        assert w.shape[0] % (steps * BF16_SUBLANES) == 0, (w.shape, steps)
        rows = w.shape[0] // steps
        in_specs.append(pl.BlockSpec((rows, w.shape[1]), lambda *g: (step_index(*g), 0)))
        if tile is None:
            out_specs.append(pl.BlockSpec((rows, w.shape[1]), lambda *g: (step_index(*g), 0)))
            out_shapes.append(jax.ShapeDtypeStruct(w.shape, BF16))
        else:
            out_specs.append(pl.BlockSpec((w.shape[1] // tile, rows, tile),
                                          lambda *g: (0, step_index(*g), 0)))
            out_shapes.append(jax.ShapeDtypeStruct((w.shape[1] // tile, w.shape[0], tile), BF16))
    return in_specs, out_specs, out_shapes


def _mixers_kernel(n_cast, aq_ref, kp_ref, kc_ref, vp_ref, vc_ref, brow_ref, again_ref,
                   q_ref, f_ref, i_ref, g_ref, lb_ref, gn_ref, *refs):
    cast_in = refs[:n_cast]
    attn_ref, o_ref, *cast_out = refs[n_cast:2 * n_cast + 2]
    bias_ref, acc_ref, s_ref, st_ref, qd_ref, a_ref, u_ref, dec_ref = refs[2 * n_cast + 2:]
    step = pl.program_id(1)

    @pl.when((pl.program_id(0) == 0) & (step == 0))
    def _():
        qc = lax.broadcasted_iota(jnp.int32, (QB, WIN), 0) // CHUNK
        kc = lax.broadcasted_iota(jnp.int32, (QB, WIN), 1) // CHUNK
        band = (kc >= qc) & (kc <= qc + N_LEFT)
        for h in range(HEADS):
            row = jnp.broadcast_to(brow_ref[h:h + 1, :] * LOG2E, (QB, ROLL_W))
            toeplitz = pltpu.roll(row, 0, 1, stride=1, stride_axis=0)
            bias_ref[h] = jnp.where(band, toeplitz[:, :WIN], NEG)

    @pl.when(step == 0)
    def _():
        st_ref[...] = jnp.zeros_like(st_ref)

    ones = jnp.ones((QB, HEAD_DIM), BF16)

    def granule(prev_ref, cur_ref, r, sl):
        ref = (prev_ref, cur_ref)[r // (TB // QB)]
        start = (r % (TB // QB)) * QB
        return ref[start:start + QB, sl]

    def attention_items(blk, segs):
        rows = slice(blk * QB, (blk + 1) * QB)
        m = [None] * HEADS
        sq = [jnp.zeros((QB, HEAD_DIM), F32)]

        def scores(h):
            sl = slice(h * HEAD_DIM, (h + 1) * HEAD_DIM)
            qh = aq_ref[rows, sl]
            mx = None
            for j in segs:
                cols = slice(j * QB, (j + 1) * QB)
                sj = lax.dot_general(qh, granule(kp_ref, kc_ref, j + blk, sl),
                                     (((1,), (1,)), ((), ())),
                                     preferred_element_type=F32) + bias_ref[h, :, cols]
                s_ref[h, :, cols] = sj
                mx = sj if mx is None else jnp.maximum(mx, sj)
            m[h] = _lane_halves(jnp.maximum, mx).max(axis=-1, keepdims=True)

        def values(h):
            sl = slice(h * HEAD_DIM, (h + 1) * HEAD_DIM)
            p = jnp.concatenate(
                [jnp.exp2(s_ref[h, :, j * QB:(j + 1) * QB] - m[h]).astype(BF16) for j in segs], axis=1)
            v1 = jnp.concatenate(
                [jnp.concatenate([granule(vp_ref, vc_ref, j + blk, sl), ones], axis=1) for j in segs],
                axis=0)
            pv = jnp.dot(p, v1, preferred_element_type=F32)
            o = pv[:, :HEAD_DIM] * (1.0 / pv[:, HEAD_DIM:])
            acc_ref[rows, sl] = o
            sq[0] = sq[0] + o * o

        def finish():
            inv = lax.rsqrt(sq[0].sum(axis=-1, keepdims=True) / GROUP + EPS)
            attn_ref[rows, :] = (acc_ref[rows, :] * inv * again_ref[...]).astype(attn_ref.dtype)

        return ([functools.partial(scores, h) for h in range(HEADS)]
                + [functools.partial(values, h) for h in range(HEADS)] + [finish])

    lb = lb_ref[...]
    f_mid = 0.5 * (1.0 + lb)
    f_amp = 0.5 * (1.0 - lb)
    gn = gn_ref[...]
    r_i = lax.broadcasted_iota(jnp.int32, (CHUNK, CHUNK), 0)
    c_i = lax.broadcasted_iota(jnp.int32, (CHUNK, CHUNK), 1)
    causal = r_i >= c_i
    tril = causal.astype(BF16)
    tril3 = jnp.concatenate([tril, tril, tril], axis=1)

    def chunk_rows(c):
        return slice(c * CHUNK, (c + 1) * CHUNK)

    def local_terms(c):
        rows = chunk_rows(c)
        q = _silu(q_ref[rows, :].astype(F32))
        f = f_mid + f_amp * jnp.tanh(0.5 * f_ref[rows, :].astype(F32))
        k = 1.0 - f
        b = jnp.dot(tril3, jnp.concatenate(_split3(jnp.log2(f)), axis=0),
                    preferred_element_type=F32)
        decay = jnp.exp2(b[CHUNK - 1:CHUNK, :])
        q_dec = (q * jnp.exp2(b)).astype(BF16)
        k_undecayed = k * jnp.exp2(-b)
        k_intra = k_undecayed.astype(BF16)
        k_state = (k_undecayed * decay).astype(BF16)
        qd_ref[rows, :] = q_dec
        dec_ref[c] = decay
        for h in range(HEADS):
            sl = slice(h * HEAD_DIM, (h + 1) * HEAD_DIM)
            a = lax.dot_general(q_dec[:, sl], k_intra[:, sl], (((1,), (1,)), ((), ())),
                                preferred_element_type=F32)
            a_ref[h, rows, :] = jnp.where(causal, a, 0.0).astype(BF16)
            u_ref[c, h] = lax.dot_general(i_ref[rows, sl], k_state[:, sl], (((0,), (0,)), ((), ())),
                                          preferred_element_type=F32)

    def outputs(c):
        rows = chunk_rows(c)
        decay = dec_ref[c]
        for h in range(HEADS):
            sl = slice(h * HEAD_DIM, (h + 1) * HEAD_DIM)
            st = st_ref[h]
            o = jnp.dot(a_ref[h, rows, :], i_ref[rows, sl], preferred_element_type=F32)
            o = o + lax.dot_general(qd_ref[rows, sl], st.astype(BF16), (((1,), (1,)), ((), ())),
                                    preferred_element_type=F32)
            st_ref[h] = st * decay[:, sl] + u_ref[c, h]
            o = _rms_rows(o, gn)
            o_ref[rows, sl] = (o * _silu(g_ref[rows, sl].astype(F32))).astype(o_ref.dtype)

    def run_step(first):
        n_blocks = TB // QB
        n_chunks = TB // CHUNK
        recurrence = ([functools.partial(local_terms, c) for c in range(n_chunks)]
                      + [functools.partial(outputs, c) for c in range(n_chunks)])
        attention = []
        for blk in range(n_blocks):
            first_seg = max(n_blocks - blk, 0) if first else 0
            attention += attention_items(blk, tuple(range(first_seg, N_SEG)))
        _cast_slabs(cast_in, cast_out)
        done = 0
        for n, item in enumerate(recurrence):
            item()
            upto = (n + 1) * len(attention) // len(recurrence)
            for other in attention[done:upto]:
                other()
            done = upto

    @pl.when(step == 0)
    def _():
        run_step(True)

    @pl.when(step > 0)
    def _():
        run_step(False)


def _mixers(proj, brow, attn_gain, lb, gn, batch, seq, weights):
    t = proj.shape[0]
    nb = seq // TB
    assert seq % TB == 0 and TB % QB == 0 and (N_SEG - 1) * QB <= TB

    def rows(col_block, back=0):
        return pl.BlockSpec((TB, GROUP),
                            lambda b, i: (b * nb + jnp.maximum(i - back, 0), col_block))

    def const(shape):
        return pl.BlockSpec(shape, lambda b, i: (0, 0))

    cast_in, cast_out, cast_shapes = _cast_specs(weights, batch * nb, lambda b, i: b * nb + i)
    return pl.pallas_call(
        functools.partial(_mixers_kernel, len(weights)),
        grid=(batch, nb),
        in_specs=[rows(0), rows(1, back=1), rows(1), rows(2, back=1), rows(2),
                  const((HEADS, ROLL_W)), const((1, GROUP)),
                  rows(3), rows(4), rows(5), rows(6), const((1, GROUP)), const((1, HEAD_DIM))]
        + cast_in,
        out_specs=[rows(0), rows(0)] + cast_out,
        out_shape=[jax.ShapeDtypeStruct((t, GROUP), BF16)] * 2 + cast_shapes,
        scratch_shapes=[
            pltpu.VMEM((HEADS, QB, WIN), F32),
            pltpu.VMEM((TB, GROUP), F32),
            pltpu.VMEM((HEADS, QB, WIN), F32),
            pltpu.VMEM((HEADS, HEAD_DIM, HEAD_DIM), F32),
            pltpu.VMEM((TB, GROUP), BF16),
            pltpu.VMEM((HEADS, TB, CHUNK), BF16),
            pltpu.VMEM((TB // CHUNK, HEADS, HEAD_DIM, HEAD_DIM), F32),
            pltpu.VMEM((TB // CHUNK, 1, GROUP), F32),
        ],
        compiler_params=_params("arbitrary", "arbitrary"),
        name="mixers",
    )(*([proj] * 5), brow, attn_gain, *([proj] * 4), lb, gn, *[w for w, _ in weights])


OUT_COLS = 512


def _outproj_kernel(x_ref, a_ref, r_ref, w_ref, o_ref):
    mixed = jnp.concatenate([a_ref[...], r_ref[...]], axis=1)
    for n in range(0, o_ref.shape[1], OUT_COLS):
        cols = slice(n, n + OUT_COLS)
        o_ref[:, cols] = x_ref[:, cols] + jnp.dot(mixed, w_ref[:, cols],
                                                  preferred_element_type=F32)


def _outproj(x2d, attn, rec, w_bf16, tm):
    t, d = x2d.shape
    g = attn.shape[1]
    return pl.pallas_call(
        _outproj_kernel,
        grid=(t // tm,),
        in_specs=[
            pl.BlockSpec((tm, d), lambda i: (i, 0)),
            pl.BlockSpec((tm, g), lambda i: (i, 0)),
            pl.BlockSpec((tm, g), lambda i: (i, 0)),
            pl.BlockSpec((2 * g, d), lambda i: (0, 0), pipeline_mode=pl.Buffered(1)),
        ],
        out_specs=pl.BlockSpec((tm, d), lambda i: (i, 0)),
        out_shape=jax.ShapeDtypeStruct((t, d), F32),
        compiler_params=_params("arbitrary"),
        name="outproj",
    )(x2d, attn, rec, w_bf16)


FFN_DOWN_COLS = 512
FFN_HIDDEN_TILE = 512


def _ffn_kernel(x_hbm, g2_ref, wg_hbm, wu_hbm, wd_hbm, gf_ref, out_hbm,
                acc_ref, h_ref, wg_buf, wu_buf, wd_buf, x_sem, o_sem, w_sem):
    _, tm, d = acc_ref.shape
    n_tiles, _, tf = wg_hbm.shape
    n_row_tiles = x_hbm.shape[0] // tm
    n_total = n_row_tiles * n_tiles
    col_chunks = [slice(n, n + FFN_DOWN_COLS) for n in range(0, d, FFN_DOWN_COLS)]

    def x_copy(i, slot):
        return pltpu.make_async_copy(x_hbm.at[pl.ds(i * tm, tm), :], acc_ref.at[slot],
                                     x_sem.at[slot])

    def out_copy(i, slot):
        return pltpu.make_async_copy(acc_ref.at[slot], out_hbm.at[pl.ds(i * tm, tm), :],
                                     o_sem.at[slot])

    def w_copies(g):
        j = lax.rem(g, n_tiles)
        slot = lax.rem(g, 2)
        return (pltpu.make_async_copy(wg_hbm.at[j], wg_buf.at[slot], w_sem.at[0, slot]),
                pltpu.make_async_copy(wu_hbm.at[j], wu_buf.at[slot], w_sem.at[1, slot]),
                pltpu.make_async_copy(wd_hbm.at[pl.ds(j * tf, tf), :], wd_buf.at[slot],
                                      w_sem.at[2, slot]))

    def weights(g):
        for c in w_copies(g):
            c.wait()

        @pl.when(g + 1 < n_total)
        def _():
            for c in w_copies(g + 1):
                c.start()

        slot = lax.rem(g, 2)
        return wg_buf.at[slot], wu_buf.at[slot], wd_buf.at[slot]

    def swiglu_down(h, w):
        wg, wu, wd = w
        gate = jnp.dot(h, wg[...], preferred_element_type=F32)
        up = jnp.dot(h, wu[...], preferred_element_type=F32)
        act = (gate * jax.nn.sigmoid(gate) * up).astype(BF16)
        return [jnp.dot(act, wd[:, cols], preferred_element_type=F32) for cols in col_chunks]

    def row_tile(i, carry):
        slot = lax.rem(i, 2)
        acc = acc_ref.at[slot]
        g0 = i * n_tiles
        x_copy(i, slot).wait()

        w = weights(g0)
        g2 = g2_ref[...]
        for rows in _row_slabs(tm):
            h = _rms_rows(acc[rows, :], g2).astype(h_ref.dtype)
            h_ref[rows, :] = h
            for cols, y in zip(col_chunks, swiglu_down(h, w)):
                acc[rows, cols] += y

        @pl.when(i >= 1)
        def _():
            out_copy(i - 1, 1 - slot).wait()

        @pl.when(i + 1 < n_row_tiles)
        def _():
            x_copy(i + 1, 1 - slot).start()

        def hidden_tile(j, c):
            w = weights(g0 + j)
            for cols, y in zip(col_chunks, swiglu_down(h_ref[...], w)):
                acc[:, cols] += y
            return c

        lax.fori_loop(1, n_tiles - 1, hidden_tile, 0)

        w = weights(g0 + n_tiles - 1)
        gf = gf_ref[...]
        for rows in _row_slabs(tm):
            ss = jnp.zeros((EDGE_ROWS, 1), F32)
            for cols, y in zip(col_chunks, swiglu_down(h_ref[rows, :], w)):
                o = acc[rows, cols] + y
                acc[rows, cols] = o
                ss = ss + (o * o).sum(axis=-1, keepdims=True)
            acc[rows, :] = acc[rows, :] * lax.rsqrt(ss / d + EPS) * gf
        out_copy(i, slot).start()
        return carry

    x_copy(0, 0).start()
    for c in w_copies(0):
        c.start()
    lax.fori_loop(0, n_row_tiles, row_tile, 0)
    out_copy(n_row_tiles - 1, (n_row_tiles - 1) % 2).wait()


def _ffn(x1, g2, wg_tiles, wu_tiles, wd, gf, tm):
    t, d = x1.shape
    n_tiles, _, tf = wg_tiles.shape
    assert n_tiles >= 2, "the first and the last hidden tile are separate code paths"
    assert t % tm == 0 and tm % EDGE_ROWS == 0
    hbm = pl.BlockSpec(memory_space=pl.ANY)
    whole = pl.BlockSpec(memory_space=pltpu.VMEM)
    return pl.pallas_call(
        _ffn_kernel,
        in_specs=[hbm, whole, hbm, hbm, hbm, whole],
        out_specs=hbm,
        out_shape=jax.ShapeDtypeStruct((t, d), F32),
        scratch_shapes=[
            pltpu.VMEM((2, tm, d), F32),
            pltpu.VMEM((tm, d), BF16),
            pltpu.VMEM((2, d, tf), BF16),
            pltpu.VMEM((2, d, tf), BF16),
            pltpu.VMEM((2, tf, d), BF16),
            pltpu.SemaphoreType.DMA((2,)),
            pltpu.SemaphoreType.DMA((2,)),
            pltpu.SemaphoreType.DMA((3, 2)),
        ],
        compiler_params=pltpu.CompilerParams(vmem_limit_bytes=VMEM_LIMIT_BYTES),
        name="ffn",
    )(x1, g2, wg_tiles, wu_tiles, wd, gf)


def kernel(x, norm1_gain, w_in, rel_bias, lower_bounds, grn_norm_gain, attn_out_gain, w_out,
           norm2_gain, w_gate, w_up, w_down, final_gain):
    batch, seq, d = x.shape
    depth = w_in.shape[0]
    assert seq % TB == 0 and seq % QB == 0 and w_in.shape[2] == 7 * GROUP and d == 2 * GROUP
    lb_all = jnp.cumsum(jax.nn.softmax(lower_bounds.astype(F32), axis=0), axis=0)
    xf = x.reshape(batch * seq, d)
    col_scale = jnp.concatenate([jnp.full((1, GROUP), QUERY_SCALE, F32),
                                 jnp.ones((1, w_in.shape[2] - GROUP), F32)], axis=1)
    for l in range(depth):
        proj, wo, wg, wu = _inproj(xf, norm1_gain[l][None, :], w_in[l].astype(BF16), col_scale,
                                   tm=1024, tn=1792,
                                   weights=[(w_out[l], None), (w_gate[l], FFN_HIDDEN_TILE),
                                            (w_up[l], FFN_HIDDEN_TILE)])
        attn, rec, wd = _mixers(proj, _bias_rows(rel_bias[l]), attn_out_gain[l][None, :],
                                lb_all[l][None, :], grn_norm_gain[l][None, :], batch, seq,
                                [(w_down[l], None)])
        x1 = _outproj(xf, attn, rec, wo, tm=1024)
        last = l == depth - 1
        assert last, "the final rmsnorm is fused into the last layer's FFN"
        xf = _ffn(x1, norm2_gain[l][None, :], wg, wu, wd, final_gain[None, :], tm=1024)
    return xf.reshape(batch, seq, d)
```

```python
import functools

import jax
import jax.numpy as jnp
from jax import lax
from jax.experimental import pallas as pl
from jax.experimental.pallas import tpu as pltpu

F32 = jnp.float32
BF16 = jnp.bfloat16

EPS = 1e-6
CHUNK = 64
N_LEFT = 8
MAX_REL = 128
HEADS = 8
HEAD_DIM = 128
GROUP = HEADS * HEAD_DIM

VMEM_LIMIT_BYTES = 56 * 1024 * 1024
BF16_SUBLANES = 16
NEG = -1e30
LOG2E = 1.4426950408889634
QUERY_SCALE = HEAD_DIM ** -0.5 * LOG2E


def _params(*sem):
    return pltpu.CompilerParams(dimension_semantics=sem, vmem_limit_bytes=VMEM_LIMIT_BYTES)


def _rms_rows(x, gain):
    ms = jnp.mean(x * x, axis=-1, keepdims=True)
    return x * lax.rsqrt(ms + EPS) * gain


EDGE_ROWS = 256


def _row_slabs(n_rows):
    return [slice(r, r + EDGE_ROWS) for r in range(0, n_rows, EDGE_ROWS)]


def _inproj_kernel(x_ref, gain_ref, w_ref, cs_ref, o_ref, h_ref):
    j = pl.program_id(1)
    col_scale = cs_ref[...]

    def project(h):
        return (jnp.dot(h, w_ref[...], preferred_element_type=F32) * col_scale).astype(o_ref.dtype)

    @pl.when(j == 0)
    def _():
        gain = gain_ref[...]
        for rows in _row_slabs(x_ref.shape[0]):
            h = _rms_rows(x_ref[rows, :], gain).astype(h_ref.dtype)
            h_ref[rows, :] = h
            o_ref[rows, :] = project(h)

    @pl.when(j > 0)
    def _():
        o_ref[...] = project(h_ref[...])


def _inproj(x2d, gain, w_bf16, col_scale, tm, tn):
    t, d = x2d.shape
    n = w_bf16.shape[1]
    return pl.pallas_call(
        _inproj_kernel,
        grid=(t // tm, n // tn),
        in_specs=[
            pl.BlockSpec((tm, d), lambda i, j: (i, 0)),
            pl.BlockSpec((1, d), lambda i, j: (0, 0)),
            pl.BlockSpec((d, tn), lambda i, j: (0, j)),
            pl.BlockSpec((1, tn), lambda i, j: (0, j)),
        ],
        out_specs=pl.BlockSpec((tm, tn), lambda i, j: (i, j)),
        out_shape=jax.ShapeDtypeStruct((t, n), BF16),
        scratch_shapes=[pltpu.VMEM((tm, d), BF16)],
        compiler_params=_params("arbitrary", "arbitrary"),
        name="inproj",
    )(x2d, gain, w_bf16, col_scale)


QB = 256
N_SEG = 1 + (N_LEFT * CHUNK) // QB
WIN = N_SEG * QB
ROLL_W = 1024


def _bias_rows(rel_bias):
    h = rel_bias.shape[0]
    far_past = rel_bias[:, 2 * MAX_REL:2 * MAX_REL + 1]
    far_future = rel_bias[:, 0:1]
    n_head = (WIN - QB) - MAX_REL
    n_tail = WIN - n_head - (2 * MAX_REL + 1)
    return jnp.concatenate([
        jnp.broadcast_to(far_past, (h, n_head)),
        rel_bias[:, ::-1],
        jnp.broadcast_to(far_future, (h, n_tail)),
        jnp.broadcast_to(far_past, (h, ROLL_W - WIN)),
    ], axis=1).astype(F32)


def _attn_kernel(q_ref, k0_ref, k1_ref, k2_ref, v0_ref, v1_ref, v2_ref, brow_ref, gain_ref,
                 o_ref, bias_ref, acc_ref, s_ref):
    qb = pl.program_id(1)

    @pl.when((pl.program_id(0) == 0) & (qb == 0))
    def _():
        qc = lax.broadcasted_iota(jnp.int32, (QB, WIN), 0) // CHUNK
        kc = lax.broadcasted_iota(jnp.int32, (QB, WIN), 1) // CHUNK
        band = (kc >= qc) & (kc <= qc + N_LEFT)
        for h in range(HEADS):
            row = jnp.broadcast_to(brow_ref[h:h + 1, :] * LOG2E, (QB, ROLL_W))
            toeplitz = pltpu.roll(row, 0, 1, stride=1, stride_axis=0)
            bias_ref[h] = jnp.where(band, toeplitz[:, :WIN], NEG)

    k_refs = (k0_ref, k1_ref, k2_ref)
    v_refs = (v0_ref, v1_ref, v2_ref)

    def lane_halves(op, x):
        while x.shape[1] > HEAD_DIM:
            half = x.shape[1] // 2
            x = op(x[:, :half], x[:, half:])
        return x

    def attend(segs):
        m = []
        for h in range(HEADS):
            sl = slice(h * HEAD_DIM, (h + 1) * HEAD_DIM)
            qh = q_ref[:, sl]
            mx = None
            for j in segs:
                cols = slice(j * QB, (j + 1) * QB)
                sj = lax.dot_general(qh, k_refs[j][:, sl], (((1,), (1,)), ((), ())),
                                     preferred_element_type=F32) + bias_ref[h, :, cols]
                s_ref[h, :, cols] = sj
                mx = sj if mx is None else jnp.maximum(mx, sj)
            m.append(lane_halves(jnp.maximum, mx).max(axis=-1, keepdims=True))
        sq = jnp.zeros((QB, HEAD_DIM), F32)
        ones = jnp.ones((QB, HEAD_DIM), BF16)
        for h in range(HEADS):
            sl = slice(h * HEAD_DIM, (h + 1) * HEAD_DIM)
            p = jnp.concatenate(
                [jnp.exp2(s_ref[h, :, j * QB:(j + 1) * QB] - m[h]).astype(BF16) for j in segs], axis=1)
            v1 = jnp.concatenate(
                [jnp.concatenate([v_refs[j][:, sl], ones], axis=1) for j in segs], axis=0)
            pv = jnp.dot(p, v1, preferred_element_type=F32)
            o = pv[:, :HEAD_DIM] * (1.0 / pv[:, HEAD_DIM:])
            acc_ref[:, sl] = o
            sq = sq + o * o
        inv = lax.rsqrt(sq.sum(axis=-1, keepdims=True) / GROUP + EPS)
        o_ref[...] = (acc_ref[...] * inv * gain_ref[...]).astype(o_ref.dtype)

    for n_valid in range(1, N_SEG + 1):
        @pl.when(jnp.minimum(qb, N_SEG - 1) == n_valid - 1)
        def _():
            attend(tuple(range(N_SEG - n_valid, N_SEG)))


def _attention(proj, brow, gain, batch, seq):
    t = proj.shape[0]
    nqb = seq // QB

    def kv_spec(col_block, seg):
        def index(b, i):
            return (b * nqb + jnp.maximum(i - (N_SEG - 1 - seg), 0), col_block)
        return pl.BlockSpec((QB, GROUP), index)

    return pl.pallas_call(
        _attn_kernel,
        grid=(batch, nqb),
        in_specs=[pl.BlockSpec((QB, GROUP), lambda b, i: (b * nqb + i, 0))]
        + [kv_spec(1, s) for s in range(N_SEG)]
        + [kv_spec(2, s) for s in range(N_SEG)]
        + [pl.BlockSpec((HEADS, ROLL_W), lambda b, i: (0, 0)),
           pl.BlockSpec((1, GROUP), lambda b, i: (0, 0))],
        out_specs=pl.BlockSpec((QB, GROUP), lambda b, i: (b * nqb + i, 0)),
        out_shape=jax.ShapeDtypeStruct((t, GROUP), BF16),
        scratch_shapes=[pltpu.VMEM((HEADS, QB, WIN), F32),
                        pltpu.VMEM((QB, GROUP), F32),
                        pltpu.VMEM((HEADS, QB, WIN), F32)],
        compiler_params=_params("arbitrary", "arbitrary"),
        name="attention",
    )(proj, proj, proj, proj, proj, proj, proj, brow, gain)


TB = 512


def _silu(x):
    hx = 0.5 * x
    return hx + hx * jnp.tanh(hx)


def _split3(x):
    hi = x.astype(BF16)
    r1 = x - hi.astype(F32)
    mid = r1.astype(BF16)
    lo = (r1 - mid.astype(F32)).astype(BF16)
    return hi, mid, lo


def _hgrn_kernel(n_cast, q_ref, f_ref, i_ref, g_ref, lb_ref, gn_ref, *refs):
    cast_in, (o_ref, *cast_out) = refs[:n_cast], refs[n_cast:2 * n_cast + 1]
    st_ref, qd_ref, a_ref, u_ref, dec_ref = refs[2 * n_cast + 1:]

    for src, dst in zip(cast_in, cast_out):
        if len(dst.shape) == 2:
            dst[...] = src[...].astype(dst.dtype)
        else:
            width = dst.shape[2]
            for n in range(dst.shape[0]):
                dst[n] = src[:, n * width:(n + 1) * width].astype(dst.dtype)

    @pl.when(pl.program_id(1) == 0)
    def _():
        st_ref[...] = jnp.zeros_like(st_ref)

    lb = lb_ref[...]
    f_mid = 0.5 * (1.0 + lb)
    f_amp = 0.5 * (1.0 - lb)
    gn = gn_ref[...]
    r_i = lax.broadcasted_iota(jnp.int32, (CHUNK, CHUNK), 0)
    c_i = lax.broadcasted_iota(jnp.int32, (CHUNK, CHUNK), 1)
    causal = r_i >= c_i
    tril = causal.astype(BF16)
    tril3 = jnp.concatenate([tril, tril, tril], axis=1)

    def chunk_rows(c):
        return slice(c * CHUNK, (c + 1) * CHUNK)

    def local_terms(c):
        rows = chunk_rows(c)
        q = _silu(q_ref[rows, :].astype(F32))
        f = f_mid + f_amp * jnp.tanh(0.5 * f_ref[rows, :].astype(F32))
        k = 1.0 - f
        b = jnp.dot(tril3, jnp.concatenate(_split3(jnp.log2(f)), axis=0),
                    preferred_element_type=F32)
        decay = jnp.exp2(b[CHUNK - 1:CHUNK, :])
        q_dec = (q * jnp.exp2(b)).astype(BF16)
        k_undecayed = k * jnp.exp2(-b)
        k_intra = k_undecayed.astype(BF16)
        k_state = (k_undecayed * decay).astype(BF16)
        qd_ref[rows, :] = q_dec
        dec_ref[c] = decay
        for h in range(HEADS):
            sl = slice(h * HEAD_DIM, (h + 1) * HEAD_DIM)
            a = lax.dot_general(q_dec[:, sl], k_intra[:, sl], (((1,), (1,)), ((), ())),
                                preferred_element_type=F32)
            a_ref[h, rows, :] = jnp.where(causal, a, 0.0).astype(BF16)
            u_ref[c, h] = lax.dot_general(i_ref[rows, sl], k_state[:, sl], (((0,), (0,)), ((), ())),
                                          preferred_element_type=F32)

    def outputs(c):
        rows = chunk_rows(c)
        decay = dec_ref[c]
        for h in range(HEADS):
            sl = slice(h * HEAD_DIM, (h + 1) * HEAD_DIM)
            st = st_ref[h]
            o = jnp.dot(a_ref[h, rows, :], i_ref[rows, sl], preferred_element_type=F32)
            o = o + lax.dot_general(qd_ref[rows, sl], st.astype(BF16), (((1,), (1,)), ((), ())),
                                    preferred_element_type=F32)
            st_ref[h] = st * decay[:, sl] + u_ref[c, h]
            o = _rms_rows(o, gn)
            o_ref[rows, sl] = (o * _silu(g_ref[rows, sl].astype(F32))).astype(o_ref.dtype)

    for c in range(TB // CHUNK):
        local_terms(c)
    for c in range(TB // CHUNK):
        outputs(c)


def _hgrn(proj, lb, gn, batch, seq, weights):
    t = proj.shape[0]
    nb = seq // TB
    steps = batch * nb
    for w, _ in weights:
        assert w.shape[0] % (steps * BF16_SUBLANES) == 0, w.shape

    def col(cb):
        return pl.BlockSpec((TB, GROUP), lambda b, i: (b * nb + i, cb))

    def slab(w):
        return pl.BlockSpec((w.shape[0] // steps, w.shape[1]), lambda b, i: (b * nb + i, 0))

    def cast_shape(w, tile):
        return w.shape if tile is None else (w.shape[1] // tile, w.shape[0], tile)

    def cast_slab(w, tile):
        if tile is None:
            return slab(w)
        return pl.BlockSpec((w.shape[1] // tile, w.shape[0] // steps, tile),
                            lambda b, i: (0, b * nb + i, 0))

    return pl.pallas_call(
        functools.partial(_hgrn_kernel, len(weights)),
        grid=(batch, nb),
        in_specs=[col(3), col(4), col(5), col(6),
                  pl.BlockSpec((1, GROUP), lambda b, i: (0, 0)),
                  pl.BlockSpec((1, HEAD_DIM), lambda b, i: (0, 0))]
        + [slab(w) for w, _ in weights],
        out_specs=[pl.BlockSpec((TB, GROUP), lambda b, i: (b * nb + i, 0))]
        + [cast_slab(w, tile) for w, tile in weights],
        out_shape=[jax.ShapeDtypeStruct((t, GROUP), BF16)]
        + [jax.ShapeDtypeStruct(cast_shape(w, tile), BF16) for w, tile in weights],
        scratch_shapes=[
            pltpu.VMEM((HEADS, HEAD_DIM, HEAD_DIM), F32),
            pltpu.VMEM((TB, GROUP), BF16),
            pltpu.VMEM((HEADS, TB, CHUNK), BF16),
            pltpu.VMEM((TB // CHUNK, HEADS, HEAD_DIM, HEAD_DIM), F32),
            pltpu.VMEM((TB // CHUNK, 1, GROUP), F32),
        ],
        compiler_params=_params("arbitrary", "arbitrary"),
        name="hgrn2",
    )(proj, proj, proj, proj, lb, gn, *[w for w, _ in weights])


OUT_COLS = 512


def _outproj_kernel(x_ref, a_ref, r_ref, w_ref, o_ref):
    mixed = jnp.concatenate([a_ref[...], r_ref[...]], axis=1)
    for n in range(0, o_ref.shape[1], OUT_COLS):
        cols = slice(n, n + OUT_COLS)
        o_ref[:, cols] = x_ref[:, cols] + jnp.dot(mixed, w_ref[:, cols],
                                                  preferred_element_type=F32)


def _outproj(x2d, attn, rec, w_bf16, tm):
    t, d = x2d.shape
    g = attn.shape[1]
    return pl.pallas_call(
        _outproj_kernel,
        grid=(t // tm,),
        in_specs=[
            pl.BlockSpec((tm, d), lambda i: (i, 0)),
            pl.BlockSpec((tm, g), lambda i: (i, 0)),
            pl.BlockSpec((tm, g), lambda i: (i, 0)),
            pl.BlockSpec((2 * g, d), lambda i: (0, 0), pipeline_mode=pl.Buffered(1)),
        ],
        out_specs=pl.BlockSpec((tm, d), lambda i: (i, 0)),
        out_shape=jax.ShapeDtypeStruct((t, d), F32),
        compiler_params=_params("arbitrary"),
        name="outproj",
    )(x2d, attn, rec, w_bf16)


FFN_DOWN_COLS = 512
FFN_HIDDEN_TILE = 512


def _ffn_kernel(x_hbm, g2_ref, wg_hbm, wu_hbm, wd_hbm, gf_ref, out_hbm,
                acc_ref, h_ref, wg_buf, wu_buf, wd_buf, x_sem, o_sem, w_sem):
    _, tm, d = acc_ref.shape
    n_tiles, _, tf = wg_hbm.shape
    n_groups = 1 + (n_tiles - 1) // 2
    n_row_tiles = x_hbm.shape[0] // tm
    col_chunks = [slice(n, n + FFN_DOWN_COLS) for n in range(0, d, FFN_DOWN_COLS)]

    def x_copy(i, slot):
        return pltpu.make_async_copy(x_hbm.at[pl.ds(i * tm, tm), :], acc_ref.at[slot],
                                     x_sem.at[slot])

    def out_copy(i, slot):
        return pltpu.make_async_copy(acc_ref.at[slot], out_hbm.at[pl.ds(i * tm, tm), :],
                                     o_sem.at[slot])

    def group_tiles(k):
        return (0,) if isinstance(k, int) and k == 0 else (2 * k - 1, 2 * k)

    def group_copies(k, slot):
        copies = []
        for p, j in enumerate(group_tiles(k)):
            copies += [
                pltpu.make_async_copy(wg_hbm.at[j], wg_buf.at[slot, p], w_sem.at[0, slot, p]),
                pltpu.make_async_copy(wu_hbm.at[j], wu_buf.at[slot, p], w_sem.at[1, slot, p]),
                pltpu.make_async_copy(wd_hbm.at[pl.ds(j * tf, tf), :],
                                      wd_buf.at[slot, pl.ds(p * tf, tf), :], w_sem.at[2, slot, p])]
        return copies

    def swiglu_down(h, slot, n):
        acts = []
        for p in range(n):
            gate = jnp.dot(h, wg_buf[slot, p], preferred_element_type=F32)
            up = jnp.dot(h, wu_buf[slot, p], preferred_element_type=F32)
            acts.append((gate * jax.nn.sigmoid(gate) * up).astype(BF16))
        act = acts[0] if n == 1 else jnp.concatenate(acts, axis=1)
        wd = wd_buf.at[slot]
        return [jnp.dot(act, wd[:n * tf, cols], preferred_element_type=F32) for cols in col_chunks]

    def row_tile(i, carry):
        slot = lax.rem(i, 2)
        acc = acc_ref.at[slot]
        x_copy(i, slot).wait()

        for c in group_copies(0, 0):
            c.wait()
        for c in group_copies(1, 1):
            c.start()
        g2 = g2_ref[...]
        for rows in _row_slabs(tm):
            h = _rms_rows(acc[rows, :], g2).astype(h_ref.dtype)
            h_ref[rows, :] = h
            for cols, y in zip(col_chunks, swiglu_down(h, 0, 1)):
                acc[rows, cols] += y

        @pl.when(i >= 1)
        def _():
            out_copy(i - 1, 1 - slot).wait()

        @pl.when(i + 1 < n_row_tiles)
        def _():
            x_copy(i + 1, 1 - slot).start()

        def pair_group(k, c):
            w_slot = lax.rem(k, 2)
            for cp in group_copies(k, w_slot):
                cp.wait()
            for cp in group_copies(k + 1, 1 - w_slot):
                cp.start()
            for cols, y in zip(col_chunks, swiglu_down(h_ref[...], w_slot, 2)):
                acc[:, cols] += y
            return c

        lax.fori_loop(1, n_groups - 1, pair_group, 0)

        last_slot = (n_groups - 1) % 2
        for c in group_copies(n_groups - 1, last_slot):
            c.wait()

        @pl.when(i + 1 < n_row_tiles)
        def _():
            for c in group_copies(0, 0):
                c.start()

        gf = gf_ref[...]
        for rows in _row_slabs(tm):
            ss = jnp.zeros((EDGE_ROWS, 1), F32)
            for cols, y in zip(col_chunks, swiglu_down(h_ref[rows, :], last_slot, 2)):
                o = acc[rows, cols] + y
                acc[rows, cols] = o
                ss = ss + (o * o).sum(axis=-1, keepdims=True)
            acc[rows, :] = acc[rows, :] * lax.rsqrt(ss / d + EPS) * gf
        out_copy(i, slot).start()
        return carry

    x_copy(0, 0).start()
    for c in group_copies(0, 0):
        c.start()
    lax.fori_loop(0, n_row_tiles, row_tile, 0)
    out_copy(n_row_tiles - 1, (n_row_tiles - 1) % 2).wait()


def _ffn(x1, g2, wg_tiles, wu_tiles, wd, gf, tm):
    t, d = x1.shape
    n_tiles, _, tf = wg_tiles.shape
    assert n_tiles % 2 == 1 and n_tiles >= 7 and (1 + (n_tiles - 1) // 2) % 2 == 0, n_tiles
    assert t % tm == 0 and tm % EDGE_ROWS == 0
    hbm = pl.BlockSpec(memory_space=pl.ANY)
    whole = pl.BlockSpec(memory_space=pltpu.VMEM)
    return pl.pallas_call(
        _ffn_kernel,
        in_specs=[hbm, whole, hbm, hbm, hbm, whole],
        out_specs=hbm,
        out_shape=jax.ShapeDtypeStruct((t, d), F32),
        scratch_shapes=[
            pltpu.VMEM((2, tm, d), F32),
            pltpu.VMEM((tm, d), BF16),
            pltpu.VMEM((2, 2, d, tf), BF16),
            pltpu.VMEM((2, 2, d, tf), BF16),
            pltpu.VMEM((2, 2 * tf, d), BF16),
            pltpu.SemaphoreType.DMA((2,)),
            pltpu.SemaphoreType.DMA((2,)),
            pltpu.SemaphoreType.DMA((3, 2, 2)),
        ],
        compiler_params=pltpu.CompilerParams(vmem_limit_bytes=VMEM_LIMIT_BYTES),
        name="ffn",
    )(x1, g2, wg_tiles, wu_tiles, wd, gf)


def kernel(x, norm1_gain, w_in, rel_bias, lower_bounds, grn_norm_gain, attn_out_gain, w_out,
           norm2_gain, w_gate, w_up, w_down, final_gain):
    batch, seq, d = x.shape
    depth = w_in.shape[0]
    assert seq % TB == 0 and seq % QB == 0 and w_in.shape[2] == 7 * GROUP and d == 2 * GROUP
    lb_all = jnp.cumsum(jax.nn.softmax(lower_bounds.astype(F32), axis=0), axis=0)
    xf = x.reshape(batch * seq, d)
    col_scale = jnp.concatenate([jnp.full((1, GROUP), QUERY_SCALE, F32),
                                 jnp.ones((1, w_in.shape[2] - GROUP), F32)], axis=1)
    for l in range(depth):
        proj = _inproj(xf, norm1_gain[l][None, :], w_in[l].astype(BF16), col_scale,
                       tm=1024, tn=1792)
        attn = _attention(proj, _bias_rows(rel_bias[l]), attn_out_gain[l][None, :], batch, seq)
        rec, wo, wg, wu, wd = _hgrn(proj, lb_all[l][None, :], grn_norm_gain[l][None, :], batch, seq,
                                    [(w_out[l], None), (w_gate[l], FFN_HIDDEN_TILE),
                                     (w_up[l], FFN_HIDDEN_TILE), (w_down[l], None)])
        x1 = _outproj(xf, attn, rec, wo, tm=1024)
        last = l == depth - 1
        assert last, "the final rmsnorm is fused into the last layer's FFN"
        xf = _ffn(x1, norm2_gain[l][None, :], wg, wu, wd, final_gain[None, :], tm=1024)
    return xf.reshape(batch, seq, d)
```

```python
import functools

import jax
import jax.numpy as jnp
from jax import lax
from jax.experimental import pallas as pl
from jax.experimental.pallas import tpu as pltpu

F32 = jnp.float32
BF16 = jnp.bfloat16

EPS = 1e-6
CHUNK = 64
N_LEFT = 8
MAX_REL = 128
HEADS = 8
HEAD_DIM = 128
GROUP = HEADS * HEAD_DIM

VMEM_LIMIT_BYTES = 56 * 1024 * 1024
BF16_SUBLANES = 16
NEG = -1e30
LOG2E = 1.4426950408889634
QUERY_SCALE = HEAD_DIM ** -0.5 * LOG2E


def _params(*sem):
    return pltpu.CompilerParams(dimension_semantics=sem, vmem_limit_bytes=VMEM_LIMIT_BYTES)


def _rms_rows(x, gain):
    ms = jnp.mean(x * x, axis=-1, keepdims=True)
    return x * lax.rsqrt(ms + EPS) * gain


EDGE_ROWS = 256


def _row_slabs(n_rows):
    return [slice(r, r + EDGE_ROWS) for r in range(0, n_rows, EDGE_ROWS)]


def _inproj_kernel(x_ref, gain_ref, w_ref, cs_ref, o_ref, h_ref):
    j = pl.program_id(1)
    col_scale = cs_ref[...]

    def project(h):
        return (jnp.dot(h, w_ref[...], preferred_element_type=F32) * col_scale).astype(o_ref.dtype)

    @pl.when(j == 0)
    def _():
        gain = gain_ref[...]
        for rows in _row_slabs(x_ref.shape[0]):
            h = _rms_rows(x_ref[rows, :], gain).astype(h_ref.dtype)
            h_ref[rows, :] = h
            o_ref[rows, :] = project(h)

    @pl.when(j > 0)
    def _():
        o_ref[...] = project(h_ref[...])


def _inproj(x2d, gain, w_bf16, col_scale, tm, tn):
    t, d = x2d.shape
    n = w_bf16.shape[1]
    return pl.pallas_call(
        _inproj_kernel,
        grid=(t // tm, n // tn),
        in_specs=[
            pl.BlockSpec((tm, d), lambda i, j: (i, 0)),
            pl.BlockSpec((1, d), lambda i, j: (0, 0)),
            pl.BlockSpec((d, tn), lambda i, j: (0, j)),
            pl.BlockSpec((1, tn), lambda i, j: (0, j)),
        ],
        out_specs=pl.BlockSpec((tm, tn), lambda i, j: (i, j)),
        out_shape=jax.ShapeDtypeStruct((t, n), BF16),
        scratch_shapes=[pltpu.VMEM((tm, d), BF16)],
        compiler_params=_params("arbitrary", "arbitrary"),
        name="inproj",
    )(x2d, gain, w_bf16, col_scale)


QB = 256
ATTN_ROWS = 512
N_SEG = 1 + (N_LEFT * CHUNK) // QB
WIN = N_SEG * QB
ROLL_W = 1024


def _bias_rows(rel_bias):
    h = rel_bias.shape[0]
    far_past = rel_bias[:, 2 * MAX_REL:2 * MAX_REL + 1]
    far_future = rel_bias[:, 0:1]
    n_head = (WIN - QB) - MAX_REL
    n_tail = WIN - n_head - (2 * MAX_REL + 1)
    return jnp.concatenate([
        jnp.broadcast_to(far_past, (h, n_head)),
        rel_bias[:, ::-1],
        jnp.broadcast_to(far_future, (h, n_tail)),
        jnp.broadcast_to(far_past, (h, ROLL_W - WIN)),
    ], axis=1).astype(F32)


def _attn_kernel(q_ref, kp_ref, kc_ref, vp_ref, vc_ref, brow_ref, gain_ref,
                 o_ref, bias_ref, acc_ref, s_ref):
    step = pl.program_id(1)
    n_blocks = ATTN_ROWS // QB

    @pl.when((pl.program_id(0) == 0) & (step == 0))
    def _():
        qc = lax.broadcasted_iota(jnp.int32, (QB, WIN), 0) // CHUNK
        kc = lax.broadcasted_iota(jnp.int32, (QB, WIN), 1) // CHUNK
        band = (kc >= qc) & (kc <= qc + N_LEFT)
        for h in range(HEADS):
            row = jnp.broadcast_to(brow_ref[h:h + 1, :] * LOG2E, (QB, ROLL_W))
            toeplitz = pltpu.roll(row, 0, 1, stride=1, stride_axis=0)
            bias_ref[h] = jnp.where(band, toeplitz[:, :WIN], NEG)

    def granule(prev_ref, cur_ref, r, sl):
        ref = (prev_ref, cur_ref)[r // n_blocks]
        start = (r % n_blocks) * QB
        return ref[start:start + QB, sl]

    def lane_halves(op, x):
        while x.shape[1] > HEAD_DIM:
            half = x.shape[1] // 2
            x = op(x[:, :half], x[:, half:])
        return x

    def attend(blk, segs):
        rows = slice(blk * QB, (blk + 1) * QB)
        m = []
        for h in range(HEADS):
            sl = slice(h * HEAD_DIM, (h + 1) * HEAD_DIM)
            qh = q_ref[rows, sl]
            mx = None
            for j in segs:
                cols = slice(j * QB, (j + 1) * QB)
                sj = lax.dot_general(qh, granule(kp_ref, kc_ref, j + blk, sl),
                                     (((1,), (1,)), ((), ())),
                                     preferred_element_type=F32) + bias_ref[h, :, cols]
                s_ref[h, :, cols] = sj
                mx = sj if mx is None else jnp.maximum(mx, sj)
            m.append(lane_halves(jnp.maximum, mx).max(axis=-1, keepdims=True))
        sq = jnp.zeros((QB, HEAD_DIM), F32)
        ones = jnp.ones((QB, HEAD_DIM), BF16)
        for h in range(HEADS):
            sl = slice(h * HEAD_DIM, (h + 1) * HEAD_DIM)
            p = jnp.concatenate(
                [jnp.exp2(s_ref[h, :, j * QB:(j + 1) * QB] - m[h]).astype(BF16) for j in segs], axis=1)
            v1 = jnp.concatenate(
                [jnp.concatenate([granule(vp_ref, vc_ref, j + blk, sl), ones], axis=1)
                 for j in segs], axis=0)
            pv = jnp.dot(p, v1, preferred_element_type=F32)
            o = pv[:, :HEAD_DIM] * (1.0 / pv[:, HEAD_DIM:])
            acc_ref[rows, sl] = o
            sq = sq + o * o
        inv = lax.rsqrt(sq.sum(axis=-1, keepdims=True) / GROUP + EPS)
        o_ref[rows, :] = (acc_ref[rows, :] * inv * gain_ref[...]).astype(o_ref.dtype)

    @pl.when(step == 0)
    def _():
        for blk in range(n_blocks):
            attend(blk, tuple(range(n_blocks - blk, N_SEG)))

    @pl.when(step > 0)
    def _():
        for blk in range(n_blocks):
            attend(blk, tuple(range(N_SEG)))


def _attention(proj, brow, gain, batch, seq):
    t = proj.shape[0]
    nb = seq // ATTN_ROWS
    assert seq % ATTN_ROWS == 0 and ATTN_ROWS % QB == 0 and (N_SEG - 1) * QB <= ATTN_ROWS

    def rows(col_block, back=0):
        return pl.BlockSpec((ATTN_ROWS, GROUP),
                            lambda b, i: (b * nb + jnp.maximum(i - back, 0), col_block))

    return pl.pallas_call(
        _attn_kernel,
        grid=(batch, nb),
        in_specs=[rows(0), rows(1, back=1), rows(1), rows(2, back=1), rows(2),
                  pl.BlockSpec((HEADS, ROLL_W), lambda b, i: (0, 0)),
                  pl.BlockSpec((1, GROUP), lambda b, i: (0, 0))],
        out_specs=rows(0),
        out_shape=jax.ShapeDtypeStruct((t, GROUP), BF16),
        scratch_shapes=[pltpu.VMEM((HEADS, QB, WIN), F32),
                        pltpu.VMEM((ATTN_ROWS, GROUP), F32),
                        pltpu.VMEM((HEADS, QB, WIN), F32)],
        compiler_params=_params("arbitrary", "arbitrary"),
        name="attention",
    )(proj, proj, proj, proj, proj, brow, gain)


TB = 512


def _silu(x):
    hx = 0.5 * x
    return hx + hx * jnp.tanh(hx)


def _split3(x):
    hi = x.astype(BF16)
    r1 = x - hi.astype(F32)
    mid = r1.astype(BF16)
    lo = (r1 - mid.astype(F32)).astype(BF16)
    return hi, mid, lo


def _hgrn_kernel(n_cast, q_ref, f_ref, i_ref, g_ref, lb_ref, gn_ref, *refs):
    cast_in, (o_ref, *cast_out) = refs[:n_cast], refs[n_cast:2 * n_cast + 1]
    st_ref, qd_ref, a_ref, u_ref, dec_ref = refs[2 * n_cast + 1:]

    for src, dst in zip(cast_in, cast_out):
        if len(dst.shape) == 2:
            dst[...] = src[...].astype(dst.dtype)
        else:
            width = dst.shape[2]
            for n in range(dst.shape[0]):
                dst[n] = src[:, n * width:(n + 1) * width].astype(dst.dtype)

    @pl.when(pl.program_id(1) == 0)
    def _():
        st_ref[...] = jnp.zeros_like(st_ref)

    lb = lb_ref[...]
    f_mid = 0.5 * (1.0 + lb)
    f_amp = 0.5 * (1.0 - lb)
    gn = gn_ref[...]
    r_i = lax.broadcasted_iota(jnp.int32, (CHUNK, CHUNK), 0)
    c_i = lax.broadcasted_iota(jnp.int32, (CHUNK, CHUNK), 1)
    causal = r_i >= c_i
    tril = causal.astype(BF16)
    tril3 = jnp.concatenate([tril, tril, tril], axis=1)

    def chunk_rows(c):
        return slice(c * CHUNK, (c + 1) * CHUNK)

    def local_terms(c):
        rows = chunk_rows(c)
        q = _silu(q_ref[rows, :].astype(F32))
        f = f_mid + f_amp * jnp.tanh(0.5 * f_ref[rows, :].astype(F32))
        k = 1.0 - f
        b = jnp.dot(tril3, jnp.concatenate(_split3(jnp.log2(f)), axis=0),
                    preferred_element_type=F32)
        decay = jnp.exp2(b[CHUNK - 1:CHUNK, :])
        q_dec = (q * jnp.exp2(b)).astype(BF16)
        k_undecayed = k * jnp.exp2(-b)
        k_intra = k_undecayed.astype(BF16)
        k_state = (k_undecayed * decay).astype(BF16)
        qd_ref[rows, :] = q_dec
        dec_ref[c] = decay
        for h in range(HEADS):
            sl = slice(h * HEAD_DIM, (h + 1) * HEAD_DIM)
            a = lax.dot_general(q_dec[:, sl], k_intra[:, sl], (((1,), (1,)), ((), ())),
                                preferred_element_type=F32)
            a_ref[h, rows, :] = jnp.where(causal, a, 0.0).astype(BF16)
            u_ref[c, h] = lax.dot_general(i_ref[rows, sl], k_state[:, sl], (((0,), (0,)), ((), ())),
                                          preferred_element_type=F32)

    def outputs(c):
        rows = chunk_rows(c)
        decay = dec_ref[c]
        for h in range(HEADS):
            sl = slice(h * HEAD_DIM, (h + 1) * HEAD_DIM)
            st = st_ref[h]
            o = jnp.dot(a_ref[h, rows, :], i_ref[rows, sl], preferred_element_type=F32)
            o = o + lax.dot_general(qd_ref[rows, sl], st.astype(BF16), (((1,), (1,)), ((), ())),
                                    preferred_element_type=F32)
            st_ref[h] = st * decay[:, sl] + u_ref[c, h]
            o = _rms_rows(o, gn)
            o_ref[rows, sl] = (o * _silu(g_ref[rows, sl].astype(F32))).astype(o_ref.dtype)

    for c in range(TB // CHUNK):
        local_terms(c)
    for c in range(TB // CHUNK):
        outputs(c)


def _hgrn(proj, lb, gn, batch, seq, weights):
    t = proj.shape[0]
    nb = seq // TB
    steps = batch * nb
    for w, _ in weights:
        assert w.shape[0] % (steps * BF16_SUBLANES) == 0, w.shape

    def col(cb):
        return pl.BlockSpec((TB, GROUP), lambda b, i: (b * nb + i, cb))

    def slab(w):
        return pl.BlockSpec((w.shape[0] // steps, w.shape[1]), lambda b, i: (b * nb + i, 0))

    def cast_shape(w, tile):
        return w.shape if tile is None else (w.shape[1] // tile, w.shape[0], tile)

    def cast_slab(w, tile):
        if tile is None:
            return slab(w)
        return pl.BlockSpec((w.shape[1] // tile, w.shape[0] // steps, tile),
                            lambda b, i: (0, b * nb + i, 0))

    return pl.pallas_call(
        functools.partial(_hgrn_kernel, len(weights)),
        grid=(batch, nb),
        in_specs=[col(3), col(4), col(5), col(6),
                  pl.BlockSpec((1, GROUP), lambda b, i: (0, 0)),
                  pl.BlockSpec((1, HEAD_DIM), lambda b, i: (0, 0))]
        + [slab(w) for w, _ in weights],
        out_specs=[pl.BlockSpec((TB, GROUP), lambda b, i: (b * nb + i, 0))]
        + [cast_slab(w, tile) for w, tile in weights],
        out_shape=[jax.ShapeDtypeStruct((t, GROUP), BF16)]
        + [jax.ShapeDtypeStruct(cast_shape(w, tile), BF16) for w, tile in weights],
        scratch_shapes=[
            pltpu.VMEM((HEADS, HEAD_DIM, HEAD_DIM), F32),
            pltpu.VMEM((TB, GROUP), BF16),
            pltpu.VMEM((HEADS, TB, CHUNK), BF16),
            pltpu.VMEM((TB // CHUNK, HEADS, HEAD_DIM, HEAD_DIM), F32),
            pltpu.VMEM((TB // CHUNK, 1, GROUP), F32),
        ],
        compiler_params=_params("arbitrary", "arbitrary"),
        name="hgrn2",
    )(proj, proj, proj, proj, lb, gn, *[w for w, _ in weights])


OUT_COLS = 512


def _outproj_kernel(x_ref, a_ref, r_ref, w_ref, o_ref):
    mixed = jnp.concatenate([a_ref[...], r_ref[...]], axis=1)
    for n in range(0, o_ref.shape[1], OUT_COLS):
        cols = slice(n, n + OUT_COLS)
        o_ref[:, cols] = x_ref[:, cols] + jnp.dot(mixed, w_ref[:, cols],
                                                  preferred_element_type=F32)


def _outproj(x2d, attn, rec, w_bf16, tm):
    t, d = x2d.shape
    g = attn.shape[1]
    return pl.pallas_call(
        _outproj_kernel,
        grid=(t // tm,),
        in_specs=[
            pl.BlockSpec((tm, d), lambda i: (i, 0)),
            pl.BlockSpec((tm, g), lambda i: (i, 0)),
            pl.BlockSpec((tm, g), lambda i: (i, 0)),
            pl.BlockSpec((2 * g, d), lambda i: (0, 0), pipeline_mode=pl.Buffered(1)),
        ],
        out_specs=pl.BlockSpec((tm, d), lambda i: (i, 0)),
        out_shape=jax.ShapeDtypeStruct((t, d), F32),
        compiler_params=_params("arbitrary"),
        name="outproj",
    )(x2d, attn, rec, w_bf16)


FFN_DOWN_COLS = 512
FFN_HIDDEN_TILE = 512


def _ffn_kernel(x_hbm, g2_ref, wg_hbm, wu_hbm, wd_hbm, gf_ref, out_hbm,
                acc_ref, h_ref, wg_buf, wu_buf, wd_buf, x_sem, o_sem, w_sem):
    _, tm, d = acc_ref.shape
    n_tiles, _, tf = wg_hbm.shape
    n_groups = 1 + (n_tiles - 1) // 2
    n_row_tiles = x_hbm.shape[0] // tm
    col_chunks = [slice(n, n + FFN_DOWN_COLS) for n in range(0, d, FFN_DOWN_COLS)]

    def x_copy(i, slot):
        return pltpu.make_async_copy(x_hbm.at[pl.ds(i * tm, tm), :], acc_ref.at[slot],
                                     x_sem.at[slot])

    def out_copy(i, slot):
        return pltpu.make_async_copy(acc_ref.at[slot], out_hbm.at[pl.ds(i * tm, tm), :],
                                     o_sem.at[slot])

    def group_tiles(k):
        return (0,) if isinstance(k, int) and k == 0 else (2 * k - 1, 2 * k)

    def group_copies(k, slot):
        copies = []
        for p, j in enumerate(group_tiles(k)):
            copies += [
                pltpu.make_async_copy(wg_hbm.at[j], wg_buf.at[slot, p], w_sem.at[0, slot, p]),
                pltpu.make_async_copy(wu_hbm.at[j], wu_buf.at[slot, p], w_sem.at[1, slot, p]),
                pltpu.make_async_copy(wd_hbm.at[pl.ds(j * tf, tf), :],
                                      wd_buf.at[slot, pl.ds(p * tf, tf), :], w_sem.at[2, slot, p])]
        return copies

    def swiglu_down(h, slot, n):
        acts = []
        for p in range(n):
            gate = jnp.dot(h, wg_buf[slot, p], preferred_element_type=F32)
            up = jnp.dot(h, wu_buf[slot, p], preferred_element_type=F32)
            acts.append((gate * jax.nn.sigmoid(gate) * up).astype(BF16))
        act = acts[0] if n == 1 else jnp.concatenate(acts, axis=1)
        wd = wd_buf.at[slot]
        return [jnp.dot(act, wd[:n * tf, cols], preferred_element_type=F32) for cols in col_chunks]

    def row_tile(i, carry):
        slot = lax.rem(i, 2)
        acc = acc_ref.at[slot]
        x_copy(i, slot).wait()

        for c in group_copies(0, 0):
            c.wait()
        for c in group_copies(1, 1):
            c.start()
        g2 = g2_ref[...]
        for rows in _row_slabs(tm):
            h = _rms_rows(acc[rows, :], g2).astype(h_ref.dtype)
            h_ref[rows, :] = h
            for cols, y in zip(col_chunks, swiglu_down(h, 0, 1)):
                acc[rows, cols] += y

        @pl.when(i >= 1)
        def _():
            out_copy(i - 1, 1 - slot).wait()

        @pl.when(i + 1 < n_row_tiles)
        def _():
            x_copy(i + 1, 1 - slot).start()

        def pair_group(k, c):
            w_slot = lax.rem(k, 2)
            for cp in group_copies(k, w_slot):
                cp.wait()
            for cp in group_copies(k + 1, 1 - w_slot):
                cp.start()
            for cols, y in zip(col_chunks, swiglu_down(h_ref[...], w_slot, 2)):
                acc[:, cols] += y
            return c

        lax.fori_loop(1, n_groups - 1, pair_group, 0)

        last_slot = (n_groups - 1) % 2
        for c in group_copies(n_groups - 1, last_slot):
            c.wait()

        @pl.when(i + 1 < n_row_tiles)
        def _():
            for c in group_copies(0, 0):
                c.start()

        gf = gf_ref[...]
        for rows in _row_slabs(tm):
            ss = jnp.zeros((EDGE_ROWS, 1), F32)
            for cols, y in zip(col_chunks, swiglu_down(h_ref[rows, :], last_slot, 2)):
                o = acc[rows, cols] + y
                acc[rows, cols] = o
                ss = ss + (o * o).sum(axis=-1, keepdims=True)
            acc[rows, :] = acc[rows, :] * lax.rsqrt(ss / d + EPS) * gf
        out_copy(i, slot).start()
        return carry

    x_copy(0, 0).start()
    for c in group_copies(0, 0):
        c.start()
    lax.fori_loop(0, n_row_tiles, row_tile, 0)
    out_copy(n_row_tiles - 1, (n_row_tiles - 1) % 2).wait()


def _ffn(x1, g2, wg_tiles, wu_tiles, wd, gf, tm):
    t, d = x1.shape
    n_tiles, _, tf = wg_tiles.shape
    assert n_tiles % 2 == 1 and n_tiles >= 7 and (1 + (n_tiles - 1) // 2) % 2 == 0, n_tiles
    assert t % tm == 0 and tm % EDGE_ROWS == 0
    hbm = pl.BlockSpec(memory_space=pl.ANY)
    whole = pl.BlockSpec(memory_space=pltpu.VMEM)
    return pl.pallas_call(
        _ffn_kernel,
        in_specs=[hbm, whole, hbm, hbm, hbm, whole],
        out_specs=hbm,
        out_shape=jax.ShapeDtypeStruct((t, d), F32),
        scratch_shapes=[
            pltpu.VMEM((2, tm, d), F32),
            pltpu.VMEM((tm, d), BF16),
            pltpu.VMEM((2, 2, d, tf), BF16),
            pltpu.VMEM((2, 2, d, tf), BF16),
            pltpu.VMEM((2, 2 * tf, d), BF16),
            pltpu.SemaphoreType.DMA((2,)),
            pltpu.SemaphoreType.DMA((2,)),
            pltpu.SemaphoreType.DMA((3, 2, 2)),
        ],
        compiler_params=pltpu.CompilerParams(vmem_limit_bytes=VMEM_LIMIT_BYTES),
        name="ffn",
    )(x1, g2, wg_tiles, wu_tiles, wd, gf)


def kernel(x, norm1_gain, w_in, rel_bias, lower_bounds, grn_norm_gain, attn_out_gain, w_out,
           norm2_gain, w_gate, w_up, w_down, final_gain):
    batch, seq, d = x.shape
    depth = w_in.shape[0]
    assert seq % TB == 0 and seq % QB == 0 and w_in.shape[2] == 7 * GROUP and d == 2 * GROUP
    lb_all = jnp.cumsum(jax.nn.softmax(lower_bounds.astype(F32), axis=0), axis=0)
    xf = x.reshape(batch * seq, d)
    col_scale = jnp.concatenate([jnp.full((1, GROUP), QUERY_SCALE, F32),
                                 jnp.ones((1, w_in.shape[2] - GROUP), F32)], axis=1)
    for l in range(depth):
        proj = _inproj(xf, norm1_gain[l][None, :], w_in[l].astype(BF16), col_scale,
                       tm=1024, tn=1792)
        attn = _attention(proj, _bias_rows(rel_bias[l]), attn_out_gain[l][None, :], batch, seq)
        rec, wo, wg, wu, wd = _hgrn(proj, lb_all[l][None, :], grn_norm_gain[l][None, :], batch, seq,
                                    [(w_out[l], None), (w_gate[l], FFN_HIDDEN_TILE),
                                     (w_up[l], FFN_HIDDEN_TILE), (w_down[l], None)])
        x1 = _outproj(xf, attn, rec, wo, tm=1024)
        last = l == depth - 1
        assert last, "the final rmsnorm is fused into the last layer's FFN"
        xf = _ffn(x1, norm2_gain[l][None, :], wg, wu, wd, final_gain[None, :], tm=1024)
    return xf.reshape(batch, seq, d)
```

```python
import functools

import jax
import jax.numpy as jnp
from jax import lax
from jax.experimental import pallas as pl
from jax.experimental.pallas import tpu as pltpu

F32 = jnp.float32
BF16 = jnp.bfloat16

EPS = 1e-6
CHUNK = 64
N_LEFT = 8
MAX_REL = 128
HEADS = 8
HEAD_DIM = 128
GROUP = HEADS * HEAD_DIM

VMEM_LIMIT_BYTES = 56 * 1024 * 1024
BF16_SUBLANES = 16
NEG = -1e30
LOG2E = 1.4426950408889634
QUERY_SCALE = HEAD_DIM ** -0.5 * LOG2E


def _params(*sem):
    return pltpu.CompilerParams(dimension_semantics=sem, vmem_limit_bytes=VMEM_LIMIT_BYTES)


def _rms_rows(x, gain):
    ms = jnp.mean(x * x, axis=-1, keepdims=True)
    return x * lax.rsqrt(ms + EPS) * gain


EDGE_ROWS = 256


def _row_slabs(n_rows):
    return [slice(r, r + EDGE_ROWS) for r in range(0, n_rows, EDGE_ROWS)]


def _inproj_kernel(x_ref, gain_ref, w_ref, cs_ref, o_ref, h_ref):
    j = pl.program_id(1)
    col_scale = cs_ref[...]

    def project(h):
        return (jnp.dot(h, w_ref[...], preferred_element_type=F32) * col_scale).astype(o_ref.dtype)

    @pl.when(j == 0)
    def _():
        gain = gain_ref[...]
        for rows in _row_slabs(x_ref.shape[0]):
            h = _rms_rows(x_ref[rows, :], gain).astype(h_ref.dtype)
            h_ref[rows, :] = h
            o_ref[rows, :] = project(h)

    @pl.when(j > 0)
    def _():
        o_ref[...] = project(h_ref[...])


def _inproj(x2d, gain, w_bf16, col_scale, tm, tn):
    t, d = x2d.shape
    n = w_bf16.shape[1]
    return pl.pallas_call(
        _inproj_kernel,
        grid=(t // tm, n // tn),
        in_specs=[
            pl.BlockSpec((tm, d), lambda i, j: (i, 0)),
            pl.BlockSpec((1, d), lambda i, j: (0, 0)),
            pl.BlockSpec((d, tn), lambda i, j: (0, j)),
            pl.BlockSpec((1, tn), lambda i, j: (0, j)),
        ],
        out_specs=pl.BlockSpec((tm, tn), lambda i, j: (i, j)),
        out_shape=jax.ShapeDtypeStruct((t, n), BF16),
        scratch_shapes=[pltpu.VMEM((tm, d), BF16)],
        compiler_params=_params("arbitrary", "arbitrary"),
        name="inproj",
    )(x2d, gain, w_bf16, col_scale)


QB = 256
ATTN_ROWS = 1024
N_SEG = 1 + (N_LEFT * CHUNK) // QB
WIN = N_SEG * QB
ROLL_W = 1024


def _bias_rows(rel_bias):
    h = rel_bias.shape[0]
    far_past = rel_bias[:, 2 * MAX_REL:2 * MAX_REL + 1]
    far_future = rel_bias[:, 0:1]
    n_head = (WIN - QB) - MAX_REL
    n_tail = WIN - n_head - (2 * MAX_REL + 1)
    return jnp.concatenate([
        jnp.broadcast_to(far_past, (h, n_head)),
        rel_bias[:, ::-1],
        jnp.broadcast_to(far_future, (h, n_tail)),
        jnp.broadcast_to(far_past, (h, ROLL_W - WIN)),
    ], axis=1).astype(F32)


def _attn_kernel(q_ref, kp_ref, kc_ref, vp_ref, vc_ref, brow_ref, gain_ref,
                 o_ref, bias_ref, acc_ref, s_ref):
    step = pl.program_id(1)
    n_blocks = ATTN_ROWS // QB
    first_granule = n_blocks - (N_SEG - 1)

    @pl.when((pl.program_id(0) == 0) & (step == 0))
    def _():
        qc = lax.broadcasted_iota(jnp.int32, (QB, WIN), 0) // CHUNK
        kc = lax.broadcasted_iota(jnp.int32, (QB, WIN), 1) // CHUNK
        band = (kc >= qc) & (kc <= qc + N_LEFT)
        for h in range(HEADS):
            row = jnp.broadcast_to(brow_ref[h:h + 1, :] * LOG2E, (QB, ROLL_W))
            toeplitz = pltpu.roll(row, 0, 1, stride=1, stride_axis=0)
            bias_ref[h] = jnp.where(band, toeplitz[:, :WIN], NEG)

    def granule(prev_ref, cur_ref, r, sl):
        ref = (prev_ref, cur_ref)[r // n_blocks]
        start = (r % n_blocks) * QB
        return ref[start:start + QB, sl]

    def lane_halves(op, x):
        while x.shape[1] > HEAD_DIM:
            half = x.shape[1] // 2
            x = op(x[:, :half], x[:, half:])
        return x

    def attend(blk, segs):
        rows = slice(blk * QB, (blk + 1) * QB)
        m = []
        for h in range(HEADS):
            sl = slice(h * HEAD_DIM, (h + 1) * HEAD_DIM)
            qh = q_ref[rows, sl]
            mx = None
            for j in segs:
                cols = slice(j * QB, (j + 1) * QB)
                sj = lax.dot_general(qh, granule(kp_ref, kc_ref, first_granule + blk + j, sl),
                                     (((1,), (1,)), ((), ())),
                                     preferred_element_type=F32) + bias_ref[h, :, cols]
                s_ref[h, :, cols] = sj
                mx = sj if mx is None else jnp.maximum(mx, sj)
            m.append(lane_halves(jnp.maximum, mx).max(axis=-1, keepdims=True))
        sq = jnp.zeros((QB, HEAD_DIM), F32)
        ones = jnp.ones((QB, HEAD_DIM), BF16)
        for h in range(HEADS):
            sl = slice(h * HEAD_DIM, (h + 1) * HEAD_DIM)
            p = jnp.concatenate(
                [jnp.exp2(s_ref[h, :, j * QB:(j + 1) * QB] - m[h]).astype(BF16) for j in segs], axis=1)
            v1 = jnp.concatenate(
                [jnp.concatenate([granule(vp_ref, vc_ref, first_granule + blk + j, sl), ones], axis=1)
                 for j in segs], axis=0)
            pv = jnp.dot(p, v1, preferred_element_type=F32)
            o = pv[:, :HEAD_DIM] * (1.0 / pv[:, HEAD_DIM:])
            acc_ref[rows, sl] = o
            sq = sq + o * o
        inv = lax.rsqrt(sq.sum(axis=-1, keepdims=True) / GROUP + EPS)
        o_ref[rows, :] = (acc_ref[rows, :] * inv * gain_ref[...]).astype(o_ref.dtype)

    @pl.when(step == 0)
    def _():
        for blk in range(n_blocks):
            attend(blk, tuple(range(max(N_SEG - 1 - blk, 0), N_SEG)))

    @pl.when(step > 0)
    def _():
        for blk in range(n_blocks):
            attend(blk, tuple(range(N_SEG)))


def _attention(proj, brow, gain, batch, seq):
    t = proj.shape[0]
    nb = seq // ATTN_ROWS
    assert seq % ATTN_ROWS == 0 and ATTN_ROWS % QB == 0 and (N_SEG - 1) * QB <= ATTN_ROWS

    def rows(col_block, back=0):
        return pl.BlockSpec((ATTN_ROWS, GROUP),
                            lambda b, i: (b * nb + jnp.maximum(i - back, 0), col_block))

    return pl.pallas_call(
        _attn_kernel,
        grid=(batch, nb),
        in_specs=[rows(0), rows(1, back=1), rows(1), rows(2, back=1), rows(2),
                  pl.BlockSpec((HEADS, ROLL_W), lambda b, i: (0, 0)),
                  pl.BlockSpec((1, GROUP), lambda b, i: (0, 0))],
        out_specs=rows(0),
        out_shape=jax.ShapeDtypeStruct((t, GROUP), BF16),
        scratch_shapes=[pltpu.VMEM((HEADS, QB, WIN), F32),
                        pltpu.VMEM((ATTN_ROWS, GROUP), F32),
                        pltpu.VMEM((HEADS, QB, WIN), F32)],
        compiler_params=_params("arbitrary", "arbitrary"),
        name="attention",
    )(proj, proj, proj, proj, proj, brow, gain)


TB = 512


def _silu(x):
    hx = 0.5 * x
    return hx + hx * jnp.tanh(hx)


def _split3(x):
    hi = x.astype(BF16)
    r1 = x - hi.astype(F32)
    mid = r1.astype(BF16)
    lo = (r1 - mid.astype(F32)).astype(BF16)
    return hi, mid, lo


def _hgrn_kernel(n_cast, q_ref, f_ref, i_ref, g_ref, lb_ref, gn_ref, *refs):
    cast_in, (o_ref, *cast_out) = refs[:n_cast], refs[n_cast:2 * n_cast + 1]
    st_ref, qd_ref, a_ref, u_ref, dec_ref = refs[2 * n_cast + 1:]

    for src, dst in zip(cast_in, cast_out):
        if len(dst.shape) == 2:
            dst[...] = src[...].astype(dst.dtype)
        else:
            width = dst.shape[2]
            for n in range(dst.shape[0]):
                dst[n] = src[:, n * width:(n + 1) * width].astype(dst.dtype)

    @pl.when(pl.program_id(1) == 0)
    def _():
        st_ref[...] = jnp.zeros_like(st_ref)

    lb = lb_ref[...]
    f_mid = 0.5 * (1.0 + lb)
    f_amp = 0.5 * (1.0 - lb)
    gn = gn_ref[...]
    r_i = lax.broadcasted_iota(jnp.int32, (CHUNK, CHUNK), 0)
    c_i = lax.broadcasted_iota(jnp.int32, (CHUNK, CHUNK), 1)
    causal = r_i >= c_i
    tril = causal.astype(BF16)
    tril3 = jnp.concatenate([tril, tril, tril], axis=1)

    def chunk_rows(c):
        return slice(c * CHUNK, (c + 1) * CHUNK)

    def local_terms(c):
        rows = chunk_rows(c)
        q = _silu(q_ref[rows, :].astype(F32))
        f = f_mid + f_amp * jnp.tanh(0.5 * f_ref[rows, :].astype(F32))
        k = 1.0 - f
        b = jnp.dot(tril3, jnp.concatenate(_split3(jnp.log2(f)), axis=0),
                    preferred_element_type=F32)
        decay = jnp.exp2(b[CHUNK - 1:CHUNK, :])
        q_dec = (q * jnp.exp2(b)).astype(BF16)
        k_undecayed = k * jnp.exp2(-b)
        k_intra = k_undecayed.astype(BF16)
        k_state = (k_undecayed * decay).astype(BF16)
        qd_ref[rows, :] = q_dec
        dec_ref[c] = decay
        for h in range(HEADS):
            sl = slice(h * HEAD_DIM, (h + 1) * HEAD_DIM)
            a = lax.dot_general(q_dec[:, sl], k_intra[:, sl], (((1,), (1,)), ((), ())),
                                preferred_element_type=F32)
            a_ref[h, rows, :] = jnp.where(causal, a, 0.0).astype(BF16)
            u_ref[c, h] = lax.dot_general(i_ref[rows, sl], k_state[:, sl], (((0,), (0,)), ((), ())),
                                          preferred_element_type=F32)

    def outputs(c):
        rows = chunk_rows(c)
        decay = dec_ref[c]
        for h in range(HEADS):
            sl = slice(h * HEAD_DIM, (h + 1) * HEAD_DIM)
            st = st_ref[h]
            o = jnp.dot(a_ref[h, rows, :], i_ref[rows, sl], preferred_element_type=F32)
            o = o + lax.dot_general(qd_ref[rows, sl], st.astype(BF16), (((1,), (1,)), ((), ())),
                                    preferred_element_type=F32)
            st_ref[h] = st * decay[:, sl] + u_ref[c, h]
            o = _rms_rows(o, gn)
            o_ref[rows, sl] = (o * _silu(g_ref[rows, sl].astype(F32))).astype(o_ref.dtype)

    for c in range(TB // CHUNK):
        local_terms(c)
    for c in range(TB // CHUNK):
        outputs(c)


def _hgrn(proj, lb, gn, batch, seq, weights):
    t = proj.shape[0]
    nb = seq // TB
    steps = batch * nb
    for w, _ in weights:
        assert w.shape[0] % (steps * BF16_SUBLANES) == 0, w.shape

    def col(cb):
        return pl.BlockSpec((TB, GROUP), lambda b, i: (b * nb + i, cb))

    def slab(w):
        return pl.BlockSpec((w.shape[0] // steps, w.shape[1]), lambda b, i: (b * nb + i, 0))

    def cast_shape(w, tile):
        return w.shape if tile is None else (w.shape[1] // tile, w.shape[0], tile)

    def cast_slab(w, tile):
        if tile is None:
            return slab(w)
        return pl.BlockSpec((w.shape[1] // tile, w.shape[0] // steps, tile),
                            lambda b, i: (0, b * nb + i, 0))

    return pl.pallas_call(
        functools.partial(_hgrn_kernel, len(weights)),
        grid=(batch, nb),
        in_specs=[col(3), col(4), col(5), col(6),
                  pl.BlockSpec((1, GROUP), lambda b, i: (0, 0)),
                  pl.BlockSpec((1, HEAD_DIM), lambda b, i: (0, 0))]
        + [slab(w) for w, _ in weights],
        out_specs=[pl.BlockSpec((TB, GROUP), lambda b, i: (b * nb + i, 0))]
        + [cast_slab(w, tile) for w, tile in weights],
        out_shape=[jax.ShapeDtypeStruct((t, GROUP), BF16)]
        + [jax.ShapeDtypeStruct(cast_shape(w, tile), BF16) for w, tile in weights],
        scratch_shapes=[
            pltpu.VMEM((HEADS, HEAD_DIM, HEAD_DIM), F32),
            pltpu.VMEM((TB, GROUP), BF16),
            pltpu.VMEM((HEADS, TB, CHUNK), BF16),
            pltpu.VMEM((TB // CHUNK, HEADS, HEAD_DIM, HEAD_DIM), F32),
            pltpu.VMEM((TB // CHUNK, 1, GROUP), F32),
        ],
        compiler_params=_params("arbitrary", "arbitrary"),
        name="hgrn2",
    )(proj, proj, proj, proj, lb, gn, *[w for w, _ in weights])


OUT_COLS = 512


def _outproj_kernel(x_ref, a_ref, r_ref, w_ref, o_ref):
    mixed = jnp.concatenate([a_ref[...], r_ref[...]], axis=1)
    for n in range(0, o_ref.shape[1], OUT_COLS):
        cols = slice(n, n + OUT_COLS)
        o_ref[:, cols] = x_ref[:, cols] + jnp.dot(mixed, w_ref[:, cols],
                                                  preferred_element_type=F32)


def _outproj(x2d, attn, rec, w_bf16, tm):
    t, d = x2d.shape
    g = attn.shape[1]
    return pl.pallas_call(
        _outproj_kernel,
        grid=(t // tm,),
        in_specs=[
            pl.BlockSpec((tm, d), lambda i: (i, 0)),
            pl.BlockSpec((tm, g), lambda i: (i, 0)),
            pl.BlockSpec((tm, g), lambda i: (i, 0)),
            pl.BlockSpec((2 * g, d), lambda i: (0, 0), pipeline_mode=pl.Buffered(1)),
        ],
        out_specs=pl.BlockSpec((tm, d), lambda i: (i, 0)),
        out_shape=jax.ShapeDtypeStruct((t, d), F32),
        compiler_params=_params("arbitrary"),
        name="outproj",
    )(x2d, attn, rec, w_bf16)


FFN_DOWN_COLS = 512
FFN_HIDDEN_TILE = 512


def _ffn_kernel(x_hbm, g2_ref, wg_hbm, wu_hbm, wd_hbm, gf_ref, out_hbm,
                acc_ref, h_ref, wg_buf, wu_buf, wd_buf, x_sem, o_sem, w_sem):
    _, tm, d = acc_ref.shape
    n_tiles, _, tf = wg_hbm.shape
    n_groups = 1 + (n_tiles - 1) // 2
    n_row_tiles = x_hbm.shape[0] // tm
    col_chunks = [slice(n, n + FFN_DOWN_COLS) for n in range(0, d, FFN_DOWN_COLS)]

    def x_copy(i, slot):
        return pltpu.make_async_copy(x_hbm.at[pl.ds(i * tm, tm), :], acc_ref.at[slot],
                                     x_sem.at[slot])

    def out_copy(i, slot):
        return pltpu.make_async_copy(acc_ref.at[slot], out_hbm.at[pl.ds(i * tm, tm), :],
                                     o_sem.at[slot])

    def group_tiles(k):
        return (0,) if isinstance(k, int) and k == 0 else (2 * k - 1, 2 * k)

    def group_copies(k, slot):
        copies = []
        for p, j in enumerate(group_tiles(k)):
            copies += [
                pltpu.make_async_copy(wg_hbm.at[j], wg_buf.at[slot, p], w_sem.at[0, slot, p]),
                pltpu.make_async_copy(wu_hbm.at[j], wu_buf.at[slot, p], w_sem.at[1, slot, p]),
                pltpu.make_async_copy(wd_hbm.at[pl.ds(j * tf, tf), :],
                                      wd_buf.at[slot, pl.ds(p * tf, tf), :], w_sem.at[2, slot, p])]
        return copies

    def swiglu_down(h, slot, n):
        acts = []
        for p in range(n):
            gate = jnp.dot(h, wg_buf[slot, p], preferred_element_type=F32)
            up = jnp.dot(h, wu_buf[slot, p], preferred_element_type=F32)
            acts.append((gate * jax.nn.sigmoid(gate) * up).astype(BF16))
        act = acts[0] if n == 1 else jnp.concatenate(acts, axis=1)
        wd = wd_buf.at[slot]
        return [jnp.dot(act, wd[:n * tf, cols], preferred_element_type=F32) for cols in col_chunks]

    def row_tile(i, carry):
        slot = lax.rem(i, 2)
        acc = acc_ref.at[slot]
        x_copy(i, slot).wait()

        for c in group_copies(0, 0):
            c.wait()
        for c in group_copies(1, 1):
            c.start()
        g2 = g2_ref[...]
        for rows in _row_slabs(tm):
            h = _rms_rows(acc[rows, :], g2).astype(h_ref.dtype)
            h_ref[rows, :] = h
            for cols, y in zip(col_chunks, swiglu_down(h, 0, 1)):
                acc[rows, cols] += y

        @pl.when(i >= 1)
        def _():
            out_copy(i - 1, 1 - slot).wait()

        @pl.when(i + 1 < n_row_tiles)
        def _():
            x_copy(i + 1, 1 - slot).start()

        def pair_group(k, c):
            w_slot = lax.rem(k, 2)
            for cp in group_copies(k, w_slot):
                cp.wait()
            for cp in group_copies(k + 1, 1 - w_slot):
                cp.start()
            for cols, y in zip(col_chunks, swiglu_down(h_ref[...], w_slot, 2)):
                acc[:, cols] += y
            return c

        lax.fori_loop(1, n_groups - 1, pair_group, 0)

        last_slot = (n_groups - 1) % 2
        for c in group_copies(n_groups - 1, last_slot):
            c.wait()

        @pl.when(i + 1 < n_row_tiles)
        def _():
            for c in group_copies(0, 0):
                c.start()

        gf = gf_ref[...]
        for rows in _row_slabs(tm):
            ss = jnp.zeros((EDGE_ROWS, 1), F32)
            for cols, y in zip(col_chunks, swiglu_down(h_ref[rows, :], last_slot, 2)):
                o = acc[rows, cols] + y
                acc[rows, cols] = o
                ss = ss + (o * o).sum(axis=-1, keepdims=True)
            acc[rows, :] = acc[rows, :] * lax.rsqrt(ss / d + EPS) * gf
        out_copy(i, slot).start()
        return carry

    x_copy(0, 0).start()
    for c in group_copies(0, 0):
        c.start()
    lax.fori_loop(0, n_row_tiles, row_tile, 0)
    out_copy(n_row_tiles - 1, (n_row_tiles - 1) % 2).wait()


def _ffn(x1, g2, wg_tiles, wu_tiles, wd, gf, tm):
    t, d = x1.shape
    n_tiles, _, tf = wg_tiles.shape
    assert n_tiles % 2 == 1 and n_tiles >= 7 and (1 + (n_tiles - 1) // 2) % 2 == 0, n_tiles
    assert t % tm == 0 and tm % EDGE_ROWS == 0
    hbm = pl.BlockSpec(memory_space=pl.ANY)
    whole = pl.BlockSpec(memory_space=pltpu.VMEM)
    return pl.pallas_call(
        _ffn_kernel,
        in_specs=[hbm, whole, hbm, hbm, hbm, whole],
        out_specs=hbm,
        out_shape=jax.ShapeDtypeStruct((t, d), F32),
        scratch_shapes=[
            pltpu.VMEM((2, tm, d), F32),
            pltpu.VMEM((tm, d), BF16),
            pltpu.VMEM((2, 2, d, tf), BF16),
            pltpu.VMEM((2, 2, d, tf), BF16),
            pltpu.VMEM((2, 2 * tf, d), BF16),
            pltpu.SemaphoreType.DMA((2,)),
            pltpu.SemaphoreType.DMA((2,)),
            pltpu.SemaphoreType.DMA((3, 2, 2)),
        ],
        compiler_params=pltpu.CompilerParams(vmem_limit_bytes=VMEM_LIMIT_BYTES),
        name="ffn",
    )(x1, g2, wg_tiles, wu_tiles, wd, gf)


def kernel(x, norm1_gain, w_in, rel_bias, lower_bounds, grn_norm_gain, attn_out_gain, w_out,
           norm2_gain, w_gate, w_up, w_down, final_gain):
    batch, seq, d = x.shape
    depth = w_in.shape[0]
    assert seq % TB == 0 and seq % QB == 0 and w_in.shape[2] == 7 * GROUP and d == 2 * GROUP
    lb_all = jnp.cumsum(jax.nn.softmax(lower_bounds.astype(F32), axis=0), axis=0)
    xf = x.reshape(batch * seq, d)
    col_scale = jnp.concatenate([jnp.full((1, GROUP), QUERY_SCALE, F32),
                                 jnp.ones((1, w_in.shape[2] - GROUP), F32)], axis=1)
    for l in range(depth):
        proj = _inproj(xf, norm1_gain[l][None, :], w_in[l].astype(BF16), col_scale,
                       tm=1024, tn=1792)
        attn = _attention(proj, _bias_rows(rel_bias[l]), attn_out_gain[l][None, :], batch, seq)
        rec, wo, wg, wu, wd = _hgrn(proj, lb_all[l][None, :], grn_norm_gain[l][None, :], batch, seq,
                                    [(w_out[l], None), (w_gate[l], FFN_HIDDEN_TILE),
                                     (w_up[l], FFN_HIDDEN_TILE), (w_down[l], None)])
        x1 = _outproj(xf, attn, rec, wo, tm=1024)
        last = l == depth - 1
        assert last, "the final rmsnorm is fused into the last layer's FFN"
        xf = _ffn(x1, norm2_gain[l][None, :], wg, wu, wd, final_gain[None, :], tm=1024)
    return xf.reshape(batch, seq, d)
```

```python
import functools

import jax
import jax.numpy as jnp
from jax import lax
from jax.experimental import pallas as pl
from jax.experimental.pallas import tpu as pltpu

F32 = jnp.float32
BF16 = jnp.bfloat16

EPS = 1e-6
CHUNK = 64
N_LEFT = 8
MAX_REL = 128
HEADS = 8
HEAD_DIM = 128
GROUP = HEADS * HEAD_DIM

VMEM_LIMIT_BYTES = 56 * 1024 * 1024
BF16_SUBLANES = 16
NEG = -1e30
LOG2E = 1.4426950408889634
QUERY_SCALE = HEAD_DIM ** -0.5 * LOG2E


def _params(*sem):
    return pltpu.CompilerParams(dimension_semantics=sem, vmem_limit_bytes=VMEM_LIMIT_BYTES)


def _rms_rows(x, gain):
    ms = jnp.mean(x * x, axis=-1, keepdims=True)
    return x * lax.rsqrt(ms + EPS) * gain


EDGE_ROWS = 256


def _row_slabs(n_rows):
    return [slice(r, r + EDGE_ROWS) for r in range(0, n_rows, EDGE_ROWS)]


def _inproj_kernel(x_ref, gain_ref, w_ref, cs_ref, o_ref, h_ref):
    j = pl.program_id(1)
    col_scale = cs_ref[...]

    def project(h):
        return (jnp.dot(h, w_ref[...], preferred_element_type=F32) * col_scale).astype(o_ref.dtype)

    @pl.when(j == 0)
    def _():
        gain = gain_ref[...]
        for rows in _row_slabs(x_ref.shape[0]):
            h = _rms_rows(x_ref[rows, :], gain).astype(h_ref.dtype)
            h_ref[rows, :] = h
            o_ref[rows, :] = project(h)

    @pl.when(j > 0)
    def _():
        o_ref[...] = project(h_ref[...])


def _inproj(x2d, gain, w_bf16, col_scale, tm, tn):
    t, d = x2d.shape
    n = w_bf16.shape[1]
    return pl.pallas_call(
        _inproj_kernel,
        grid=(t // tm, n // tn),
        in_specs=[
            pl.BlockSpec((tm, d), lambda i, j: (i, 0)),
            pl.BlockSpec((1, d), lambda i, j: (0, 0)),
            pl.BlockSpec((d, tn), lambda i, j: (0, j)),
            pl.BlockSpec((1, tn), lambda i, j: (0, j)),
        ],
        out_specs=pl.BlockSpec((tm, tn), lambda i, j: (i, j)),
        out_shape=jax.ShapeDtypeStruct((t, n), BF16),
        scratch_shapes=[pltpu.VMEM((tm, d), BF16)],
        compiler_params=_params("arbitrary", "arbitrary"),
        name="inproj",
    )(x2d, gain, w_bf16, col_scale)


QB = 256
ATTN_ROWS = 1024
N_SEG = 1 + (N_LEFT * CHUNK) // QB
WIN = N_SEG * QB
ROLL_W = 1024


def _bias_rows(rel_bias):
    h = rel_bias.shape[0]
    far_past = rel_bias[:, 2 * MAX_REL:2 * MAX_REL + 1]
    far_future = rel_bias[:, 0:1]
    n_head = (WIN - QB) - MAX_REL
    n_tail = WIN - n_head - (2 * MAX_REL + 1)
    return jnp.concatenate([
        jnp.broadcast_to(far_past, (h, n_head)),
        rel_bias[:, ::-1],
        jnp.broadcast_to(far_future, (h, n_tail)),
        jnp.broadcast_to(far_past, (h, ROLL_W - WIN)),
    ], axis=1).astype(F32)


def _attn_kernel(q_ref, kp_ref, kc_ref, vp_ref, vc_ref, brow_ref, gain_ref,
                 o_ref, bias_ref, acc_ref, s_ref):
    step = pl.program_id(1)
    n_blocks = ATTN_ROWS // QB
    first_granule = n_blocks - (N_SEG - 1)

    @pl.when((pl.program_id(0) == 0) & (step == 0))
    def _():
        qc = lax.broadcasted_iota(jnp.int32, (QB, WIN), 0) // CHUNK
        kc = lax.broadcasted_iota(jnp.int32, (QB, WIN), 1) // CHUNK
        band = (kc >= qc) & (kc <= qc + N_LEFT)
        for h in range(HEADS):
            row = jnp.broadcast_to(brow_ref[h:h + 1, :] * LOG2E, (QB, ROLL_W))
            toeplitz = pltpu.roll(row, 0, 1, stride=1, stride_axis=0)
            bias_ref[h] = jnp.where(band, toeplitz[:, :WIN], NEG)

    def granule(prev_ref, cur_ref, r, sl):
        ref = (prev_ref, cur_ref)[r // n_blocks]
        start = (r % n_blocks) * QB
        return ref[start:start + QB, sl]

    def lane_halves(op, x):
        while x.shape[1] > HEAD_DIM:
            half = x.shape[1] // 2
            x = op(x[:, :half], x[:, half:])
        return x

    def attend(blk, segs):
        rows = slice(blk * QB, (blk + 1) * QB)
        m = []
        for h in range(HEADS):
            sl = slice(h * HEAD_DIM, (h + 1) * HEAD_DIM)
            qh = q_ref[rows, sl]
            mx = None
            for j in segs:
                cols = slice(j * QB, (j + 1) * QB)
                sj = lax.dot_general(qh, granule(kp_ref, kc_ref, first_granule + blk + j, sl),
                                     (((1,), (1,)), ((), ())),
                                     preferred_element_type=F32) + bias_ref[h, :, cols]
                s_ref[h, :, cols] = sj
                mx = sj if mx is None else jnp.maximum(mx, sj)
            m.append(lane_halves(jnp.maximum, mx).max(axis=-1, keepdims=True))
        sq = jnp.zeros((QB, HEAD_DIM), F32)
        ones = jnp.ones((QB, HEAD_DIM), BF16)
        for h in range(HEADS):
            sl = slice(h * HEAD_DIM, (h + 1) * HEAD_DIM)
            p = jnp.concatenate(
                [jnp.exp2(s_ref[h, :, j * QB:(j + 1) * QB] - m[h]).astype(BF16) for j in segs], axis=1)
            v1 = jnp.concatenate(
                [jnp.concatenate([granule(vp_ref, vc_ref, first_granule + blk + j, sl), ones], axis=1)
                 for j in segs], axis=0)
            pv = jnp.dot(p, v1, preferred_element_type=F32)
            o = pv[:, :HEAD_DIM] * (1.0 / pv[:, HEAD_DIM:])
            acc_ref[rows, sl] = o
            sq = sq + o * o
        inv = lax.rsqrt(sq.sum(axis=-1, keepdims=True) / GROUP + EPS)
        o_ref[rows, :] = (acc_ref[rows, :] * inv * gain_ref[...]).astype(o_ref.dtype)

    @pl.when(step == 0)
    def _():
        for blk in range(n_blocks):
            attend(blk, tuple(range(max(N_SEG - 1 - blk, 0), N_SEG)))

    @pl.when(step > 0)
    def _():
        for blk in range(n_blocks):
            attend(blk, tuple(range(N_SEG)))


def _attention(proj, brow, gain, batch, seq):
    t = proj.shape[0]
    nb = seq // ATTN_ROWS
    assert seq % ATTN_ROWS == 0 and ATTN_ROWS % QB == 0 and (N_SEG - 1) * QB <= ATTN_ROWS

    def rows(col_block, back=0):
        return pl.BlockSpec((ATTN_ROWS, GROUP),
                            lambda b, i: (b * nb + jnp.maximum(i - back, 0), col_block))

    return pl.pallas_call(
        _attn_kernel,
        grid=(batch, nb),
        in_specs=[rows(0), rows(1, back=1), rows(1), rows(2, back=1), rows(2),
                  pl.BlockSpec((HEADS, ROLL_W), lambda b, i: (0, 0)),
                  pl.BlockSpec((1, GROUP), lambda b, i: (0, 0))],
        out_specs=rows(0),
        out_shape=jax.ShapeDtypeStruct((t, GROUP), BF16),
        scratch_shapes=[pltpu.VMEM((HEADS, QB, WIN), F32),
                        pltpu.VMEM((ATTN_ROWS, GROUP), F32),
                        pltpu.VMEM((HEADS, QB, WIN), F32)],
        compiler_params=_params("arbitrary", "arbitrary"),
        name="attention",
    )(proj, proj, proj, proj, proj, brow, gain)


TB = 512


def _silu(x):
    hx = 0.5 * x
    return hx + hx * jnp.tanh(hx)


def _split3(x):
    hi = x.astype(BF16)
    r1 = x - hi.astype(F32)
    mid = r1.astype(BF16)
    lo = (r1 - mid.astype(F32)).astype(BF16)
    return hi, mid, lo


def _hgrn_kernel(n_cast, q_ref, f_ref, i_ref, g_ref, lb_ref, gn_ref, *refs):
    cast_in, (o_ref, *cast_out) = refs[:n_cast], refs[n_cast:2 * n_cast + 1]
    st_ref, qd_ref, a_ref, u_ref, dec_ref = refs[2 * n_cast + 1:]

    for src, dst in zip(cast_in, cast_out):
        if len(dst.shape) == 2:
            dst[...] = src[...].astype(dst.dtype)
        else:
            width = dst.shape[2]
            for n in range(dst.shape[0]):
                dst[n] = src[:, n * width:(n + 1) * width].astype(dst.dtype)

    @pl.when(pl.program_id(1) == 0)
    def _():
        st_ref[...] = jnp.zeros_like(st_ref)

    lb = lb_ref[...]
    f_mid = 0.5 * (1.0 + lb)
    f_amp = 0.5 * (1.0 - lb)
    gn = gn_ref[...]
    r_i = lax.broadcasted_iota(jnp.int32, (CHUNK, CHUNK), 0)
    c_i = lax.broadcasted_iota(jnp.int32, (CHUNK, CHUNK), 1)
    causal = r_i >= c_i
    tril = causal.astype(BF16)
    tril3 = jnp.concatenate([tril, tril, tril], axis=1)

    def chunk_rows(c):
        return slice(c * CHUNK, (c + 1) * CHUNK)

    def local_terms(c):
        rows = chunk_rows(c)
        q = _silu(q_ref[rows, :].astype(F32))
        f = f_mid + f_amp * jnp.tanh(0.5 * f_ref[rows, :].astype(F32))
        k = 1.0 - f
        b = jnp.dot(tril3, jnp.concatenate(_split3(jnp.log2(f)), axis=0),
                    preferred_element_type=F32)
        decay = jnp.exp2(b[CHUNK - 1:CHUNK, :])
        q_dec = (q * jnp.exp2(b)).astype(BF16)
        k_undecayed = k * jnp.exp2(-b)
        k_intra = k_undecayed.astype(BF16)
        k_state = (k_undecayed * decay).astype(BF16)
        qd_ref[rows, :] = q_dec
        dec_ref[c] = decay
        for h in range(HEADS):
            sl = slice(h * HEAD_DIM, (h + 1) * HEAD_DIM)
            a = lax.dot_general(q_dec[:, sl], k_intra[:, sl], (((1,), (1,)), ((), ())),
                                preferred_element_type=F32)
            a_ref[h, rows, :] = jnp.where(causal, a, 0.0).astype(BF16)
            u_ref[c, h] = lax.dot_general(i_ref[rows, sl], k_state[:, sl], (((0,), (0,)), ((), ())),
                                          preferred_element_type=F32)

    def outputs(c):
        rows = chunk_rows(c)
        decay = dec_ref[c]
        for h in range(HEADS):
            sl = slice(h * HEAD_DIM, (h + 1) * HEAD_DIM)
            st = st_ref[h]
            o = jnp.dot(a_ref[h, rows, :], i_ref[rows, sl], preferred_element_type=F32)
            o = o + lax.dot_general(qd_ref[rows, sl], st.astype(BF16), (((1,), (1,)), ((), ())),
                                    preferred_element_type=F32)
            st_ref[h] = st * decay[:, sl] + u_ref[c, h]
            o = _rms_rows(o, gn)
            o_ref[rows, sl] = (o * _silu(g_ref[rows, sl].astype(F32))).astype(o_ref.dtype)

    for c in range(TB // CHUNK):
        local_terms(c)
    for c in range(TB // CHUNK):
        outputs(c)


def _hgrn(proj, lb, gn, batch, seq, weights):
    t = proj.shape[0]
    nb = seq // TB
    steps = batch * nb
    for w, _ in weights:
        assert w.shape[0] % (steps * BF16_SUBLANES) == 0, w.shape

    def col(cb):
        return pl.BlockSpec((TB, GROUP), lambda b, i: (b * nb + i, cb))

    def slab(w):
        return pl.BlockSpec((w.shape[0] // steps, w.shape[1]), lambda b, i: (b * nb + i, 0))

    def cast_shape(w, tile):
        return w.shape if tile is None else (w.shape[1] // tile, w.shape[0], tile)

    def cast_slab(w, tile):
        if tile is None:
            return slab(w)
        return pl.BlockSpec((w.shape[1] // tile, w.shape[0] // steps, tile),
                            lambda b, i: (0, b * nb + i, 0))

    return pl.pallas_call(
        functools.partial(_hgrn_kernel, len(weights)),
        grid=(batch, nb),
        in_specs=[col(3), col(4), col(5), col(6),
                  pl.BlockSpec((1, GROUP), lambda b, i: (0, 0)),
                  pl.BlockSpec((1, HEAD_DIM), lambda b, i: (0, 0))]
        + [slab(w) for w, _ in weights],
        out_specs=[pl.BlockSpec((TB, GROUP), lambda b, i: (b * nb + i, 0))]
        + [cast_slab(w, tile) for w, tile in weights],
        out_shape=[jax.ShapeDtypeStruct((t, GROUP), BF16)]
        + [jax.ShapeDtypeStruct(cast_shape(w, tile), BF16) for w, tile in weights],
        scratch_shapes=[
            pltpu.VMEM((HEADS, HEAD_DIM, HEAD_DIM), F32),
            pltpu.VMEM((TB, GROUP), BF16),
            pltpu.VMEM((HEADS, TB, CHUNK), BF16),
            pltpu.VMEM((TB // CHUNK, HEADS, HEAD_DIM, HEAD_DIM), F32),
            pltpu.VMEM((TB // CHUNK, 1, GROUP), F32),
        ],
        compiler_params=_params("arbitrary", "arbitrary"),
        name="hgrn2",
    )(proj, proj, proj, proj, lb, gn, *[w for w, _ in weights])


OUT_COLS = 512


def _outproj_kernel(x_ref, a_ref, r_ref, w_ref, o_ref):
    mixed = jnp.concatenate([a_ref[...], r_ref[...]], axis=1)
    for n in range(0, o_ref.shape[1], OUT_COLS):
        cols = slice(n, n + OUT_COLS)
        o_ref[:, cols] = x_ref[:, cols] + jnp.dot(mixed, w_ref[:, cols],
                                                  preferred_element_type=F32)


def _outproj(x2d, attn, rec, w_bf16, tm):
    t, d = x2d.shape
    g = attn.shape[1]
    return pl.pallas_call(
        _outproj_kernel,
        grid=(t // tm,),
        in_specs=[
            pl.BlockSpec((tm, d), lambda i: (i, 0)),
            pl.BlockSpec((tm, g), lambda i: (i, 0)),
            pl.BlockSpec((tm, g), lambda i: (i, 0)),
            pl.BlockSpec((2 * g, d), lambda i: (0, 0), pipeline_mode=pl.Buffered(1)),
        ],
        out_specs=pl.BlockSpec((tm, d), lambda i: (i, 0)),
        out_shape=jax.ShapeDtypeStruct((t, d), F32),
        compiler_params=_params("arbitrary"),
        name="outproj",
    )(x2d, attn, rec, w_bf16)


FFN_DOWN_COLS = 512
FFN_HIDDEN_TILE = 512
ROW_TILE_DMA_PRIORITY = 1


def _ffn_kernel(x_hbm, g2_ref, wg_hbm, wu_hbm, wd_hbm, gf_ref, out_hbm,
                acc_ref, h_ref, wg_buf, wu_buf, wd_buf, x_sem, o_sem, w_sem):
    _, tm, d = acc_ref.shape
    n_tiles, _, tf = wg_hbm.shape
    n_groups = 1 + (n_tiles - 1) // 2
    n_row_tiles = x_hbm.shape[0] // tm
    col_chunks = [slice(n, n + FFN_DOWN_COLS) for n in range(0, d, FFN_DOWN_COLS)]

    def x_copy(i, slot):
        return pltpu.make_async_copy(x_hbm.at[pl.ds(i * tm, tm), :], acc_ref.at[slot],
                                     x_sem.at[slot])

    def out_copy(i, slot):
        return pltpu.make_async_copy(acc_ref.at[slot], out_hbm.at[pl.ds(i * tm, tm), :],
                                     o_sem.at[slot])

    def group_tiles(k):
        return (0,) if isinstance(k, int) and k == 0 else (2 * k - 1, 2 * k)

    def group_copies(k, slot):
        copies = []
        for p, j in enumerate(group_tiles(k)):
            copies += [
                pltpu.make_async_copy(wg_hbm.at[j], wg_buf.at[slot, p], w_sem.at[0, slot, p]),
                pltpu.make_async_copy(wu_hbm.at[j], wu_buf.at[slot, p], w_sem.at[1, slot, p]),
                pltpu.make_async_copy(wd_hbm.at[pl.ds(j * tf, tf), :],
                                      wd_buf.at[slot, pl.ds(p * tf, tf), :], w_sem.at[2, slot, p])]
        return copies

    def swiglu_down(h, slot, n):
        acts = []
        for p in range(n):
            gate = jnp.dot(h, wg_buf[slot, p], preferred_element_type=F32)
            up = jnp.dot(h, wu_buf[slot, p], preferred_element_type=F32)
            acts.append((gate * jax.nn.sigmoid(gate) * up).astype(BF16))
        act = acts[0] if n == 1 else jnp.concatenate(acts, axis=1)
        wd = wd_buf.at[slot]
        return [jnp.dot(act, wd[:n * tf, cols], preferred_element_type=F32) for cols in col_chunks]

    def row_tile(i, carry):
        slot = lax.rem(i, 2)
        acc = acc_ref.at[slot]
        x_copy(i, slot).wait()

        for c in group_copies(0, 0):
            c.wait()
        for c in group_copies(1, 1):
            c.start()
        g2 = g2_ref[...]
        for rows in _row_slabs(tm):
            h = _rms_rows(acc[rows, :], g2).astype(h_ref.dtype)
            h_ref[rows, :] = h
            for cols, y in zip(col_chunks, swiglu_down(h, 0, 1)):
                acc[rows, cols] += y

        @pl.when(i >= 1)
        def _():
            out_copy(i - 1, 1 - slot).wait()

        @pl.when(i + 1 < n_row_tiles)
        def _():
            x_copy(i + 1, 1 - slot).start(priority=ROW_TILE_DMA_PRIORITY)

        def pair_group(k, c):
            w_slot = lax.rem(k, 2)
            for cp in group_copies(k, w_slot):
                cp.wait()
            for cp in group_copies(k + 1, 1 - w_slot):
                cp.start()
            for cols, y in zip(col_chunks, swiglu_down(h_ref[...], w_slot, 2)):
                acc[:, cols] += y
            return c

        lax.fori_loop(1, n_groups - 1, pair_group, 0)

        last_slot = (n_groups - 1) % 2
        for c in group_copies(n_groups - 1, last_slot):
            c.wait()

        @pl.when(i + 1 < n_row_tiles)
        def _():
            for c in group_copies(0, 0):
                c.start()

        gf = gf_ref[...]
        for rows in _row_slabs(tm):
            ss = jnp.zeros((EDGE_ROWS, 1), F32)
            for cols, y in zip(col_chunks, swiglu_down(h_ref[rows, :], last_slot, 2)):
                o = acc[rows, cols] + y
                acc[rows, cols] = o
                ss = ss + (o * o).sum(axis=-1, keepdims=True)
            acc[rows, :] = acc[rows, :] * lax.rsqrt(ss / d + EPS) * gf
        out_copy(i, slot).start(priority=ROW_TILE_DMA_PRIORITY)
        return carry

    x_copy(0, 0).start()
    for c in group_copies(0, 0):
        c.start()
    lax.fori_loop(0, n_row_tiles, row_tile, 0)
    out_copy(n_row_tiles - 1, (n_row_tiles - 1) % 2).wait()


def _ffn(x1, g2, wg_tiles, wu_tiles, wd, gf, tm):
    t, d = x1.shape
    n_tiles, _, tf = wg_tiles.shape
    assert n_tiles % 2 == 1 and n_tiles >= 7 and (1 + (n_tiles - 1) // 2) % 2 == 0, n_tiles
    assert t % tm == 0 and tm % EDGE_ROWS == 0
    hbm = pl.BlockSpec(memory_space=pl.ANY)
    whole = pl.BlockSpec(memory_space=pltpu.VMEM)
    return pl.pallas_call(
        _ffn_kernel,
        in_specs=[hbm, whole, hbm, hbm, hbm, whole],
        out_specs=hbm,
        out_shape=jax.ShapeDtypeStruct((t, d), F32),
        scratch_shapes=[
            pltpu.VMEM((2, tm, d), F32),
            pltpu.VMEM((tm, d), BF16),
            pltpu.VMEM((2, 2, d, tf), BF16),
            pltpu.VMEM((2, 2, d, tf), BF16),
            pltpu.VMEM((2, 2 * tf, d), BF16),
            pltpu.SemaphoreType.DMA((2,)),
            pltpu.SemaphoreType.DMA((2,)),
            pltpu.SemaphoreType.DMA((3, 2, 2)),
        ],
        compiler_params=pltpu.CompilerParams(vmem_limit_bytes=VMEM_LIMIT_BYTES),
        name="ffn",
    )(x1, g2, wg_tiles, wu_tiles, wd, gf)


def kernel(x, norm1_gain, w_in, rel_bias, lower_bounds, grn_norm_gain, attn_out_gain, w_out,
           norm2_gain, w_gate, w_up, w_down, final_gain):
    batch, seq, d = x.shape
    depth = w_in.shape[0]
    assert seq % TB == 0 and seq % QB == 0 and w_in.shape[2] == 7 * GROUP and d == 2 * GROUP
    lb_all = jnp.cumsum(jax.nn.softmax(lower_bounds.astype(F32), axis=0), axis=0)
    xf = x.reshape(batch * seq, d)
    col_scale = jnp.concatenate([jnp.full((1, GROUP), QUERY_SCALE, F32),
                                 jnp.ones((1, w_in.shape[2] - GROUP), F32)], axis=1)
    for l in range(depth):
        proj = _inproj(xf, norm1_gain[l][None, :], w_in[l].astype(BF16), col_scale,
                       tm=1024, tn=1792)
        attn = _attention(proj, _bias_rows(rel_bias[l]), attn_out_gain[l][None, :], batch, seq)
        rec, wo, wg, wu, wd = _hgrn(proj, lb_all[l][None, :], grn_norm_gain[l][None, :], batch, seq,
                                    [(w_out[l], None), (w_gate[l], FFN_HIDDEN_TILE),
                                     (w_up[l], FFN_HIDDEN_TILE), (w_down[l], None)])
        x1 = _outproj(xf, attn, rec, wo, tm=1024)
        last = l == depth - 1
        assert last, "the final rmsnorm is fused into the last layer's FFN"
        xf = _ffn(x1, norm2_gain[l][None, :], wg, wu, wd, final_gain[None, :], tm=1024)
    return xf.reshape(batch, seq, d)
```

```python
import functools

import jax
import jax.numpy as jnp
from jax import lax
from jax.experimental import pallas as pl
from jax.experimental.pallas import tpu as pltpu

F32 = jnp.float32
BF16 = jnp.bfloat16

EPS = 1e-6
CHUNK = 64
N_LEFT = 8
MAX_REL = 128
HEADS = 8
HEAD_DIM = 128
GROUP = HEADS * HEAD_DIM

VMEM_LIMIT_BYTES = 56 * 1024 * 1024
BF16_SUBLANES = 16
NEG = -1e30
LOG2E = 1.4426950408889634
QUERY_SCALE = HEAD_DIM ** -0.5 * LOG2E


def _params(*sem):
    return pltpu.CompilerParams(dimension_semantics=sem, vmem_limit_bytes=VMEM_LIMIT_BYTES)


def _rms_rows(x, gain):
    ms = jnp.mean(x * x, axis=-1, keepdims=True)
    return x * lax.rsqrt(ms + EPS) * gain


EDGE_ROWS = 256


def _row_slabs(n_rows):
    return [slice(r, r + EDGE_ROWS) for r in range(0, n_rows, EDGE_ROWS)]


def _inproj_kernel(x_hbm, gain_ref, w_hbm, cs_ref, out_hbm,
                   x_buf, h_ref, w_buf, o_buf, x_sem, w_sem, o_sem):
    _, tm, d = x_buf.shape
    tn = w_buf.shape[2] // 2
    n_groups = 1 + (w_hbm.shape[1] // tn - 1) // 2
    n_row_tiles = x_hbm.shape[0] // tm

    def group_cols(k):
        return (0, 1) if k == 0 else (2 * k - 1, 2)

    def x_copy(i, slot):
        return pltpu.make_async_copy(x_hbm.at[pl.ds(i * tm, tm), :], x_buf.at[slot], x_sem.at[slot])

    def w_copies(k):
        j0, n = group_cols(k)
        return [pltpu.make_async_copy(w_hbm.at[:, pl.ds((j0 + p) * tn, tn)],
                                      w_buf.at[k % 2, :, pl.ds(p * tn, tn)], w_sem.at[k % 2, p])
                for p in range(n)]

    def out_copy(i, k):
        j0, n = group_cols(k)
        return pltpu.make_async_copy(o_buf.at[k % 2, :, pl.ds(0, n * tn)],
                                     out_hbm.at[pl.ds(i * tm, tm), pl.ds(j0 * tn, n * tn)],
                                     o_sem.at[k % 2])

    def row_tile(i, carry):
        x_slot = lax.rem(i, 2)
        x_copy(i, x_slot).wait()

        @pl.when(i + 1 < n_row_tiles)
        def _():
            x_copy(i + 1, 1 - x_slot).start()

        for k in range(n_groups):
            for c in w_copies(k):
                c.wait()
            if k + 1 < n_groups:
                for c in w_copies(k + 1):
                    c.start()
            else:
                @pl.when(i + 1 < n_row_tiles)
                def _():
                    for c in w_copies(0):
                        c.start()

            if k >= 2:
                out_copy(i, k - 2).wait()
            else:
                @pl.when(i >= 1)
                def _():
                    out_copy(i - 1, k + n_groups - 2).wait()

            j0, n = group_cols(k)
            w = w_buf.at[k % 2]
            o = o_buf.at[k % 2]
            col_scale = cs_ref[:, j0 * tn:(j0 + n) * tn]

            def project(h):
                y = jnp.dot(h, w[:, :n * tn], preferred_element_type=F32)
                return (y * col_scale).astype(o.dtype)

            if k == 0:
                gain = gain_ref[...]
                x = x_buf.at[x_slot]
                for rows in _row_slabs(tm):
                    h = _rms_rows(x[rows, :], gain).astype(h_ref.dtype)
                    h_ref[rows, :] = h
                    o[rows, :n * tn] = project(h)
            else:
                o[:, :n * tn] = project(h_ref[...])
            out_copy(i, k).start()
        return carry

    x_copy(0, 0).start()
    for c in w_copies(0):
        c.start()
    lax.fori_loop(0, n_row_tiles, row_tile, 0)
    for k in (n_groups - 2, n_groups - 1):
        out_copy(n_row_tiles - 1, k).wait()


def _inproj(x2d, gain, w_bf16, col_scale, tm, tn):
    t, d = x2d.shape
    n = w_bf16.shape[1]
    n_tiles = n // tn
    assert n % tn == 0 and n_tiles % 2 == 1 and (1 + (n_tiles - 1) // 2) % 2 == 0, n_tiles
    assert t % tm == 0 and tm % EDGE_ROWS == 0
    hbm = pl.BlockSpec(memory_space=pl.ANY)
    whole = pl.BlockSpec(memory_space=pltpu.VMEM)
    return pl.pallas_call(
        _inproj_kernel,
        in_specs=[hbm, whole, hbm, whole],
        out_specs=hbm,
        out_shape=jax.ShapeDtypeStruct((t, n), BF16),
        scratch_shapes=[
            pltpu.VMEM((2, tm, d), F32),
            pltpu.VMEM((tm, d), BF16),
            pltpu.VMEM((2, d, 2 * tn), BF16),
            pltpu.VMEM((2, tm, 2 * tn), BF16),
            pltpu.SemaphoreType.DMA((2,)),
            pltpu.SemaphoreType.DMA((2, 2)),
            pltpu.SemaphoreType.DMA((2,)),
        ],
        compiler_params=pltpu.CompilerParams(vmem_limit_bytes=VMEM_LIMIT_BYTES),
        name="inproj",
    )(x2d, gain, w_bf16, col_scale)


QB = 256
ATTN_ROWS = 1024
N_SEG = 1 + (N_LEFT * CHUNK) // QB
WIN = N_SEG * QB
ROLL_W = 1024


def _bias_rows(rel_bias):
    h = rel_bias.shape[0]
    far_past = rel_bias[:, 2 * MAX_REL:2 * MAX_REL + 1]
    far_future = rel_bias[:, 0:1]
    n_head = (WIN - QB) - MAX_REL
    n_tail = WIN - n_head - (2 * MAX_REL + 1)
    return jnp.concatenate([
        jnp.broadcast_to(far_past, (h, n_head)),
        rel_bias[:, ::-1],
        jnp.broadcast_to(far_future, (h, n_tail)),
        jnp.broadcast_to(far_past, (h, ROLL_W - WIN)),
    ], axis=1).astype(F32)


def _attn_kernel(q_ref, kp_ref, kc_ref, vp_ref, vc_ref, brow_ref, gain_ref,
                 o_ref, bias_ref, acc_ref, s_ref):
    step = pl.program_id(1)
    n_blocks = ATTN_ROWS // QB
    first_granule = n_blocks - (N_SEG - 1)

    @pl.when((pl.program_id(0) == 0) & (step == 0))
    def _():
        qc = lax.broadcasted_iota(jnp.int32, (QB, WIN), 0) // CHUNK
        kc = lax.broadcasted_iota(jnp.int32, (QB, WIN), 1) // CHUNK
        band = (kc >= qc) & (kc <= qc + N_LEFT)
        for h in range(HEADS):
            row = jnp.broadcast_to(brow_ref[h:h + 1, :] * LOG2E, (QB, ROLL_W))
            toeplitz = pltpu.roll(row, 0, 1, stride=1, stride_axis=0)
            bias_ref[h] = jnp.where(band, toeplitz[:, :WIN], NEG)

    def granule(prev_ref, cur_ref, r, sl):
        ref = (prev_ref, cur_ref)[r // n_blocks]
        start = (r % n_blocks) * QB
        return ref[start:start + QB, sl]

    def lane_halves(op, x):
        while x.shape[1] > HEAD_DIM:
            half = x.shape[1] // 2
            x = op(x[:, :half], x[:, half:])
        return x

    def attend(blk, segs):
        rows = slice(blk * QB, (blk + 1) * QB)
        m = []
        for h in range(HEADS):
            sl = slice(h * HEAD_DIM, (h + 1) * HEAD_DIM)
            qh = q_ref[rows, sl]
            mx = None
            for j in segs:
                cols = slice(j * QB, (j + 1) * QB)
                sj = lax.dot_general(qh, granule(kp_ref, kc_ref, first_granule + blk + j, sl),
                                     (((1,), (1,)), ((), ())),
                                     preferred_element_type=F32) + bias_ref[h, :, cols]
                s_ref[h, :, cols] = sj
                mx = sj if mx is None else jnp.maximum(mx, sj)
            m.append(lane_halves(jnp.maximum, mx).max(axis=-1, keepdims=True))
        sq = jnp.zeros((QB, HEAD_DIM), F32)
        ones = jnp.ones((QB, HEAD_DIM), BF16)
        for h in range(HEADS):
            sl = slice(h * HEAD_DIM, (h + 1) * HEAD_DIM)
            p = jnp.concatenate(
                [jnp.exp2(s_ref[h, :, j * QB:(j + 1) * QB] - m[h]).astype(BF16) for j in segs], axis=1)
            v1 = jnp.concatenate(
                [jnp.concatenate([granule(vp_ref, vc_ref, first_granule + blk + j, sl), ones], axis=1)
                 for j in segs], axis=0)
            pv = jnp.dot(p, v1, preferred_element_type=F32)
            o = pv[:, :HEAD_DIM] * (1.0 / pv[:, HEAD_DIM:])
            acc_ref[rows, sl] = o
            sq = sq + o * o
        inv = lax.rsqrt(sq.sum(axis=-1, keepdims=True) / GROUP + EPS)
        o_ref[rows, :] = (acc_ref[rows, :] * inv * gain_ref[...]).astype(o_ref.dtype)

    @pl.when(step == 0)
    def _():
        for blk in range(n_blocks):
            attend(blk, tuple(range(max(N_SEG - 1 - blk, 0), N_SEG)))

    @pl.when(step > 0)
    def _():
        for blk in range(n_blocks):
            attend(blk, tuple(range(N_SEG)))


def _attention(proj, brow, gain, batch, seq):
    t = proj.shape[0]
    nb = seq // ATTN_ROWS
    assert seq % ATTN_ROWS == 0 and ATTN_ROWS % QB == 0 and (N_SEG - 1) * QB <= ATTN_ROWS

    def rows(col_block, back=0):
        return pl.BlockSpec((ATTN_ROWS, GROUP),
                            lambda b, i: (b * nb + jnp.maximum(i - back, 0), col_block))

    return pl.pallas_call(
        _attn_kernel,
        grid=(batch, nb),
        in_specs=[rows(0), rows(1, back=1), rows(1), rows(2, back=1), rows(2),
                  pl.BlockSpec((HEADS, ROLL_W), lambda b, i: (0, 0)),
                  pl.BlockSpec((1, GROUP), lambda b, i: (0, 0))],
        out_specs=rows(0),
        out_shape=jax.ShapeDtypeStruct((t, GROUP), BF16),
        scratch_shapes=[pltpu.VMEM((HEADS, QB, WIN), F32),
                        pltpu.VMEM((ATTN_ROWS, GROUP), F32),
                        pltpu.VMEM((HEADS, QB, WIN), F32)],
        compiler_params=_params("arbitrary", "arbitrary"),
        name="attention",
    )(proj, proj, proj, proj, proj, brow, gain)


TB = 512


def _silu(x):
    hx = 0.5 * x
    return hx + hx * jnp.tanh(hx)


def _split3(x):
    hi = x.astype(BF16)
    r1 = x - hi.astype(F32)
    mid = r1.astype(BF16)
    lo = (r1 - mid.astype(F32)).astype(BF16)
    return hi, mid, lo


def _hgrn_kernel(n_cast, q_ref, f_ref, i_ref, g_ref, lb_ref, gn_ref, *refs):
    cast_in, (o_ref, *cast_out) = refs[:n_cast], refs[n_cast:2 * n_cast + 1]
    st_ref, qd_ref, a_ref, u_ref, dec_ref = refs[2 * n_cast + 1:]

    for src, dst in zip(cast_in, cast_out):
        if len(dst.shape) == 2:
            dst[...] = src[...].astype(dst.dtype)
        else:
            width = dst.shape[2]
            for n in range(dst.shape[0]):
                dst[n] = src[:, n * width:(n + 1) * width].astype(dst.dtype)

    @pl.when(pl.program_id(1) == 0)
    def _():
        st_ref[...] = jnp.zeros_like(st_ref)

    lb = lb_ref[...]
    f_mid = 0.5 * (1.0 + lb)
    f_amp = 0.5 * (1.0 - lb)
    gn = gn_ref[...]
    r_i = lax.broadcasted_iota(jnp.int32, (CHUNK, CHUNK), 0)
    c_i = lax.broadcasted_iota(jnp.int32, (CHUNK, CHUNK), 1)
    causal = r_i >= c_i
    tril = causal.astype(BF16)
    tril3 = jnp.concatenate([tril, tril, tril], axis=1)

    def chunk_rows(c):
        return slice(c * CHUNK, (c + 1) * CHUNK)

    def local_terms(c):
        rows = chunk_rows(c)
        q = _silu(q_ref[rows, :].astype(F32))
        f = f_mid + f_amp * jnp.tanh(0.5 * f_ref[rows, :].astype(F32))
        k = 1.0 - f
        b = jnp.dot(tril3, jnp.concatenate(_split3(jnp.log2(f)), axis=0),
                    preferred_element_type=F32)
        decay = jnp.exp2(b[CHUNK - 1:CHUNK, :])
        q_dec = (q * jnp.exp2(b)).astype(BF16)
        k_undecayed = k * jnp.exp2(-b)
        k_intra = k_undecayed.astype(BF16)
        k_state = (k_undecayed * decay).astype(BF16)
        qd_ref[rows, :] = q_dec
        dec_ref[c] = decay
        for h in range(HEADS):
            sl = slice(h * HEAD_DIM, (h + 1) * HEAD_DIM)
            a = lax.dot_general(q_dec[:, sl], k_intra[:, sl], (((1,), (1,)), ((), ())),
                                preferred_element_type=F32)
            a_ref[h, rows, :] = jnp.where(causal, a, 0.0).astype(BF16)
            u_ref[c, h] = lax.dot_general(i_ref[rows, sl], k_state[:, sl], (((0,), (0,)), ((), ())),
                                          preferred_element_type=F32)

    def outputs(c):
        rows = chunk_rows(c)
        decay = dec_ref[c]
        for h in range(HEADS):
            sl = slice(h * HEAD_DIM, (h + 1) * HEAD_DIM)
            st = st_ref[h]
            o = jnp.dot(a_ref[h, rows, :], i_ref[rows, sl], preferred_element_type=F32)
            o = o + lax.dot_general(qd_ref[rows, sl], st.astype(BF16), (((1,), (1,)), ((), ())),
                                    preferred_element_type=F32)
            st_ref[h] = st * decay[:, sl] + u_ref[c, h]
            o = _rms_rows(o, gn)
            o_ref[rows, sl] = (o * _silu(g_ref[rows, sl].astype(F32))).astype(o_ref.dtype)

    for c in range(TB // CHUNK):
        local_terms(c)
    for c in range(TB // CHUNK):
        outputs(c)


def _hgrn(proj, lb, gn, batch, seq, weights):
    t = proj.shape[0]
    nb = seq // TB
    steps = batch * nb
    for w, _ in weights:
        assert w.shape[0] % (steps * BF16_SUBLANES) == 0, w.shape

    def col(cb):
        return pl.BlockSpec((TB, GROUP), lambda b, i: (b * nb + i, cb))

    def slab(w):
        return pl.BlockSpec((w.shape[0] // steps, w.shape[1]), lambda b, i: (b * nb + i, 0))

    def cast_shape(w, tile):
        return w.shape if tile is None else (w.shape[1] // tile, w.shape[0], tile)

    def cast_slab(w, tile):
        if tile is None:
            return slab(w)
        return pl.BlockSpec((w.shape[1] // tile, w.shape[0] // steps, tile),
                            lambda b, i: (0, b * nb + i, 0))

    return pl.pallas_call(
        functools.partial(_hgrn_kernel, len(weights)),
        grid=(batch, nb),
        in_specs=[col(3), col(4), col(5), col(6),
                  pl.BlockSpec((1, GROUP), lambda b, i: (0, 0)),
                  pl.BlockSpec((1, HEAD_DIM), lambda b, i: (0, 0))]
        + [slab(w) for w, _ in weights],
        out_specs=[pl.BlockSpec((TB, GROUP), lambda b, i: (b * nb + i, 0))]
        + [cast_slab(w, tile) for w, tile in weights],
        out_shape=[jax.ShapeDtypeStruct((t, GROUP), BF16)]
        + [jax.ShapeDtypeStruct(cast_shape(w, tile), BF16) for w, tile in weights],
        scratch_shapes=[
            pltpu.VMEM((HEADS, HEAD_DIM, HEAD_DIM), F32),
            pltpu.VMEM((TB, GROUP), BF16),
            pltpu.VMEM((HEADS, TB, CHUNK), BF16),
            pltpu.VMEM((TB // CHUNK, HEADS, HEAD_DIM, HEAD_DIM), F32),
            pltpu.VMEM((TB // CHUNK, 1, GROUP), F32),
        ],
        compiler_params=_params("arbitrary", "arbitrary"),
        name="hgrn2",
    )(proj, proj, proj, proj, lb, gn, *[w for w, _ in weights])


OUT_COLS = 512


def _outproj_kernel(x_ref, a_ref, r_ref, w_ref, o_ref):
    mixed = jnp.concatenate([a_ref[...], r_ref[...]], axis=1)
    for n in range(0, o_ref.shape[1], OUT_COLS):
        cols = slice(n, n + OUT_COLS)
        o_ref[:, cols] = x_ref[:, cols] + jnp.dot(mixed, w_ref[:, cols],
                                                  preferred_element_type=F32)


def _outproj(x2d, attn, rec, w_bf16, tm):
    t, d = x2d.shape
    g = attn.shape[1]
    return pl.pallas_call(
        _outproj_kernel,
        grid=(t // tm,),
        in_specs=[
            pl.BlockSpec((tm, d), lambda i: (i, 0)),
            pl.BlockSpec((tm, g), lambda i: (i, 0)),
            pl.BlockSpec((tm, g), lambda i: (i, 0)),
            pl.BlockSpec((2 * g, d), lambda i: (0, 0), pipeline_mode=pl.Buffered(1)),
        ],
        out_specs=pl.BlockSpec((tm, d), lambda i: (i, 0)),
        out_shape=jax.ShapeDtypeStruct((t, d), F32),
        compiler_params=_params("arbitrary"),
        name="outproj",
    )(x2d, attn, rec, w_bf16)


FFN_DOWN_COLS = 512
FFN_HIDDEN_TILE = 512


def _ffn_kernel(x_hbm, g2_ref, wg_hbm, wu_hbm, wd_hbm, gf_ref, out_hbm,
                acc_ref, h_ref, wg_buf, wu_buf, wd_buf, x_sem, o_sem, w_sem):
    _, tm, d = acc_ref.shape
    n_tiles, _, tf = wg_hbm.shape
    n_groups = 1 + (n_tiles - 1) // 2
    n_row_tiles = x_hbm.shape[0] // tm
    col_chunks = [slice(n, n + FFN_DOWN_COLS) for n in range(0, d, FFN_DOWN_COLS)]

    def x_copy(i, slot):
        return pltpu.make_async_copy(x_hbm.at[pl.ds(i * tm, tm), :], acc_ref.at[slot],
                                     x_sem.at[slot])

    def out_copy(i, slot):
        return pltpu.make_async_copy(acc_ref.at[slot], out_hbm.at[pl.ds(i * tm, tm), :],
                                     o_sem.at[slot])

    def group_tiles(k):
        return (0,) if isinstance(k, int) and k == 0 else (2 * k - 1, 2 * k)

    def group_copies(k, slot):
        copies = []
        for p, j in enumerate(group_tiles(k)):
            copies += [
                pltpu.make_async_copy(wg_hbm.at[j], wg_buf.at[slot, p], w_sem.at[0, slot, p]),
                pltpu.make_async_copy(wu_hbm.at[j], wu_buf.at[slot, p], w_sem.at[1, slot, p]),
                pltpu.make_async_copy(wd_hbm.at[pl.ds(j * tf, tf), :],
                                      wd_buf.at[slot, pl.ds(p * tf, tf), :], w_sem.at[2, slot, p])]
        return copies

    def swiglu_down(h, slot, n):
        acts = []
        for p in range(n):
            gate = jnp.dot(h, wg_buf[slot, p], preferred_element_type=F32)
            up = jnp.dot(h, wu_buf[slot, p], preferred_element_type=F32)
            acts.append((gate * jax.nn.sigmoid(gate) * up).astype(BF16))
        act = acts[0] if n == 1 else jnp.concatenate(acts, axis=1)
        wd = wd_buf.at[slot]
        return [jnp.dot(act, wd[:n * tf, cols], preferred_element_type=F32) for cols in col_chunks]

    def row_tile(i, carry):
        slot = lax.rem(i, 2)
        acc = acc_ref.at[slot]
        x_copy(i, slot).wait()

        for c in group_copies(0, 0):
            c.wait()
        for c in group_copies(1, 1):
            c.start()
        g2 = g2_ref[...]
        for rows in _row_slabs(tm):
            h = _rms_rows(acc[rows, :], g2).astype(h_ref.dtype)
            h_ref[rows, :] = h
            for cols, y in zip(col_chunks, swiglu_down(h, 0, 1)):
                acc[rows, cols] += y

        @pl.when(i >= 1)
        def _():
            out_copy(i - 1, 1 - slot).wait()

        @pl.when(i + 1 < n_row_tiles)
        def _():
            x_copy(i + 1, 1 - slot).start()

        def pair_group(k, c):
            w_slot = lax.rem(k, 2)
            for cp in group_copies(k, w_slot):
                cp.wait()
            for cp in group_copies(k + 1, 1 - w_slot):
                cp.start()
            for cols, y in zip(col_chunks, swiglu_down(h_ref[...], w_slot, 2)):
                acc[:, cols] += y
            return c

        lax.fori_loop(1, n_groups - 1, pair_group, 0)

        last_slot = (n_groups - 1) % 2
        for c in group_copies(n_groups - 1, last_slot):
            c.wait()

        @pl.when(i + 1 < n_row_tiles)
        def _():
            for c in group_copies(0, 0):
                c.start()

        gf = gf_ref[...]
        for rows in _row_slabs(tm):
            ss = jnp.zeros((EDGE_ROWS, 1), F32)
            for cols, y in zip(col_chunks, swiglu_down(h_ref[rows, :], last_slot, 2)):
                o = acc[rows, cols] + y
                acc[rows, cols] = o
                ss = ss + (o * o).sum(axis=-1, keepdims=True)
            acc[rows, :] = acc[rows, :] * lax.rsqrt(ss / d + EPS) * gf
        out_copy(i, slot).start()
        return carry

    x_copy(0, 0).start()
    for c in group_copies(0, 0):
        c.start()
    lax.fori_loop(0, n_row_tiles, row_tile, 0)
    out_copy(n_row_tiles - 1, (n_row_tiles - 1) % 2).wait()


def _ffn(x1, g2, wg_tiles, wu_tiles, wd, gf, tm):
    t, d = x1.shape
    n_tiles, _, tf = wg_tiles.shape
    assert n_tiles % 2 == 1 and n_tiles >= 7 and (1 + (n_tiles - 1) // 2) % 2 == 0, n_tiles
    assert t % tm == 0 and tm % EDGE_ROWS == 0
    hbm = pl.BlockSpec(memory_space=pl.ANY)
    whole = pl.BlockSpec(memory_space=pltpu.VMEM)
    return pl.pallas_call(
        _ffn_kernel,
        in_specs=[hbm, whole, hbm, hbm, hbm, whole],
        out_specs=hbm,
        out_shape=jax.ShapeDtypeStruct((t, d), F32),
        scratch_shapes=[
            pltpu.VMEM((2, tm, d), F32),
            pltpu.VMEM((tm, d), BF16),
            pltpu.VMEM((2, 2, d, tf), BF16),
            pltpu.VMEM((2, 2, d, tf), BF16),
            pltpu.VMEM((2, 2 * tf, d), BF16),
            pltpu.SemaphoreType.DMA((2,)),
            pltpu.SemaphoreType.DMA((2,)),
            pltpu.SemaphoreType.DMA((3, 2, 2)),
        ],
        compiler_params=pltpu.CompilerParams(vmem_limit_bytes=VMEM_LIMIT_BYTES),
        name="ffn",
    )(x1, g2, wg_tiles, wu_tiles, wd, gf)


def kernel(x, norm1_gain, w_in, rel_bias, lower_bounds, grn_norm_gain, attn_out_gain, w_out,
           norm2_gain, w_gate, w_up, w_down, final_gain):
    batch, seq, d = x.shape
    depth = w_in.shape[0]
    assert seq % TB == 0 and seq % QB == 0 and w_in.shape[2] == 7 * GROUP and d == 2 * GROUP
    lb_all = jnp.cumsum(jax.nn.softmax(lower_bounds.astype(F32), axis=0), axis=0)
    xf = x.reshape(batch * seq, d)
    col_scale = jnp.concatenate([jnp.full((1, GROUP), QUERY_SCALE, F32),
                                 jnp.ones((1, w_in.shape[2] - GROUP), F32)], axis=1)
    for l in range(depth):
        proj = _inproj(xf, norm1_gain[l][None, :], w_in[l].astype(BF16), col_scale,
                       tm=1024, tn=1024)
        attn = _attention(proj, _bias_rows(rel_bias[l]), attn_out_gain[l][None, :], batch, seq)
        rec, wo, wg, wu, wd = _hgrn(proj, lb_all[l][None, :], grn_norm_gain[l][None, :], batch, seq,
                                    [(w_out[l], None), (w_gate[l], FFN_HIDDEN_TILE),
                                     (w_up[l], FFN_HIDDEN_TILE), (w_down[l], None)])
        x1 = _outproj(xf, attn, rec, wo, tm=1024)
        last = l == depth - 1
        assert last, "the final rmsnorm is fused into the last layer's FFN"
        xf = _ffn(x1, norm2_gain[l][None, :], wg, wu, wd, final_gain[None, :], tm=1024)
    return xf.reshape(batch, seq, d)
```

```python
import functools

import jax
import jax.numpy as jnp
from jax import lax
from jax.experimental import pallas as pl
from jax.experimental.pallas import tpu as pltpu

F32 = jnp.float32
BF16 = jnp.bfloat16

EPS = 1e-6
CHUNK = 64
N_LEFT = 8
MAX_REL = 128
HEADS = 8
HEAD_DIM = 128
GROUP = HEADS * HEAD_DIM

VMEM_LIMIT_BYTES = 56 * 1024 * 1024
BF16_SUBLANES = 16
NEG = -1e30
LOG2E = 1.4426950408889634
QUERY_SCALE = HEAD_DIM ** -0.5 * LOG2E


def _params(*sem):
    return pltpu.CompilerParams(dimension_semantics=sem, vmem_limit_bytes=VMEM_LIMIT_BYTES)


def _rms_rows(x, gain):
    ms = jnp.mean(x * x, axis=-1, keepdims=True)
    return x * lax.rsqrt(ms + EPS) * gain


EDGE_ROWS = 256


def _row_slabs(n_rows):
    return [slice(r, r + EDGE_ROWS) for r in range(0, n_rows, EDGE_ROWS)]


def _inproj_kernel(x_ref, gain_ref, w_ref, cs_ref, o_ref, h_ref):
    j = pl.program_id(1)
    col_scale = cs_ref[...]

    def project(h):
        return (jnp.dot(h, w_ref[...], preferred_element_type=F32) * col_scale).astype(o_ref.dtype)

    @pl.when(j == 0)
    def _():
        gain = gain_ref[...]
        for rows in _row_slabs(x_ref.shape[0]):
            h = _rms_rows(x_ref[rows, :], gain).astype(h_ref.dtype)
            h_ref[rows, :] = h
            o_ref[rows, :] = project(h)

    @pl.when(j > 0)
    def _():
        o_ref[...] = project(h_ref[...])


def _inproj(x2d, gain, w_bf16, col_scale, tm, tn):
    t, d = x2d.shape
    n = w_bf16.shape[1]
    return pl.pallas_call(
        _inproj_kernel,
        grid=(t // tm, n // tn),
        in_specs=[
            pl.BlockSpec((tm, d), lambda i, j: (i, 0)),
            pl.BlockSpec((1, d), lambda i, j: (0, 0)),
            pl.BlockSpec((d, tn), lambda i, j: (0, j)),
            pl.BlockSpec((1, tn), lambda i, j: (0, j)),
        ],
        out_specs=pl.BlockSpec((tm, tn), lambda i, j: (i, j)),
        out_shape=jax.ShapeDtypeStruct((t, n), BF16),
        scratch_shapes=[pltpu.VMEM((tm, d), BF16)],
        compiler_params=_params("arbitrary", "arbitrary"),
        name="inproj",
    )(x2d, gain, w_bf16, col_scale)


QB = 256
ATTN_ROWS = 1024
N_SEG = 1 + (N_LEFT * CHUNK) // QB
WIN = N_SEG * QB
ROLL_W = 1024


def _bias_rows(rel_bias):
    h = rel_bias.shape[0]
    far_past = rel_bias[:, 2 * MAX_REL:2 * MAX_REL + 1]
    far_future = rel_bias[:, 0:1]
    n_head = (WIN - QB) - MAX_REL
    n_tail = WIN - n_head - (2 * MAX_REL + 1)
    return jnp.concatenate([
        jnp.broadcast_to(far_past, (h, n_head)),
        rel_bias[:, ::-1],
        jnp.broadcast_to(far_future, (h, n_tail)),
        jnp.broadcast_to(far_past, (h, ROLL_W - WIN)),
    ], axis=1).astype(F32)


def _attn_kernel(q_ref, kp_ref, kc_ref, vp_ref, vc_ref, brow_ref, gain_ref,
                 o_ref, bias_ref, acc_ref, s_ref):
    step = pl.program_id(1)
    n_blocks = ATTN_ROWS // QB
    first_granule = n_blocks - (N_SEG - 1)

    @pl.when((pl.program_id(0) == 0) & (step == 0))
    def _():
        qc = lax.broadcasted_iota(jnp.int32, (QB, WIN), 0) // CHUNK
        kc = lax.broadcasted_iota(jnp.int32, (QB, WIN), 1) // CHUNK
        band = (kc >= qc) & (kc <= qc + N_LEFT)
        for h in range(HEADS):
            row = jnp.broadcast_to(brow_ref[h:h + 1, :] * LOG2E, (QB, ROLL_W))
            toeplitz = pltpu.roll(row, 0, 1, stride=1, stride_axis=0)
            bias_ref[h] = jnp.where(band, toeplitz[:, :WIN], NEG)

    def granule(prev_ref, cur_ref, r, sl):
        ref = (prev_ref, cur_ref)[r // n_blocks]
        start = (r % n_blocks) * QB
        return ref[start:start + QB, sl]

    def lane_halves(op, x):
        while x.shape[1] > HEAD_DIM:
            half = x.shape[1] // 2
            x = op(x[:, :half], x[:, half:])
        return x

    def attend(blk, segs):
        rows = slice(blk * QB, (blk + 1) * QB)
        m = []
        for h in range(HEADS):
            sl = slice(h * HEAD_DIM, (h + 1) * HEAD_DIM)
            qh = q_ref[rows, sl]
            mx = None
            for j in segs:
                cols = slice(j * QB, (j + 1) * QB)
                sj = lax.dot_general(qh, granule(kp_ref, kc_ref, first_granule + blk + j, sl),
                                     (((1,), (1,)), ((), ())),
                                     preferred_element_type=F32) + bias_ref[h, :, cols]
                s_ref[h, :, cols] = sj
                mx = sj if mx is None else jnp.maximum(mx, sj)
            m.append(lane_halves(jnp.maximum, mx).max(axis=-1, keepdims=True))
        sq = jnp.zeros((QB, HEAD_DIM), F32)
        ones = jnp.ones((QB, HEAD_DIM), BF16)
        for h in range(HEADS):
            sl = slice(h * HEAD_DIM, (h + 1) * HEAD_DIM)
            p = jnp.concatenate(
                [jnp.exp2(s_ref[h, :, j * QB:(j + 1) * QB] - m[h]).astype(BF16) for j in segs], axis=1)
            v1 = jnp.concatenate(
                [jnp.concatenate([granule(vp_ref, vc_ref, first_granule + blk + j, sl), ones], axis=1)
                 for j in segs], axis=0)
            pv = jnp.dot(p, v1, preferred_element_type=F32)
            o = pv[:, :HEAD_DIM] * (1.0 / pv[:, HEAD_DIM:])
            acc_ref[rows, sl] = o
            sq = sq + o * o
        inv = lax.rsqrt(sq.sum(axis=-1, keepdims=True) / GROUP + EPS)
        o_ref[rows, :] = (acc_ref[rows, :] * inv * gain_ref[...]).astype(o_ref.dtype)

    @pl.when(step == 0)
    def _():
        for blk in range(n_blocks):
            attend(blk, tuple(range(max(N_SEG - 1 - blk, 0), N_SEG)))

    @pl.when(step > 0)
    def _():
        for blk in range(n_blocks):
            attend(blk, tuple(range(N_SEG)))


def _attention(proj, brow, gain, batch, seq):
    t = proj.shape[0]
    nb = seq // ATTN_ROWS
    assert seq % ATTN_ROWS == 0 and ATTN_ROWS % QB == 0 and (N_SEG - 1) * QB <= ATTN_ROWS

    def rows(col_block, back=0):
        return pl.BlockSpec((ATTN_ROWS, GROUP),
                            lambda b, i: (b * nb + jnp.maximum(i - back, 0), col_block))

    return pl.pallas_call(
        _attn_kernel,
        grid=(batch, nb),
        in_specs=[rows(0), rows(1, back=1), rows(1), rows(2, back=1), rows(2),
                  pl.BlockSpec((HEADS, ROLL_W), lambda b, i: (0, 0)),
                  pl.BlockSpec((1, GROUP), lambda b, i: (0, 0))],
        out_specs=rows(0),
        out_shape=jax.ShapeDtypeStruct((t, GROUP), BF16),
        scratch_shapes=[pltpu.VMEM((HEADS, QB, WIN), F32),
                        pltpu.VMEM((ATTN_ROWS, GROUP), F32),
                        pltpu.VMEM((HEADS, QB, WIN), F32)],
        compiler_params=_params("arbitrary", "arbitrary"),
        name="attention",
    )(proj, proj, proj, proj, proj, brow, gain)


TB = 512


def _silu(x):
    hx = 0.5 * x
    return hx + hx * jnp.tanh(hx)


def _split3(x):
    hi = x.astype(BF16)
    r1 = x - hi.astype(F32)
    mid = r1.astype(BF16)
    lo = (r1 - mid.astype(F32)).astype(BF16)
    return hi, mid, lo


def _hgrn_kernel(n_cast, q_ref, f_ref, i_ref, g_ref, lb_ref, gn_ref, *refs):
    cast_in, (o_ref, *cast_out) = refs[:n_cast], refs[n_cast:2 * n_cast + 1]
    st_ref, qd_ref, a_ref, u_ref, dec_ref = refs[2 * n_cast + 1:]

    for src, dst in zip(cast_in, cast_out):
        if len(dst.shape) == 2:
            dst[...] = src[...].astype(dst.dtype)
        else:
            width = dst.shape[2]
            for n in range(dst.shape[0]):
                dst[n] = src[:, n * width:(n + 1) * width].astype(dst.dtype)

    @pl.when(pl.program_id(1) == 0)
    def _():
        st_ref[...] = jnp.zeros_like(st_ref)

    lb = lb_ref[...]
    f_mid = 0.5 * (1.0 + lb)
    f_amp = 0.5 * (1.0 - lb)
    gn = gn_ref[...]
    r_i = lax.broadcasted_iota(jnp.int32, (CHUNK, CHUNK), 0)
    c_i = lax.broadcasted_iota(jnp.int32, (CHUNK, CHUNK), 1)
    causal = r_i >= c_i
    tril = causal.astype(BF16)
    tril3 = jnp.concatenate([tril, tril, tril], axis=1)

    def chunk_rows(c):
        return slice(c * CHUNK, (c + 1) * CHUNK)

    def local_terms(c):
        rows = chunk_rows(c)
        q = _silu(q_ref[rows, :].astype(F32))
        f = f_mid + f_amp * jnp.tanh(0.5 * f_ref[rows, :].astype(F32))
        k = 1.0 - f
        b = jnp.dot(tril3, jnp.concatenate(_split3(jnp.log2(f)), axis=0),
                    preferred_element_type=F32)
        decay = jnp.exp2(b[CHUNK - 1:CHUNK, :])
        q_dec = (q * jnp.exp2(b)).astype(BF16)
        k_undecayed = k * jnp.exp2(-b)
        k_intra = k_undecayed.astype(BF16)
        k_state = (k_undecayed * decay).astype(BF16)
        qd_ref[rows, :] = q_dec
        dec_ref[c] = decay
        for h in range(HEADS):
            sl = slice(h * HEAD_DIM, (h + 1) * HEAD_DIM)
            a = lax.dot_general(q_dec[:, sl], k_intra[:, sl], (((1,), (1,)), ((), ())),
                                preferred_element_type=F32)
            a_ref[h, rows, :] = jnp.where(causal, a, 0.0).astype(BF16)
            u_ref[c, h] = lax.dot_general(i_ref[rows, sl], k_state[:, sl], (((0,), (0,)), ((), ())),
                                          preferred_element_type=F32)

    def outputs(c):
        rows = chunk_rows(c)
        decay = dec_ref[c]
        for h in range(HEADS):
            sl = slice(h * HEAD_DIM, (h + 1) * HEAD_DIM)
            st = st_ref[h]
            o = jnp.dot(a_ref[h, rows, :], i_ref[rows, sl], preferred_element_type=F32)
            o = o + lax.dot_general(qd_ref[rows, sl], st.astype(BF16), (((1,), (1,)), ((), ())),
                                    preferred_element_type=F32)
            st_ref[h] = st * decay[:, sl] + u_ref[c, h]
            o = _rms_rows(o, gn)
            o_ref[rows, sl] = (o * _silu(g_ref[rows, sl].astype(F32))).astype(o_ref.dtype)

    for c in range(TB // CHUNK):
        local_terms(c)
    for c in range(TB // CHUNK):
        outputs(c)


def _hgrn(proj, lb, gn, batch, seq, weights):
    t = proj.shape[0]
    nb = seq // TB
    steps = batch * nb
    for w, _ in weights:
        assert w.shape[0] % (steps * BF16_SUBLANES) == 0, w.shape

    def col(cb):
        return pl.BlockSpec((TB, GROUP), lambda b, i: (b * nb + i, cb))

    def slab(w):
        return pl.BlockSpec((w.shape[0] // steps, w.shape[1]), lambda b, i: (b * nb + i, 0))

    def cast_shape(w, tile):
        return w.shape if tile is None else (w.shape[1] // tile, w.shape[0], tile)

    def cast_slab(w, tile):
        if tile is None:
            return slab(w)
        return pl.BlockSpec((w.shape[1] // tile, w.shape[0] // steps, tile),
                            lambda b, i: (0, b * nb + i, 0))

    return pl.pallas_call(
        functools.partial(_hgrn_kernel, len(weights)),
        grid=(batch, nb),
        in_specs=[col(3), col(4), col(5), col(6),
                  pl.BlockSpec((1, GROUP), lambda b, i: (0, 0)),
                  pl.BlockSpec((1, HEAD_DIM), lambda b, i: (0, 0))]
        + [slab(w) for w, _ in weights],
        out_specs=[pl.BlockSpec((TB, GROUP), lambda b, i: (b * nb + i, 0))]
        + [cast_slab(w, tile) for w, tile in weights],
        out_shape=[jax.ShapeDtypeStruct((t, GROUP), BF16)]
        + [jax.ShapeDtypeStruct(cast_shape(w, tile), BF16) for w, tile in weights],
        scratch_shapes=[
            pltpu.VMEM((HEADS, HEAD_DIM, HEAD_DIM), F32),
            pltpu.VMEM((TB, GROUP), BF16),
            pltpu.VMEM((HEADS, TB, CHUNK), BF16),
            pltpu.VMEM((TB // CHUNK, HEADS, HEAD_DIM, HEAD_DIM), F32),
            pltpu.VMEM((TB // CHUNK, 1, GROUP), F32),
        ],
        compiler_params=_params("arbitrary", "arbitrary"),
        name="hgrn2",
    )(proj, proj, proj, proj, lb, gn, *[w for w, _ in weights])


OUT_COLS = 512


def _outproj_kernel(x_ref, a_ref, r_ref, w_ref, o_ref):
    mixed = jnp.concatenate([a_ref[...], r_ref[...]], axis=1)
    for n in range(0, o_ref.shape[1], OUT_COLS):
        cols = slice(n, n + OUT_COLS)
        o_ref[:, cols] = x_ref[:, cols] + jnp.dot(mixed, w_ref[:, cols],
                                                  preferred_element_type=F32)


def _outproj(x2d, attn, rec, w_bf16, tm):
    t, d = x2d.shape
    g = attn.shape[1]
    return pl.pallas_call(
        _outproj_kernel,
        grid=(t // tm,),
        in_specs=[
            pl.BlockSpec((tm, d), lambda i: (i, 0)),
            pl.BlockSpec((tm, g), lambda i: (i, 0)),
            pl.BlockSpec((tm, g), lambda i: (i, 0)),
            pl.BlockSpec((2 * g, d), lambda i: (0, 0), pipeline_mode=pl.Buffered(1)),
        ],
        out_specs=pl.BlockSpec((tm, d), lambda i: (i, 0)),
        out_shape=jax.ShapeDtypeStruct((t, d), F32),
        compiler_params=_params("arbitrary"),
        name="outproj",
    )(x2d, attn, rec, w_bf16)


FFN_DOWN_COLS = 512
FFN_HIDDEN_TILE = 512


def _ffn_kernel(x_hbm, g2_ref, wg_hbm, wu_hbm, wd_hbm, gf_ref, out_hbm,
                acc_ref, h_ref, wg_buf, wu_buf, wd_buf, x_sem, o_sem, w_sem):
    _, tm, d = acc_ref.shape
    n_tiles, _, tf = wg_hbm.shape
    n_groups = 1 + (n_tiles - 1) // 2
    n_row_tiles = x_hbm.shape[0] // tm
    col_chunks = [slice(n, n + FFN_DOWN_COLS) for n in range(0, d, FFN_DOWN_COLS)]

    def x_copy(i, slot):
        return pltpu.make_async_copy(x_hbm.at[pl.ds(i * tm, tm), :], acc_ref.at[slot],
                                     x_sem.at[slot])

    def out_copy(i, slot):
        return pltpu.make_async_copy(acc_ref.at[slot], out_hbm.at[pl.ds(i * tm, tm), :],
                                     o_sem.at[slot])

    def group_tiles(k):
        return (0,) if isinstance(k, int) and k == 0 else (2 * k - 1, 2 * k)

    def group_copies(k, slot):
        copies = []
        for p, j in enumerate(group_tiles(k)):
            copies += [
                pltpu.make_async_copy(wg_hbm.at[j], wg_buf.at[slot, p], w_sem.at[0, slot, p]),
                pltpu.make_async_copy(wu_hbm.at[j], wu_buf.at[slot, p], w_sem.at[1, slot, p]),
                pltpu.make_async_copy(wd_hbm.at[pl.ds(j * tf, tf), :],
                                      wd_buf.at[slot, pl.ds(p * tf, tf), :], w_sem.at[2, slot, p])]
        return copies

    def swiglu_down(h, slot, n):
        acts = []
        for p in range(n):
            gate = jnp.dot(h, wg_buf[slot, p], preferred_element_type=F32)
            up = jnp.dot(h, wu_buf[slot, p], preferred_element_type=F32)
            acts.append((_silu(gate) * up).astype(BF16))
        act = acts[0] if n == 1 else jnp.concatenate(acts, axis=1)
        wd = wd_buf.at[slot]
        return [jnp.dot(act, wd[:n * tf, cols], preferred_element_type=F32) for cols in col_chunks]

    def row_tile(i, carry):
        slot = lax.rem(i, 2)
        acc = acc_ref.at[slot]
        x_copy(i, slot).wait()

        for c in group_copies(0, 0):
            c.wait()
        for c in group_copies(1, 1):
            c.start()
        g2 = g2_ref[...]
        for rows in _row_slabs(tm):
            h = _rms_rows(acc[rows, :], g2).astype(h_ref.dtype)
            h_ref[rows, :] = h
            for cols, y in zip(col_chunks, swiglu_down(h, 0, 1)):
                acc[rows, cols] += y

        @pl.when(i >= 1)
        def _():
            out_copy(i - 1, 1 - slot).wait()

        @pl.when(i + 1 < n_row_tiles)
        def _():
            x_copy(i + 1, 1 - slot).start()

        def pair_group(k, c):
            w_slot = lax.rem(k, 2)
            for cp in group_copies(k, w_slot):
                cp.wait()
            for cp in group_copies(k + 1, 1 - w_slot):
                cp.start()
            for cols, y in zip(col_chunks, swiglu_down(h_ref[...], w_slot, 2)):
                acc[:, cols] += y
            return c

        lax.fori_loop(1, n_groups - 1, pair_group, 0)

        last_slot = (n_groups - 1) % 2
        for c in group_copies(n_groups - 1, last_slot):
            c.wait()

        @pl.when(i + 1 < n_row_tiles)
        def _():
            for c in group_copies(0, 0):
                c.start()

        gf = gf_ref[...]
        for rows in _row_slabs(tm):
            ss = jnp.zeros((EDGE_ROWS, 1), F32)
            for cols, y in zip(col_chunks, swiglu_down(h_ref[rows, :], last_slot, 2)):
                o = acc[rows, cols] + y
                acc[rows, cols] = o
                ss = ss + (o * o).sum(axis=-1, keepdims=True)
            acc[rows, :] = acc[rows, :] * lax.rsqrt(ss / d + EPS) * gf
        out_copy(i, slot).start()
        return carry

    x_copy(0, 0).start()
    for c in group_copies(0, 0):
        c.start()
    lax.fori_loop(0, n_row_tiles, row_tile, 0)
    out_copy(n_row_tiles - 1, (n_row_tiles - 1) % 2).wait()


def _ffn(x1, g2, wg_tiles, wu_tiles, wd, gf, tm):
    t, d = x1.shape
    n_tiles, _, tf = wg_tiles.shape
    assert n_tiles % 2 == 1 and n_tiles >= 7 and (1 + (n_tiles - 1) // 2) % 2 == 0, n_tiles
    assert t % tm == 0 and tm % EDGE_ROWS == 0
    hbm = pl.BlockSpec(memory_space=pl.ANY)
    whole = pl.BlockSpec(memory_space=pltpu.VMEM)
    return pl.pallas_call(
        _ffn_kernel,
        in_specs=[hbm, whole, hbm, hbm, hbm, whole],
        out_specs=hbm,
        out_shape=jax.ShapeDtypeStruct((t, d), F32),
        scratch_shapes=[
            pltpu.VMEM((2, tm, d), F32),
            pltpu.VMEM((tm, d), BF16),
            pltpu.VMEM((2, 2, d, tf), BF16),
            pltpu.VMEM((2, 2, d, tf), BF16),
            pltpu.VMEM((2, 2 * tf, d), BF16),
            pltpu.SemaphoreType.DMA((2,)),
            pltpu.SemaphoreType.DMA((2,)),
            pltpu.SemaphoreType.DMA((3, 2, 2)),
        ],
        compiler_params=pltpu.CompilerParams(vmem_limit_bytes=VMEM_LIMIT_BYTES),
        name="ffn",
    )(x1, g2, wg_tiles, wu_tiles, wd, gf)


def kernel(x, norm1_gain, w_in, rel_bias, lower_bounds, grn_norm_gain, attn_out_gain, w_out,
           norm2_gain, w_gate, w_up, w_down, final_gain):
    batch, seq, d = x.shape
    depth = w_in.shape[0]
    assert seq % TB == 0 and seq % QB == 0 and w_in.shape[2] == 7 * GROUP and d == 2 * GROUP
    lb_all = jnp.cumsum(jax.nn.softmax(lower_bounds.astype(F32), axis=0), axis=0)
    xf = x.reshape(batch * seq, d)
    col_scale = jnp.concatenate([jnp.full((1, GROUP), QUERY_SCALE, F32),
                                 jnp.ones((1, w_in.shape[2] - GROUP), F32)], axis=1)
    for l in range(depth):
        proj = _inproj(xf, norm1_gain[l][None, :], w_in[l].astype(BF16), col_scale,
                       tm=1024, tn=1792)
        attn = _attention(proj, _bias_rows(rel_bias[l]), attn_out_gain[l][None, :], batch, seq)
        rec, wo, wg, wu, wd = _hgrn(proj, lb_all[l][None, :], grn_norm_gain[l][None, :], batch, seq,
                                    [(w_out[l], None), (w_gate[l], FFN_HIDDEN_TILE),
                                     (w_up[l], FFN_HIDDEN_TILE), (w_down[l], None)])
        x1 = _outproj(xf, attn, rec, wo, tm=1024)
        last = l == depth - 1
        assert last, "the final rmsnorm is fused into the last layer's FFN"
        xf = _ffn(x1, norm2_gain[l][None, :], wg, wu, wd, final_gain[None, :], tm=1024)
    return xf.reshape(batch, seq, d)
```
